```python
import math
import jax, jax.numpy as jnp
from jax import lax
import numpy as np

D_MODEL = 1024
BATCH = 8
SEQ = 2048
DEPTH = 4
DEC_BATCH = 128
DEC_SEQ = 1
PAST_LEN = 2048
PAGE_SIZE = 128

EPS = 1e-6
N_MEM = 256
CONV_W = 4
N_BRANCH = 4
BR_W = D_MODEL // 2
DA_HEADS = 4
DA_HEAD = BR_W // (2 * DA_HEADS)
Q_BLOCK = 128
SSM_HEAD = 64
SSM_HEADS = BR_W // SSM_HEAD
SSM_GROUPS = 2
SSM_STATE = 128
SSM_CONV_CH = BR_W + 2 * SSM_GROUPS * SSM_STATE
SSM_CHUNK = 128
GDN_HEADS = 4
GDN_HEAD = BR_W // GDN_HEADS
GDN_CONV_CH = 3 * BR_W
GDN_CHUNK = 64
ML_HEADS = 4
ML_HEAD = BR_W // ML_HEADS
ML_CHUNK = 128
X_HEADS = 4
X_HEAD = D_MODEL // X_HEADS
D_FF = -(-8 * D_MODEL // (3 * 256)) * 256

IN_SPLITS = (BR_W, BR_W, BR_W,
             BR_W, SSM_CONV_CH, SSM_HEADS,
             GDN_CONV_CH, BR_W, GDN_HEADS, GDN_HEADS,
             BR_W, BR_W, BR_W, BR_W, ML_HEADS, ML_HEADS,
             N_BRANCH * D_MODEL)
IN_W = sum(IN_SPLITS)
STATE_NAMES = ("ssm_conv", "ssm", "gdn_conv", "gdn", "ml_c", "ml_n", "ml_m")

kernel_name = "hybrid_gated_parallel_decoder_step"


def _rmsnorm(x, g):
    xf = x.astype(jnp.float32)
    y = xf * lax.rsqrt(jnp.mean(xf * xf, axis=-1, keepdims=True) + EPS)
    return (y * g.astype(jnp.float32)).astype(x.dtype)


def _l2norm(x):
    xf = x.astype(jnp.float32)
    return xf * lax.rsqrt(jnp.sum(xf * xf, axis=-1, keepdims=True) + EPS)


def _chunk_len(L, c):
    return c if L % c == 0 else L


def _causal_conv(x, buf, w):
    L = x.shape[1]
    xp = jnp.concatenate([buf.astype(x.dtype), x], axis=1)
    y = sum(xp[:, j:j + L] * w[j] for j in range(CONV_W))
    return y, xp[:, L:]


def _diff_core(q, k, v, mask, lam):
    s = jnp.einsum("bqhsd,bkhsd->bhsqk", q, k).astype(jnp.float32) * DA_HEAD ** -0.5
    p = jax.nn.softmax(jnp.where(mask, s, -jnp.inf), axis=-1)
    p = p[:, :, 0] - lam * p[:, :, 1]
    return jnp.einsum("bhqk,bkhe->bqhe", p.astype(v.dtype), v)


def _diff_attn_prompt(q, k, v, lam):
    bsz, L = q.shape[:2]
    qb = _chunk_len(L, Q_BLOCK)
    kpos = jnp.arange(L)

    def block(i):
        q_blk = lax.dynamic_slice_in_dim(q, i * qb, qb, axis=1)
        qpos = i * qb + jnp.arange(qb)
        return _diff_core(q_blk, k, v, kpos[None, :] <= qpos[:, None], lam)

    out = lax.map(block, jnp.arange(L // qb))
    return jnp.moveaxis(out, 0, 1).reshape(bsz, L, DA_HEADS, 2 * DA_HEAD)


def _diff_attn_sample(q, k_new, v_new, ck, cv, page_table, lam):
    db, ls = q.shape[:2]
    kp = ck[page_table].reshape(db, -1, DA_HEADS, 2, DA_HEAD)
    vp = cv[page_table].reshape(db, -1, DA_HEADS, 2 * DA_HEAD)
    past = kp.shape[1]
    k = jnp.concatenate([kp.astype(k_new.dtype), k_new], axis=1)
    v = jnp.concatenate([vp.astype(v_new.dtype), v_new], axis=1)
    kpos = jnp.arange(past + ls)
    qpos = past + jnp.arange(ls)
    return _diff_core(q, k, v, kpos[None, :] <= qpos[:, None], lam)


def _ssd(x, dt, a, bm, cm, h0):
    bsz, L, H, P = x.shape
    N = bm.shape[-1]
    c = _chunk_len(L, SSM_CHUNK)
    nc = L // c
    x = x.reshape(bsz, nc, c, H, P)
    dt = dt.reshape(bsz, nc, c, H)
    bm = bm.reshape(bsz, nc, c, H, N)
    cm = cm.reshape(bsz, nc, c, H, N)
    acs = jnp.cumsum(dt * a, axis=2)
    causal = jnp.tril(jnp.ones((c, c), bool))[None, None, :, :, None]
    decay = jnp.exp(jnp.where(causal, acs[:, :, :, None, :] - acs[:, :, None, :, :], -jnp.inf))
    xdt = x * dt[..., None]
    scores = jnp.einsum("bzihn,bzjhn->bzijh", cm, bm) * decay
    y_diag = jnp.einsum("bzijh,bzjhp->bzihp", scores, xdt)
    to_end = jnp.exp(acs[:, :, -1:, :] - acs)
    states = jnp.einsum("bzjhn,bzjh,bzjhp->bzhpn", bm, to_end, xdt)
    chunk_decay = jnp.exp(acs[:, :, -1, :])

    def step(h, inp):
        st, dc = inp
        return h * dc[:, :, None, None] + st, h

    h_last, h_prev = lax.scan(step, h0, (jnp.moveaxis(states, 1, 0), jnp.moveaxis(chunk_decay, 1, 0)))
    h_prev = jnp.moveaxis(h_prev, 0, 1)
    y_off = jnp.einsum("bzihn,bzhpn->bzihp", cm, h_prev) * jnp.exp(acs)[..., None]
    return (y_diag + y_off).reshape(bsz, L, H, P), h_last


def _mamba2(h_z, h_xbc, h_dt, conv_buf, ssm0, conv_w, conv_b, dt_bias, a_log, d_skip, norm_g):
    bsz, L, _ = h_z.shape
    f32 = jnp.float32
    xbc, new_buf = _causal_conv(h_xbc, conv_buf, conv_w)
    xbc = jax.nn.silu((xbc + conv_b).astype(f32))
    xs, bm, cm = jnp.split(xbc, [BR_W, BR_W + SSM_GROUPS * SSM_STATE], axis=-1)
    rep = SSM_HEADS // SSM_GROUPS
    xs = xs.reshape(bsz, L, SSM_HEADS, SSM_HEAD)
    bm = jnp.repeat(bm.reshape(bsz, L, SSM_GROUPS, SSM_STATE), rep, axis=2)
    cm = jnp.repeat(cm.reshape(bsz, L, SSM_GROUPS, SSM_STATE), rep, axis=2)
    dt = jax.nn.softplus((h_dt + dt_bias).astype(f32))
    a = -jnp.exp(a_log.astype(f32))
    y, ssm1 = _ssd(xs, dt, a, bm, cm, ssm0.astype(f32))
    y = y + d_skip.astype(f32)[:, None] * xs
    y = (y.reshape(bsz, L, BR_W) * jax.nn.silu(h_z.astype(f32))).reshape(bsz, L, SSM_GROUPS, BR_W // SSM_GROUPS)
    y = _rmsnorm(y, norm_g.reshape(SSM_GROUPS, BR_W // SSM_GROUPS)).reshape(bsz, L, BR_W)
    return y.astype(h_z.dtype), new_buf, ssm1.astype(h_z.dtype)


def _gated_delta(q, k, v, g, beta, s0):
    bsz, L, H, K = q.shape
    c = _chunk_len(L, GDN_CHUNK)
    nc = L // c

    def blk(t):
        return jnp.moveaxis(t, 1, 2).reshape(bsz, H, nc, c, *t.shape[3:])

    q = blk(_l2norm(q)) * K ** -0.5
    k = blk(_l2norm(k))
    v, g, beta = blk(v), blk(g), blk(beta)
    gc = jnp.cumsum(g, axis=-1)
    causal = jnp.tril(jnp.ones((c, c), bool))
    decay = jnp.exp(jnp.where(causal, gc[..., :, None] - gc[..., None, :], -jnp.inf))
    kb = k * beta[..., None]
    a_low = jnp.where(jnp.tril(jnp.ones((c, c), bool), -1),
                      jnp.einsum("bhzid,bhzjd->bhzij", kb, k) * decay, 0.0)
    eye = jnp.eye(c, dtype=a_low.dtype)
    t_inv = lax.linalg.triangular_solve(a_low + eye, jnp.broadcast_to(eye, a_low.shape),
                                        left_side=True, lower=True)
    u = t_inv @ (v * beta[..., None])
    w = t_inv @ (kb * jnp.exp(gc)[..., None])
    attn = jnp.einsum("bhzid,bhzjd->bhzij", q, k) * decay
    qg = q * jnp.exp(gc)[..., None]
    kg = k * jnp.exp(gc[..., -1:] - gc)[..., None]
    g_last = jnp.exp(gc[..., -1])

    def step(S, inp):
        u_z, w_z, qg_z, kg_z, at_z, gl_z = inp
        v_new = u_z - w_z @ S
        o = qg_z @ S + at_z @ v_new
        S = S * gl_z[..., None, None] + jnp.swapaxes(kg_z, -1, -2) @ v_new
        return S, o

    s1, o = lax.scan(step, s0, tuple(jnp.moveaxis(t, 2, 0) for t in (u, w, qg, kg, attn, g_last)))
    o = jnp.moveaxis(o, 0, 2).reshape(bsz, H, L, -1)
    return jnp.moveaxis(o, 1, 2), s1


def _gdn(h_qkv, h_z, h_a, h_b, conv_buf, s0, conv_w, dt_bias, a_log, norm_g):
    bsz, L, _ = h_z.shape
    f32 = jnp.float32
    qkv, new_buf = _causal_conv(h_qkv, conv_buf, conv_w)
    qkv = jax.nn.silu(qkv.astype(f32)).reshape(bsz, L, 3, GDN_HEADS, GDN_HEAD)
    g = -jnp.exp(a_log.astype(f32)) * jax.nn.softplus((h_a + dt_bias).astype(f32))
    beta = jax.nn.sigmoid(h_b.astype(f32))
    o, s1 = _gated_delta(qkv[:, :, 0], qkv[:, :, 1], qkv[:, :, 2], g, beta, s0.astype(f32))
    z = h_z.reshape(bsz, L, GDN_HEADS, GDN_HEAD).astype(f32)
    o = _rmsnorm(o, norm_g) * jax.nn.silu(z)
    return o.reshape(bsz, L, BR_W).astype(h_z.dtype), new_buf, s1.astype(h_z.dtype)


def _mlstm_chunked(q, k, v, i_pre, logf, c0, n0, m0):
    bsz, L, H, K = q.shape
    c = _chunk_len(L, ML_CHUNK)
    nc = L // c

    def blk(t):
        return jnp.moveaxis(t, 1, 2).reshape(bsz, H, nc, c, *t.shape[3:])

    q, k, v, i_pre, logf = blk(q), blk(k) * K ** -0.5, blk(v), blk(i_pre), blk(logf)
    b = jnp.cumsum(logf, axis=-1)
    causal = jnp.tril(jnp.ones((c, c), bool))
    dmat = jnp.where(causal, b[..., :, None] - b[..., None, :] + i_pre[..., None, :], -jnp.inf)
    dmax = jnp.max(dmat, axis=-1)
    qk = jnp.einsum("bhzid,bhzjd->bhzij", q, k)
    to_end = b[..., -1:] - b + i_pre

    def step(carry, inp):
        C, n, m = carry
        q_z, k_z, v_z, b_z, d_z, dm_z, qk_z, te_z = inp
        m_t = jnp.maximum(b_z + m[..., None], dm_z)
        w_prev = jnp.exp(b_z + m[..., None] - m_t)
        s = qk_z * jnp.exp(d_z - m_t[..., None])
        num = w_prev[..., None] * jnp.einsum("bhid,bhde->bhie", q_z, C) + jnp.einsum("bhij,bhje->bhie", s, v_z)
        den = w_prev * jnp.einsum("bhid,bhd->bhi", q_z, n) + jnp.sum(s, axis=-1)
        hid = num / jnp.maximum(jnp.abs(den), jnp.exp(-m_t))[..., None]
        m_new = m_t[..., -1]
        w_c = jnp.exp(b_z[..., -1] + m - m_new)
        w_j = jnp.exp(te_z - m_new[..., None])
        C = C * w_c[..., None, None] + jnp.einsum("bhjd,bhje,bhj->bhde", k_z, v_z, w_j)
        n = n * w_c[..., None] + jnp.einsum("bhjd,bhj->bhd", k_z, w_j)
        return (C, n, m_new), hid

    xs = tuple(jnp.moveaxis(t, 2, 0) for t in (q, k, v, b, dmat, dmax, qk, to_end))
    (C, n, m), hid = lax.scan(step, (c0, n0, m0), xs)
    hid = jnp.moveaxis(hid, 0, 2).reshape(bsz, H, L, -1)
    return jnp.moveaxis(hid, 1, 2), C, n, m


def _mlstm(h_q, h_k, h_v, h_o, h_i, h_f, c0, n0, m0, i_bias, f_bias, norm_g):
    bsz, L, _ = h_q.shape
    f32 = jnp.float32

    def heads(t):
        return t.reshape(bsz, L, ML_HEADS, ML_HEAD).astype(f32)

    i_pre = (h_i + i_bias).astype(f32)
    logf = jax.nn.log_sigmoid((h_f + f_bias).astype(f32))
    hid, c1, n1, m1 = _mlstm_chunked(heads(h_q), heads(h_k), heads(h_v), i_pre, logf,
                                     c0.astype(f32), n0.astype(f32), m0.astype(f32))
    out = _rmsnorm(hid, norm_g) * jax.nn.sigmoid(heads(h_o))
    dt = h_q.dtype
    return out.reshape(bsz, L, BR_W).astype(dt), c1.astype(dt), n1.astype(dt), m1.astype(dt)


def _mixer(l, h, p, st, past):
    bsz, L, _ = h.shape
    f32 = jnp.float32
    offs = np.cumsum(IN_SPLITS)[:-1].tolist()
    (da_q, da_k, da_v, s_z, s_xbc, s_dt, g_qkv, g_z, g_a, g_b,
     m_q, m_k, m_v, m_o, m_i, m_f, gate_pre) = jnp.split(h @ p["w_in"][l], offs, axis=-1)

    q = da_q.reshape(bsz, L, DA_HEADS, 2, DA_HEAD)
    k = da_k.reshape(bsz, L, DA_HEADS, 2, DA_HEAD)
    v = da_v.reshape(bsz, L, DA_HEADS, 2 * DA_HEAD)
    lam_init = 0.8 - 0.6 * math.exp(-0.3 * l)
    lam = (jnp.exp(jnp.sum(p["da_lq1"][l].astype(f32) * p["da_lk1"][l].astype(f32)))
           - jnp.exp(jnp.sum(p["da_lq2"][l].astype(f32) * p["da_lk2"][l].astype(f32))) + lam_init)
    if past is None:
        o = _diff_attn_prompt(q, k, v, lam)
    else:
        o = _diff_attn_sample(q, k, v, past[0][l], past[1][l], past[2], lam)
    o_da = (_rmsnorm(o, p["da_sub_g"][l]) * (1.0 - lam_init)).reshape(bsz, L, BR_W)

    o_ssm, ssm_conv, ssm = _mamba2(s_z, s_xbc, s_dt, st["ssm_conv"][l], st["ssm"][l],
                                   p["ssm_conv_w"][l], p["ssm_conv_b"][l], p["ssm_dt_bias"][l],
                                   p["ssm_a_log"][l], p["ssm_d"][l], p["ssm_norm_g"][l])
    o_gdn, gdn_conv, gdn = _gdn(g_qkv, g_z, g_a, g_b, st["gdn_conv"][l], st["gdn"][l],
                                p["gdn_conv_w"][l], p["gdn_dt_bias"][l], p["gdn_a_log"][l], p["gdn_norm_g"][l])
    o_ml, ml_c, ml_n, ml_m = _mlstm(m_q, m_k, m_v, m_o, m_i, m_f, st["ml_c"][l], st["ml_n"][l], st["ml_m"][l],
                                    p["ml_i_bias"][l], p["ml_f_bias"][l], p["ml_norm_g"][l])

    br = jnp.stack([o_da, o_ssm, o_gdn, o_ml], axis=2)
    gates = jax.nn.sigmoid(gate_pre).reshape(bsz, L, N_BRANCH, D_MODEL)
    merged = jnp.sum(gates * jnp.einsum("btne,ned->btnd", br, p["w_branch"][l]), axis=2)
    new_st = dict(ssm_conv=ssm_conv, ssm=ssm, gdn_conv=gdn_conv, gdn=gdn, ml_c=ml_c, ml_n=ml_n, ml_m=ml_m)
    return merged @ p["w_out"][l], k, v, new_st


def _cross_attn(h, mk, mv, w_q, w_o):
    bsz, L, _ = h.shape
    q = (h @ w_q).reshape(bsz, L, X_HEADS, X_HEAD)
    s = jnp.einsum("bqhd,bkhd->bhqk", q, mk.astype(q.dtype)).astype(jnp.float32) * X_HEAD ** -0.5
    pr = jax.nn.softmax(s, axis=-1)
    o = jnp.einsum("bhqk,bkhd->bqhd", pr.astype(q.dtype), mv.astype(q.dtype))
    return o.reshape(bsz, L, X_HEADS * X_HEAD) @ w_o


def _swiglu(h, w_gu, w_down):
    gate, up = jnp.split(h @ w_gu, 2, axis=-1)
    return (jax.nn.silu(gate) * up) @ w_down


def _trunk(x, p, mem_k, mem_v, st, past):
    new = {name: [] for name in ("k", "v") + STATE_NAMES}
    for l in range(DEPTH):
        mix, k_rows, v_rows, st_l = _mixer(l, _rmsnorm(x, p["g_mix"][l]), p, st, past)
        x = x + mix
        x = x + _cross_attn(_rmsnorm(x, p["g_cross"][l]), mem_k[l], mem_v[l], p["w_cq"][l], p["w_co"][l])
        x = x + _swiglu(_rmsnorm(x, p["g_ffn"][l]), p["w_gu"][l], p["w_down"][l])
        new["k"].append(k_rows)
        new["v"].append(v_rows)
        for name in STATE_NAMES:
            new[name].append(st_l[name])
    return _rmsnorm(x, p["g_final"]), {name: jnp.stack(vals) for name, vals in new.items()}


def setup_inputs(seed: int = 0) -> dict:
    key = jax.random.key(seed)
    keys = iter(jax.random.split(key, 64))
    f32 = jnp.float32

    def nrm(shape, scale=1.0):
        return jax.random.normal(next(keys), shape, f32) * scale

    def gain(shape):
        return 1.0 + nrm(shape, 0.02)

    def dt_bias(shape):
        dt = jnp.exp(jax.random.uniform(next(keys), shape, f32, math.log(1e-3), math.log(1e-1)))
        return dt + jnp.log(-jnp.expm1(-dt))

    def a_log(shape):
        return jnp.log(jax.random.uniform(next(keys), shape, f32, 1.0, 16.0))

    n_pages = PAST_LEN // PAGE_SIZE
    n_pool = (DEC_BATCH * n_pages * 5 + 3) // 4
    page_table = jax.random.permutation(next(keys), n_pool)[: DEC_BATCH * n_pages]
    page_table = page_table.reshape(DEC_BATCH, n_pages).astype(jnp.int32)
    D = D_MODEL
    return {
        "x_prompt": nrm((BATCH, SEQ, D)),
        "x_sample": nrm((DEC_BATCH, DEC_SEQ, D)),
        "cache_k": nrm((DEPTH, n_pool, PAGE_SIZE, DA_HEADS, 2, DA_HEAD)),
        "cache_v": nrm((DEPTH, n_pool, PAGE_SIZE, DA_HEADS, 2 * DA_HEAD)),
        "cache_mem_k": nrm((DEPTH, DEC_BATCH, N_MEM, X_HEADS, X_HEAD)),
        "cache_mem_v": nrm((DEPTH, DEC_BATCH, N_MEM, X_HEADS, X_HEAD)),
        "state_ssm_conv": nrm((DEPTH, DEC_BATCH, CONV_W - 1, SSM_CONV_CH)),
        "state_ssm": nrm((DEPTH, DEC_BATCH, SSM_HEADS, SSM_HEAD, SSM_STATE), 0.1),
        "state_gdn_conv": nrm((DEPTH, DEC_BATCH, CONV_W - 1, GDN_CONV_CH)),
        "state_gdn": nrm((DEPTH, DEC_BATCH, GDN_HEADS, GDN_HEAD, GDN_HEAD), 0.1),
        "state_mlstm_c": nrm((DEPTH, DEC_BATCH, ML_HEADS, ML_HEAD, ML_HEAD), 0.1),
        "state_mlstm_n": nrm((DEPTH, DEC_BATCH, ML_HEADS, ML_HEAD), 0.1),
        "state_mlstm_m": nrm((DEPTH, DEC_BATCH, ML_HEADS), 0.5),
        "page_table": page_table,
        "mem_prompt": nrm((BATCH, N_MEM, D)),
        "g_mix": gain((DEPTH, D)),
        "w_in": nrm((DEPTH, D, IN_W), D ** -0.5),
        "da_lq1": nrm((DEPTH, DA_HEAD), 0.1),
        "da_lk1": nrm((DEPTH, DA_HEAD), 0.1),
        "da_lq2": nrm((DEPTH, DA_HEAD), 0.1),
        "da_lk2": nrm((DEPTH, DA_HEAD), 0.1),
        "da_sub_g": gain((DEPTH, 2 * DA_HEAD)),
        "ssm_conv_w": nrm((DEPTH, CONV_W, SSM_CONV_CH), CONV_W ** -0.5),
        "ssm_conv_b": nrm((DEPTH, SSM_CONV_CH), 0.02),
        "ssm_dt_bias": dt_bias((DEPTH, SSM_HEADS)),
        "ssm_a_log": a_log((DEPTH, SSM_HEADS)),
        "ssm_d": 1.0 + nrm((DEPTH, SSM_HEADS), 0.1),
        "ssm_norm_g": gain((DEPTH, BR_W)),
        "gdn_conv_w": nrm((DEPTH, CONV_W, GDN_CONV_CH), CONV_W ** -0.5),
        "gdn_dt_bias": dt_bias((DEPTH, GDN_HEADS)),
        "gdn_a_log": a_log((DEPTH, GDN_HEADS)),
        "gdn_norm_g": gain((DEPTH, GDN_HEAD)),
        "ml_i_bias": nrm((DEPTH, ML_HEADS), 0.1),
        "ml_f_bias": jnp.linspace(3.0, 6.0, ML_HEADS, dtype=f32)[None, :] + nrm((DEPTH, ML_HEADS), 0.1),
        "ml_norm_g": gain((DEPTH, ML_HEAD)),
        "w_branch": nrm((DEPTH, N_BRANCH, BR_W, D), BR_W ** -0.5),
        "w_out": nrm((DEPTH, D, D), D ** -0.5),
        "g_cross": gain((DEPTH, D)),
        "g_mem": gain((DEPTH, D)),
        "w_cq": nrm((DEPTH, D, X_HEADS * X_HEAD), D ** -0.5),
        "w_ckv": nrm((DEPTH, D, 2 * X_HEADS * X_HEAD), D ** -0.5),
        "w_co": nrm((DEPTH, X_HEADS * X_HEAD, D), (X_HEADS * X_HEAD) ** -0.5),
        "g_ffn": gain((DEPTH, D)),
        "w_gu": nrm((DEPTH, D, 2 * D_FF), D ** -0.5),
        "w_down": nrm((DEPTH, D_FF, D), D_FF ** -0.5),
        "g_final": gain((D,)),
    }


def reference(x_prompt, x_sample, cache_k, cache_v, cache_mem_k, cache_mem_v,
              state_ssm_conv, state_ssm, state_gdn_conv, state_gdn,
              state_mlstm_c, state_mlstm_n, state_mlstm_m, page_table, mem_prompt,
              g_mix, w_in, da_lq1, da_lk1, da_lq2, da_lk2, da_sub_g,
              ssm_conv_w, ssm_conv_b, ssm_dt_bias, ssm_a_log, ssm_d, ssm_norm_g,
              gdn_conv_w, gdn_dt_bias, gdn_a_log, gdn_norm_g,
              ml_i_bias, ml_f_bias, ml_norm_g,
              w_branch, w_out, g_cross, g_mem, w_cq, w_ckv, w_co,
              g_ffn, w_gu, w_down, g_final):
    p = dict(g_mix=g_mix, w_in=w_in, da_lq1=da_lq1, da_lk1=da_lk1, da_lq2=da_lq2, da_lk2=da_lk2,
             da_sub_g=da_sub_g, ssm_conv_w=ssm_conv_w, ssm_conv_b=ssm_conv_b, ssm_dt_bias=ssm_dt_bias,
             ssm_a_log=ssm_a_log, ssm_d=ssm_d, ssm_norm_g=ssm_norm_g, gdn_conv_w=gdn_conv_w,
             gdn_dt_bias=gdn_dt_bias, gdn_a_log=gdn_a_log, gdn_norm_g=gdn_norm_g,
             ml_i_bias=ml_i_bias, ml_f_bias=ml_f_bias, ml_norm_g=ml_norm_g,
             w_branch=w_branch, w_out=w_out, g_cross=g_cross, w_cq=w_cq, w_co=w_co,
             g_ffn=g_ffn, w_gu=w_gu, w_down=w_down, g_final=g_final)

    pb, n_mem = mem_prompt.shape[:2]
    mkv = jnp.stack([_rmsnorm(mem_prompt, g_mem[l]) @ w_ckv[l] for l in range(DEPTH)])
    mkv = mkv.reshape(DEPTH, pb, n_mem, 2, X_HEADS, X_HEAD)
    p_mem_k = mkv[:, :, :, 0]
    p_mem_v = mkv[:, :, :, 1]
    dt = x_prompt.dtype

    def zero(*s):
        return jnp.zeros((DEPTH, pb) + s, dt)

    st0 = dict(ssm_conv=zero(CONV_W - 1, SSM_CONV_CH), ssm=zero(SSM_HEADS, SSM_HEAD, SSM_STATE),
               gdn_conv=zero(CONV_W - 1, GDN_CONV_CH), gdn=zero(GDN_HEADS, GDN_HEAD, GDN_HEAD),
               ml_c=zero(ML_HEADS, ML_HEAD, ML_HEAD), ml_n=zero(ML_HEADS, ML_HEAD), ml_m=zero(ML_HEADS))
    y_prompt, sp = _trunk(x_prompt, p, p_mem_k, p_mem_v, st0, None)

    st_s = dict(ssm_conv=state_ssm_conv, ssm=state_ssm, gdn_conv=state_gdn_conv, gdn=state_gdn,
                ml_c=state_mlstm_c, ml_n=state_mlstm_n, ml_m=state_mlstm_m)
    y_sample, ss = _trunk(x_sample, p, cache_mem_k, cache_mem_v, st_s, (cache_k, cache_v, page_table))

    return (y_prompt, y_sample,
            sp["k"], sp["v"], p_mem_k, p_mem_v, sp["ssm_conv"], sp["ssm"], sp["gdn_conv"], sp["gdn"],
            sp["ml_c"], sp["ml_n"], sp["ml_m"],
            ss["k"], ss["v"], ss["ssm_conv"], ss["ssm"], ss["gdn_conv"], ss["gdn"],
            ss["ml_c"], ss["ml_n"], ss["ml_m"])
```

```python
import functools
import math

import numpy as np
import jax
import jax.numpy as jnp
from jax import lax
from jax.experimental import pallas as pl
from jax.experimental.pallas import tpu as pltpu

F32 = jnp.float32
BF16 = jnp.bfloat16

D_MODEL = 1024
DEPTH = 4
PAGE_SIZE = 128
EPS = 1e-6
N_MEM = 256
CONV_W = 4
N_BRANCH = 4
BR_W = D_MODEL // 2
DA_HEADS = 4
DA_HEAD = BR_W // (2 * DA_HEADS)
SSM_HEAD = 64
SSM_HEADS = BR_W // SSM_HEAD
SSM_GROUPS = 2
SSM_STATE = 128
SSM_CONV_CH = BR_W + 2 * SSM_GROUPS * SSM_STATE
SSM_CHUNK = 128
GDN_HEADS = 4
GDN_HEAD = BR_W // GDN_HEADS
GDN_CONV_CH = 3 * BR_W
GDN_CHUNK = 64
ML_HEADS = 4
ML_HEAD = BR_W // ML_HEADS
ML_CHUNK = 128
X_HEADS = 4
X_HEAD = D_MODEL // X_HEADS
D_FF = -(-8 * D_MODEL // (3 * 256)) * 256

IN_SPLITS = (BR_W, BR_W, BR_W, BR_W, SSM_CONV_CH, SSM_HEADS, GDN_CONV_CH, BR_W, GDN_HEADS, GDN_HEADS,
             BR_W, BR_W, BR_W, BR_W, ML_HEADS, ML_HEADS, N_BRANCH * D_MODEL)

P_Q, P_K, P_V = 0, BR_W, 2 * BR_W
P_SZ = 3 * BR_W
P_XBC = P_SZ + BR_W
P_GQKV = P_XBC + SSM_CONV_CH
P_GZ = P_GQKV + GDN_CONV_CH
P_MQ = P_GZ + BR_W
P_GATE = P_MQ + 4 * BR_W
P_SMALL = P_GATE + N_BRANCH * D_MODEL
SMALL_W = 256
PACK_W = P_SMALL + SMALL_W
S_DT, S_GA, S_GB, S_MI, S_MF = 0, 8, 12, 16, 20

LANE = 128
VMEM_LIMIT = 56 * 1024 * 1024


def _cparams(*sem):
    return pltpu.CompilerParams(dimension_semantics=sem, vmem_limit_bytes=VMEM_LIMIT)


def _dot(a, b):
    return jnp.dot(a.astype(BF16), b.astype(BF16), preferred_element_type=F32)


def _dot_nt(a, b):
    return lax.dot_general(a.astype(BF16), b.astype(BF16), (((1,), (1,)), ((), ())), preferred_element_type=F32)


def _dot_f32(a, b):
    return jnp.dot(a, b, precision=lax.Precision.HIGHEST, preferred_element_type=F32)


def _dot_split(a, b_bf16):
    hi = a.astype(BF16)
    lo = (a - hi.astype(F32)).astype(BF16)
    return (jnp.dot(hi, b_bf16, preferred_element_type=F32) + jnp.dot(lo, b_bf16, preferred_element_type=F32))


def _sigmoid(x):
    return 1.0 / (1.0 + jnp.exp(-x))


def _silu(x):
    return x * _sigmoid(x)


def _softplus(x):
    return jnp.maximum(x, 0.0) + jnp.log1p(jnp.exp(-jnp.abs(x)))


def _rms(x, g):
    return x * lax.rsqrt(jnp.mean(x * x, axis=-1, keepdims=True) + EPS) * g


def _iota2(shape, dim):
    return lax.broadcasted_iota(jnp.int32, shape, dim)


def _row_to_col(x):
    n = x.shape[1]
    eye = _iota2((n, n), 0) == _iota2((n, n), 1)
    return jnp.sum(jnp.where(eye, jnp.broadcast_to(x, (n, n)), 0.0), axis=1, keepdims=True)


def _col_to_row(x):
    n = x.shape[0]
    eye = _iota2((n, n), 0) == _iota2((n, n), 1)
    return jnp.sum(jnp.where(eye, jnp.broadcast_to(x, (n, n)), 0.0), axis=0, keepdims=True)


def _tril_f32(c):
    return (_iota2((c, c), 0) >= _iota2((c, c), 1)).astype(F32)


def _head_expander(n_heads, width):
    rows = _iota2((LANE, n_heads * width), 0)
    cols = _iota2((LANE, n_heads * width), 1)
    return (rows * width <= cols) & (cols < (rows + 1) * width)


def _lam(lq1, lk1, lq2, lk2, lam_init):
    return (jnp.exp(jnp.sum(lq1[...] * lk1[...], axis=1, keepdims=True))
            - jnp.exp(jnp.sum(lq2[...] * lk2[...], axis=1, keepdims=True)) + lam_init)


def _norm_matmul_kernel(x_ref, g_ref, w_ref, o_ref, h_ref):
    @pl.when(pl.program_id(1) == 0)
    def _():
        h_ref[...] = _rms(x_ref[...], g_ref[...]).astype(BF16)

    o_ref[...] = jnp.dot(h_ref[...], w_ref[...], preferred_element_type=F32)


def _norm_matmul(x, g, w, tm, tn, name):
    m, k = x.shape
    n = w.shape[1]
    return pl.pallas_call(
        _norm_matmul_kernel,
        grid=(m // tm, n // tn),
        in_specs=[pl.BlockSpec((tm, k), lambda i, j: (i, 0)),
                  pl.BlockSpec((1, k), lambda i, j: (0, 0)),
                  pl.BlockSpec((k, tn), lambda i, j: (0, j))],
        out_specs=pl.BlockSpec((tm, tn), lambda i, j: (i, j)),
        out_shape=jax.ShapeDtypeStruct((m, n), F32),
        scratch_shapes=[pltpu.VMEM((tm, k), BF16)],
        compiler_params=_cparams("parallel", "arbitrary"),
        name=name)(x, g.reshape(1, k), w)


def _matmul_res_kernel(x_ref, a_ref, w_ref, o_ref):
    o_ref[...] = x_ref[...] + jnp.dot(a_ref[...].astype(BF16), w_ref[...], preferred_element_type=F32)


def _matmul_residual(x, a, w, tm, name):
    m, n = x.shape
    k = a.shape[1]
    return pl.pallas_call(
        _matmul_res_kernel,
        grid=(m // tm,),
        in_specs=[pl.BlockSpec((tm, n), lambda i: (i, 0)),
                  pl.BlockSpec((tm, k), lambda i: (i, 0)),
                  pl.BlockSpec((k, n), lambda i: (0, 0))],
        out_specs=pl.BlockSpec((tm, n), lambda i: (i, 0)),
        out_shape=jax.ShapeDtypeStruct((m, n), F32),
        compiler_params=_cparams("parallel"),
        name=name)(x, a, w)


def _final_norm_kernel(x_ref, g_ref, o_ref):
    o_ref[...] = _rms(x_ref[...], g_ref[...])


def _final_norm(x, g, tm, name):
    m, n = x.shape
    return pl.pallas_call(
        _final_norm_kernel,
        grid=(m // tm,),
        in_specs=[pl.BlockSpec((tm, n), lambda i: (i, 0)), pl.BlockSpec((1, n), lambda i: (0, 0))],
        out_specs=pl.BlockSpec((tm, n), lambda i: (i, 0)),
        out_shape=jax.ShapeDtypeStruct((m, n), F32),
        compiler_params=_cparams("parallel"),
        name=name)(x, g.reshape(1, n))


def _da_prompt_kernel(lq1, lk1, lq2, lk2, subg_ref, q_ref, k_ref, v_ref, o_ref, kb_ref, vb_ref, *, lam_init, tq):
    qi = pl.program_id(2)

    @pl.when(qi == 0)
    def _():
        kb_ref[...] = k_ref[...].astype(BF16)
        vb_ref[...] = v_ref[...].astype(BF16)

    lam = _lam(lq1, lk1, lq2, lk2, lam_init)
    q = q_ref[...] * (DA_HEAD ** -0.5)
    lane = _iota2((tq, 2 * DA_HEAD), 1)
    q2 = jnp.concatenate([jnp.where(lane < DA_HEAD, q, 0.0).astype(BF16),
                          jnp.where(lane >= DA_HEAD, q, 0.0).astype(BF16)], axis=0)

    def step(j, carry, masked):
        m, l, acc = carry
        start = pl.multiple_of(j * tq, tq)
        kj = kb_ref[pl.ds(start, tq), :]
        vj = vb_ref[pl.ds(start, tq), :]
        s = lax.dot_general(q2, kj, (((1,), (1,)), ((), ())), preferred_element_type=F32)
        if masked:
            row = _iota2((2 * tq, tq), 0)
            col = _iota2((2 * tq, tq), 1)
            row = jnp.where(row >= tq, row - tq, row)
            s = jnp.where(col <= row, s, -jnp.inf)
        m_new = jnp.maximum(m, jnp.max(s, axis=1, keepdims=True))
        alpha = jnp.exp(m - m_new)
        p = jnp.exp(s - m_new)
        l = alpha * l + jnp.sum(p, axis=1, keepdims=True)
        acc = alpha * acc + jnp.dot(p.astype(BF16), vj, preferred_element_type=F32)
        return m_new, l, acc

    init = (jnp.full((2 * tq, 1), -jnp.inf, F32), jnp.zeros((2 * tq, 1), F32), jnp.zeros((2 * tq, 2 * DA_HEAD), F32))
    carry = lax.fori_loop(0, qi, functools.partial(step, masked=False), init)
    _, l, acc = step(qi, carry, True)
    o = acc / l
    od = o[:tq] - lam * o[tq:]
    o_ref[...] = _rms(od, subg_ref[...]) * (1.0 - lam_init)


def _da_prompt(proj, lam_params, sub_g, bsz, seq, lam_init, tq=256):
    nq = seq // tq
    hw = 2 * DA_HEAD
    small = pl.BlockSpec((1, DA_HEAD), lambda b, h, i: (0, 0))
    kv_spec = lambda off: pl.BlockSpec((seq, hw), lambda b, h, i: (b, off // hw + h))
    return pl.pallas_call(
        functools.partial(_da_prompt_kernel, lam_init=lam_init, tq=tq),
        grid=(bsz, DA_HEADS, nq),
        in_specs=[small, small, small, small,
                  pl.BlockSpec((1, hw), lambda b, h, i: (0, 0)),
                  pl.BlockSpec((tq, hw), lambda b, h, i: (b * nq + i, P_Q // hw + h)),
                  kv_spec(P_K), kv_spec(P_V)],
        out_specs=pl.BlockSpec((tq, hw), lambda b, h, i: (b * nq + i, h)),
        out_shape=jax.ShapeDtypeStruct((bsz * seq, BR_W), F32),
        scratch_shapes=[pltpu.VMEM((seq, hw), BF16), pltpu.VMEM((seq, hw), BF16)],
        compiler_params=_cparams("parallel", "parallel", "arbitrary"),
        name="da_prompt")(*lam_params, sub_g.reshape(1, hw), proj, proj, proj)


def _da_decode_kernel(pt_ref, lq1, lk1, lq2, lk2, subg_ref, q_ref, kn_ref, vn_ref, *rest, lam_init, n_pages):
    del pt_ref
    k_refs = rest[:n_pages]
    v_refs = rest[n_pages:2 * n_pages]
    o_ref = rest[2 * n_pages]
    lam = _lam(lq1, lk1, lq2, lk2, lam_init)
    q = q_ref[...] * (DA_HEAD ** -0.5)
    seg = ((_iota2((BR_W, LANE), 0) >> 6) == _iota2((BR_W, LANE), 1)).astype(BF16)
    scores = [_dot_split(k_refs[j][...] * q, seg) for j in range(n_pages)]
    s_new = _dot_split(jnp.broadcast_to(kn_ref[...] * q, (8, BR_W)), seg)[0:1]
    m = s_new
    for s in scores:
        m = jnp.maximum(m, jnp.max(s, axis=0, keepdims=True))
    e_new = jnp.exp(s_new - m)
    es = [jnp.exp(s - m) for s in scores]
    l = e_new
    for e in es:
        l = l + jnp.sum(e, axis=0, keepdims=True)
    lane = _iota2((1, LANE), 1)
    coef = jnp.where((lane & 1) == 0, 1.0, -lam) / l
    coef = jnp.where(lane < 2 * DA_HEADS, coef, 0.0)
    expand = (((_iota2((LANE, BR_W), 0) >> 1) == (_iota2((LANE, BR_W), 1) >> 7))
              & (_iota2((LANE, BR_W), 0) < 2 * DA_HEADS)).astype(BF16)
    acc = _dot_split(jnp.broadcast_to(e_new * coef, (8, LANE)), expand)[0:1] * vn_ref[...]
    for j in range(n_pages):
        acc = acc + jnp.sum(_dot_split(es[j] * coef, expand) * v_refs[j][...], axis=0, keepdims=True)
    g = subg_ref[...]
    hw = 2 * DA_HEAD
    out = [_rms(acc[:, h * hw:(h + 1) * hw], g) * (1.0 - lam_init) for h in range(DA_HEADS)]
    o_ref[...] = jnp.concatenate(out, axis=1)


def _da_decode(proj3, lam_params, sub_g, cache_k, cache_v, page_table, layer, lam_init):
    db = proj3.shape[0]
    n_pages = page_table.shape[1]
    hw = 2 * DA_HEAD
    small = pl.BlockSpec((1, DA_HEAD), lambda b, pt: (0, 0))
    row = lambda off: pl.BlockSpec((None, 1, BR_W), lambda b, pt: (b, 0, off // BR_W))

    def page(j):
        return pl.BlockSpec((None, None, PAGE_SIZE, BR_W), lambda b, pt: (layer, pt[b * n_pages + j], 0, 0))

    grid_spec = pltpu.PrefetchScalarGridSpec(
        num_scalar_prefetch=1,
        grid=(db,),
        in_specs=[small, small, small, small, pl.BlockSpec((1, hw), lambda b, pt: (0, 0)),
                  row(P_Q), row(P_K), row(P_V)]
                 + [page(j) for j in range(n_pages)] + [page(j) for j in range(n_pages)],
        out_specs=pl.BlockSpec((None, 1, BR_W), lambda b, pt: (b, 0, 0)))
    out = pl.pallas_call(
        functools.partial(_da_decode_kernel, lam_init=lam_init, n_pages=n_pages),
        grid_spec=grid_spec,
        out_shape=jax.ShapeDtypeStruct((db, 1, BR_W), F32),
        compiler_params=_cparams("parallel"),
        name="da_decode")(page_table.reshape(-1), *lam_params, sub_g.reshape(1, hw), proj3, proj3, proj3,
                          *([cache_k] * n_pages), *([cache_v] * n_pages))
    return out.reshape(db, BR_W)


def _conv_window(win_ref, x_ref, cw, c, zi):
    @pl.when(zi == 0)
    def _():
        win_ref[0:8, :] = jnp.zeros((8, win_ref.shape[1]), F32)

    @pl.when(zi > 0)
    def _():
        win_ref[0:8, :] = win_ref[c:c + 8, :]

    win_ref[8:8 + c, :] = x_ref[...]
    y = win_ref[5:5 + c, :] * cw[0:1, :]
    for j in range(1, CONV_W):
        y = y + win_ref[5 + j:5 + j + c, :] * cw[j:j + 1, :]
    return y


def _ssd_prompt_kernel(z_ref, xbc_ref, sm_ref, cw_ref, cb_ref, bias_ref, alog_ref, dsk_ref, ng_ref,
                       o_ref, st_ref, win_ref, yd_ref, yo_ref, *, c):
    zi = pl.program_id(1)

    @pl.when(zi == 0)
    def _():
        st_ref[...] = jnp.zeros(st_ref.shape, F32)

    xbc = _silu(_conv_window(win_ref, xbc_ref, cw_ref[...], c, zi) + cb_ref[...])
    xs = xbc[:, :BR_W]
    gs = SSM_GROUPS * SSM_STATE
    bm = xbc[:, BR_W:BR_W + gs]
    cm = xbc[:, BR_W + gs:]
    lane = _iota2((1, LANE), 1)
    head_lane = lane < SSM_HEADS
    dt = _softplus(sm_ref[:, :LANE] + bias_ref[:, :LANE])
    a = jnp.where(head_lane, -jnp.exp(alog_ref[:, :LANE]), 0.0)
    dt = jnp.where(head_lane, dt, 0.0)
    acs = _dot_f32(_tril_f32(c), dt * a)
    acs_t = acs.T
    acs_last = acs[c - 1:c, :]
    expander = _head_expander(SSM_HEADS, SSM_HEAD).astype(F32)
    xdt = xs * _dot_f32(dt, expander)
    w_t = (xdt * _dot_f32(jnp.exp(acs_last - acs), expander)).T
    causal = _iota2((c, c), 0) >= _iota2((c, c), 1)
    rep = SSM_HEADS // SSM_GROUPS
    for g in range(SSM_GROUPS):
        bm_g = bm[:, g * SSM_STATE:(g + 1) * SSM_STATE].astype(BF16)
        cm_g = cm[:, g * SSM_STATE:(g + 1) * SSM_STATE].astype(BF16)
        cb = _dot_nt(cm_g, bm_g)
        for h in range(g * rep, (g + 1) * rep):
            sl = slice(h * SSM_HEAD, (h + 1) * SSM_HEAD)
            decay = jnp.exp(jnp.where(causal, acs[:, h:h + 1] - acs_t[h:h + 1, :], -jnp.inf))
            yd_ref[:, sl] = _dot(cb * decay, xdt[:, sl])
            h_prev = st_ref[h]
            yo_ref[:, sl] = _dot_nt(cm_g, h_prev)
            st_ref[h] = h_prev * jnp.exp(acs_last[:, h:h + 1]) + _dot(w_t[sl, :], bm_g)
    y = yd_ref[...] + yo_ref[...] * _dot_f32(jnp.exp(acs), expander) + dsk_ref[...] * xs
    y = y * _silu(z_ref[...])
    gw = BR_W // SSM_GROUPS
    ng = ng_ref[...]
    o_ref[...] = jnp.concatenate([_rms(y[:, g * gw:(g + 1) * gw], ng[:, g * gw:(g + 1) * gw])
                                  for g in range(SSM_GROUPS)], axis=1)


def _full(shape):
    return pl.BlockSpec(shape, lambda *a: (0,) * len(shape))


def _ssd_prompt(proj, cw, cb, bias_row, alog_row, dsk, ng, bsz, seq):
    c = SSM_CHUNK
    nc = seq // c
    return pl.pallas_call(
        functools.partial(_ssd_prompt_kernel, c=c),
        grid=(bsz, nc),
        in_specs=[pl.BlockSpec((c, BR_W), lambda b, z: (b * nc + z, P_SZ // BR_W)),
                  pl.BlockSpec((c, SSM_CONV_CH), lambda b, z: (b * nc + z, P_XBC // SSM_CONV_CH)),
                  pl.BlockSpec((c, SMALL_W), lambda b, z: (b * nc + z, P_SMALL // SMALL_W)),
                  _full((CONV_W, SSM_CONV_CH)), _full((1, SSM_CONV_CH)), _full((1, SMALL_W)), _full((1, SMALL_W)),
                  _full((1, BR_W)), _full((1, BR_W))],
        out_specs=[pl.BlockSpec((c, BR_W), lambda b, z: (b * nc + z, 0)),
                   pl.BlockSpec((None, SSM_HEADS, SSM_HEAD, SSM_STATE), lambda b, z: (b, 0, 0, 0))],
        out_shape=[jax.ShapeDtypeStruct((bsz * seq, BR_W), F32),
                   jax.ShapeDtypeStruct((bsz, SSM_HEADS, SSM_HEAD, SSM_STATE), F32)],
        scratch_shapes=[pltpu.VMEM((c + 8, SSM_CONV_CH), F32), pltpu.VMEM((c, BR_W), F32), pltpu.VMEM((c, BR_W), F32)],
        compiler_params=_cparams("parallel", "arbitrary"),
        name="ssd_prompt")(proj, proj, proj, cw, cb, bias_row, alog_row, dsk, ng)


def _inv_unit_lower(a, c):
    row = _iota2((c, c), 0)
    col = _iota2((c, c), 1)
    eye = (row == col).astype(F32)
    blk = 16
    d = jnp.where((row // blk) == (col // blk), a, 0.0)
    x = eye - d
    p = d
    for _ in range(int(math.log2(blk)) - 1):
        p = _dot_f32(p, p)
        x = x + _dot_f32(x, p)
    while blk < c:
        off = jnp.where(((row // (2 * blk)) == (col // (2 * blk))) & ((row // blk) != (col // blk)), a, 0.0)
        x = x - _dot_f32(_dot_f32(x, off), x)
        blk *= 2
    return x


def _gdn_prompt_kernel(qkv_ref, z_ref, sm_ref, cw_ref, bias_ref, alog_ref, ng_ref, o_ref, st_ref, win_ref, *, c):
    zi = pl.program_id(1)

    @pl.when(zi == 0)
    def _():
        st_ref[...] = jnp.zeros(st_ref.shape, F32)

    qkv = _silu(_conv_window(win_ref, qkv_ref, cw_ref[...], c, zi))
    pre = sm_ref[:, :LANE] + bias_ref[:, :LANE]
    g_all = -jnp.exp(alog_ref[:, :LANE]) * _softplus(pre)
    beta_all = _sigmoid(sm_ref[:, :LANE])
    gc = _dot_f32(_tril_f32(c), g_all)
    gc_t = jnp.concatenate([gc, jnp.zeros((LANE - c, LANE), F32)], axis=0).T if c < LANE else gc.T
    row = _iota2((c, c), 0)
    col = _iota2((c, c), 1)
    causal = row >= col
    z = z_ref[...]
    ng = ng_ref[...]
    hd = GDN_HEAD
    for h in range(GDN_HEADS):
        q = qkv[:, h * hd:(h + 1) * hd]
        k = qkv[:, BR_W + h * hd:BR_W + (h + 1) * hd]
        v = qkv[:, 2 * BR_W + h * hd:2 * BR_W + (h + 1) * hd]
        q = q * lax.rsqrt(jnp.sum(q * q, axis=-1, keepdims=True) + EPS) * (hd ** -0.5)
        k = k * lax.rsqrt(jnp.sum(k * k, axis=-1, keepdims=True) + EPS)
        g_col = gc[:, S_GA + h:S_GA + h + 1]
        g_row = gc_t[S_GA + h:S_GA + h + 1, :c]
        g_last = g_col[c - 1:c, :]
        decay = jnp.exp(jnp.where(causal, g_col - g_row, -jnp.inf))
        beta = beta_all[:, S_GB + h:S_GB + h + 1]
        kb = k * beta
        a_low = jnp.where(row > col, _dot_nt(kb, k) * decay, 0.0)
        t_inv = _inv_unit_lower(a_low, c)
        eg = jnp.exp(g_col)
        u = _dot(t_inv, v * beta)
        w = _dot(t_inv, kb * eg)
        attn = _dot_nt(q, k) * decay
        s_prev = st_ref[h]
        v_new = u - _dot(w, s_prev)
        o = _dot(q * eg, s_prev) + _dot(attn, v_new)
        kg = k * jnp.exp(g_last - g_col)
        st_ref[h] = s_prev * jnp.exp(g_last) + _dot(kg.T, v_new)
        o_ref[:, h * hd:(h + 1) * hd] = _rms(o, ng) * _silu(z[:, h * hd:(h + 1) * hd])


def _gdn_prompt(proj, cw, bias_row, alog_row, ng, bsz, seq):
    c = GDN_CHUNK
    nc = seq // c
    return pl.pallas_call(
        functools.partial(_gdn_prompt_kernel, c=c),
        grid=(bsz, nc),
        in_specs=[pl.BlockSpec((c, GDN_CONV_CH), lambda b, z: (b * nc + z, P_GQKV // GDN_CONV_CH)),
                  pl.BlockSpec((c, BR_W), lambda b, z: (b * nc + z, P_GZ // BR_W)),
                  pl.BlockSpec((c, SMALL_W), lambda b, z: (b * nc + z, P_SMALL // SMALL_W)),
                  _full((CONV_W, GDN_CONV_CH)), _full((1, SMALL_W)), _full((1, SMALL_W)), _full((1, GDN_HEAD))],
        out_specs=[pl.BlockSpec((c, BR_W), lambda b, z: (b * nc + z, 0)),
                   pl.BlockSpec((None, GDN_HEADS, GDN_HEAD, GDN_HEAD), lambda b, z: (b, 0, 0, 0))],
        out_shape=[jax.ShapeDtypeStruct((bsz * seq, BR_W), F32),
                   jax.ShapeDtypeStruct((bsz, GDN_HEADS, GDN_HEAD, GDN_HEAD), F32)],
        scratch_shapes=[pltpu.VMEM((c + 8, GDN_CONV_CH), F32)],
        compiler_params=_cparams("parallel", "arbitrary"),
        name="gdn_prompt")(proj, proj, proj, cw, bias_row, alog_row, ng)


def _mlstm_prompt_kernel(q_ref, k_ref, v_ref, og_ref, sm_ref, bias_ref, ng_ref, o_ref, c_ref, n_ref, m_ref, *, c):
    zi = pl.program_id(1)

    @pl.when(zi == 0)
    def _():
        c_ref[...] = jnp.zeros(c_ref.shape, F32)
        n_ref[...] = jnp.zeros(n_ref.shape, F32)
        m_ref[...] = jnp.zeros(m_ref.shape, F32)

    pre = sm_ref[:, :LANE] + bias_ref[:, :LANE]
    logf = -_softplus(-pre)
    bcum = _dot_f32(_tril_f32(c), logf)
    bcum_t = bcum.T
    pre_t = pre.T
    causal = _iota2((c, c), 0) >= _iota2((c, c), 1)
    ng = ng_ref[...]
    hd = ML_HEAD
    for h in range(ML_HEADS):
        sl = slice(h * hd, (h + 1) * hd)
        q = q_ref[:, sl]
        k = k_ref[:, sl] * (hd ** -0.5)
        v = v_ref[:, sl]
        b_col = bcum[:, S_MF + h:S_MF + h + 1]
        b_row = bcum_t[S_MF + h:S_MF + h + 1, :]
        i_col = pre[:, S_MI + h:S_MI + h + 1]
        i_row = pre_t[S_MI + h:S_MI + h + 1, :]
        dmat = jnp.where(causal, b_col - b_row + i_row, -jnp.inf)
        dmax = jnp.max(dmat, axis=1, keepdims=True)
        m_prev = m_ref[h:h + 1, 0:1]
        m_t = jnp.maximum(b_col + m_prev, dmax)
        w_prev = jnp.exp(b_col + m_prev - m_t)
        s = _dot_nt(q, k) * jnp.exp(dmat - m_t)
        c_prev = c_ref[h]
        n_prev = n_ref[h:h + 1, :]
        num = w_prev * _dot(q, c_prev) + _dot(s, v)
        den = w_prev * jnp.sum(q * n_prev, axis=1, keepdims=True) + jnp.sum(s, axis=1, keepdims=True)
        hid = num / jnp.maximum(jnp.abs(den), jnp.exp(-m_t))
        m_new = m_t[c - 1:c, :]
        b_last = b_col[c - 1:c, :]
        w_c = jnp.exp(b_last + m_prev - m_new)
        kw = k * jnp.exp(b_last - b_col + i_col - m_new)
        c_ref[h] = c_prev * w_c + _dot(kw.T, v)
        n_ref[h:h + 1, :] = n_prev * w_c + jnp.sum(kw, axis=0, keepdims=True)
        m_ref[h:h + 1, :] = jnp.broadcast_to(m_new, (1, LANE))
        o_ref[:, sl] = _rms(hid, ng) * _sigmoid(og_ref[:, sl])


def _mlstm_prompt(proj, bias_row, ng, bsz, seq):
    c = ML_CHUNK
    nc = seq // c
    col = lambda i: pl.BlockSpec((c, BR_W), lambda b, z: (b * nc + z, P_MQ // BR_W + i))
    return pl.pallas_call(
        functools.partial(_mlstm_prompt_kernel, c=c),
        grid=(bsz, nc),
        in_specs=[col(0), col(1), col(2), col(3),
                  pl.BlockSpec((c, SMALL_W), lambda b, z: (b * nc + z, P_SMALL // SMALL_W)),
                  _full((1, SMALL_W)), _full((1, ML_HEAD))],
        out_specs=[pl.BlockSpec((c, BR_W), lambda b, z: (b * nc + z, 0)),
                   pl.BlockSpec((None, ML_HEADS, ML_HEAD, ML_HEAD), lambda b, z: (b, 0, 0, 0)),
                   pl.BlockSpec((None, ML_HEADS, ML_HEAD), lambda b, z: (b, 0, 0)),
                   pl.BlockSpec((None, 8, LANE), lambda b, z: (b, 0, 0))],
        out_shape=[jax.ShapeDtypeStruct((bsz * seq, BR_W), F32),
                   jax.ShapeDtypeStruct((bsz, ML_HEADS, ML_HEAD, ML_HEAD), F32),
                   jax.ShapeDtypeStruct((bsz, ML_HEADS, ML_HEAD), F32),
                   jax.ShapeDtypeStruct((bsz, 8, LANE), F32)],
        compiler_params=_cparams("parallel", "arbitrary"),
        name="mlstm_prompt")(proj, proj, proj, proj, proj, bias_row, ng)


STEP_ROWS = 8


def _conv_step(x, buf, cw, nb_ref, r):
    y = buf[0:1] * cw[0:1] + buf[1:2] * cw[1:2] + buf[2:3] * cw[2:3] + x * cw[3:4]
    nb_ref[r, 0:2, :] = buf[1:3]
    nb_ref[r, 2:3, :] = x
    return y


def _ssd_step_kernel(z_ref, xbc_ref, sm_ref, buf_ref, st_ref, cw_ref, cb_ref, bias_ref, alog_ref, dsk_ref, ng_ref,
                     o_ref, nb_ref, nst_ref):
    cw = cw_ref[...]
    gw = BR_W // SSM_GROUPS
    ng = ng_ref[...]
    dsk = dsk_ref[...]

    def body(r, carry):
        xbc = _silu(_conv_step(xbc_ref[r], buf_ref[r], cw, nb_ref, r) + cb_ref[...])
        xs = xbc[:, :BR_W]
        bm = xbc[:, BR_W:BR_W + SSM_GROUPS * SSM_STATE]
        cm = xbc[:, BR_W + SSM_GROUPS * SSM_STATE:]
        sm = sm_ref[r]
        dt = _softplus(sm[:, :LANE] + bias_ref[:, :LANE])
        d_a = jnp.exp(-jnp.exp(alog_ref[:, :LANE]) * dt)
        ys = []
        for h in range(SSM_HEADS):
            g = h // (SSM_HEADS // SSM_GROUPS)
            xs_h = xs[:, h * SSM_HEAD:(h + 1) * SSM_HEAD]
            x_col = _row_to_col(xs_h * dt[:, h:h + 1])
            bm_g = bm[:, g * SSM_STATE:(g + 1) * SSM_STATE]
            cm_g = cm[:, g * SSM_STATE:(g + 1) * SSM_STATE]
            h_new = st_ref[r, h] * d_a[:, h:h + 1] + x_col * bm_g
            nst_ref[r, h] = h_new
            y_col = jnp.sum(h_new * cm_g, axis=1, keepdims=True)
            ys.append(_col_to_row(y_col))
        y = (jnp.concatenate(ys, axis=1) + dsk * xs) * _silu(z_ref[r])
        o_ref[r] = jnp.concatenate([_rms(y[:, g * gw:(g + 1) * gw], ng[:, g * gw:(g + 1) * gw])
                                    for g in range(SSM_GROUPS)], axis=1)
        return carry

    lax.fori_loop(0, STEP_ROWS, body, 0)


def _ssd_step(proj3, buf, st, layer, cw, cb, bias_row, alog_row, dsk, ng):
    db = proj3.shape[0]
    rb = STEP_ROWS
    return pl.pallas_call(
        _ssd_step_kernel,
        grid=(db // rb,),
        in_specs=[pl.BlockSpec((rb, 1, BR_W), lambda i: (i, 0, P_SZ // BR_W)),
                  pl.BlockSpec((rb, 1, SSM_CONV_CH), lambda i: (i, 0, P_XBC // SSM_CONV_CH)),
                  pl.BlockSpec((rb, 1, SMALL_W), lambda i: (i, 0, P_SMALL // SMALL_W)),
                  pl.BlockSpec((None, rb, CONV_W - 1, SSM_CONV_CH), lambda i: (layer, i, 0, 0)),
                  pl.BlockSpec((None, rb, SSM_HEADS, SSM_HEAD, SSM_STATE), lambda i: (layer, i, 0, 0, 0)),
                  _full((CONV_W, SSM_CONV_CH)), _full((1, SSM_CONV_CH)), _full((1, SMALL_W)), _full((1, SMALL_W)),
                  _full((1, BR_W)), _full((1, BR_W))],
        out_specs=[pl.BlockSpec((rb, 1, BR_W), lambda i: (i, 0, 0)),
                   pl.BlockSpec((rb, CONV_W - 1, SSM_CONV_CH), lambda i: (i, 0, 0)),
                   pl.BlockSpec((rb, SSM_HEADS, SSM_HEAD, SSM_STATE), lambda i: (i, 0, 0, 0))],
        out_shape=[jax.ShapeDtypeStruct((db, 1, BR_W), F32),
                   jax.ShapeDtypeStruct((db, CONV_W - 1, SSM_CONV_CH), F32),
                   jax.ShapeDtypeStruct((db, SSM_HEADS, SSM_HEAD, SSM_STATE), F32)],
        compiler_params=_cparams("parallel"),
        name="ssd_step")(proj3, proj3, proj3, buf, st, cw, cb, bias_row, alog_row, dsk, ng)


def _gdn_step_kernel(qkv_ref, z_ref, sm_ref, buf_ref, st_ref, cw_ref, bias_ref, alog_ref, ng_ref,
                     o_ref, nb_ref, nst_ref):
    cw = cw_ref[...]
    ng = ng_ref[...]
    hd = GDN_HEAD

    def body(r, carry):
        qkv = _silu(_conv_step(qkv_ref[r], buf_ref[r], cw, nb_ref, r))
        sm = sm_ref[r][:, :LANE]
        g_all = -jnp.exp(alog_ref[:, :LANE]) * _softplus(sm + bias_ref[:, :LANE])
        eg_all = jnp.exp(g_all)
        beta_all = _sigmoid(sm)
        z = z_ref[r]
        outs = []
        for h in range(GDN_HEADS):
            q = qkv[:, h * hd:(h + 1) * hd]
            k = qkv[:, BR_W + h * hd:BR_W + (h + 1) * hd]
            v = qkv[:, 2 * BR_W + h * hd:2 * BR_W + (h + 1) * hd]
            q = q * lax.rsqrt(jnp.sum(q * q, axis=-1, keepdims=True) + EPS) * (hd ** -0.5)
            k = k * lax.rsqrt(jnp.sum(k * k, axis=-1, keepdims=True) + EPS)
            eg = eg_all[:, S_GA + h:S_GA + h + 1]
            beta = beta_all[:, S_GB + h:S_GB + h + 1]
            s_prev = st_ref[r, h]
            k_col = _row_to_col(k)
            q_col = _row_to_col(q)
            v_new = v * beta - jnp.sum((k_col * (beta * eg)) * s_prev, axis=0, keepdims=True)
            attn = jnp.sum(q * k, axis=1, keepdims=True)
            o = jnp.sum((q_col * eg) * s_prev, axis=0, keepdims=True) + attn * v_new
            nst_ref[r, h] = s_prev * eg + k_col * v_new
            outs.append(_rms(o, ng) * _silu(z[:, h * hd:(h + 1) * hd]))
        o_ref[r] = jnp.concatenate(outs, axis=1)
        return carry

    lax.fori_loop(0, STEP_ROWS, body, 0)


def _gdn_step(proj3, buf, st, layer, cw, bias_row, alog_row, ng):
    db = proj3.shape[0]
    rb = STEP_ROWS
    return pl.pallas_call(
        _gdn_step_kernel,
        grid=(db // rb,),
        in_specs=[pl.BlockSpec((rb, 1, GDN_CONV_CH), lambda i: (i, 0, P_GQKV // GDN_CONV_CH)),
                  pl.BlockSpec((rb, 1, BR_W), lambda i: (i, 0, P_GZ // BR_W)),
                  pl.BlockSpec((rb, 1, SMALL_W), lambda i: (i, 0, P_SMALL // SMALL_W)),
                  pl.BlockSpec((None, rb, CONV_W - 1, GDN_CONV_CH), lambda i: (layer, i, 0, 0)),
                  pl.BlockSpec((None, rb, GDN_HEADS, GDN_HEAD, GDN_HEAD), lambda i: (layer, i, 0, 0, 0)),
                  _full((CONV_W, GDN_CONV_CH)), _full((1, SMALL_W)), _full((1, SMALL_W)), _full((1, GDN_HEAD))],
        out_specs=[pl.BlockSpec((rb, 1, BR_W), lambda i: (i, 0, 0)),
                   pl.BlockSpec((rb, CONV_W - 1, GDN_CONV_CH), lambda i: (i, 0, 0)),
                   pl.BlockSpec((rb, GDN_HEADS, GDN_HEAD, GDN_HEAD), lambda i: (i, 0, 0, 0))],
        out_shape=[jax.ShapeDtypeStruct((db, 1, BR_W), F32),
                   jax.ShapeDtypeStruct((db, CONV_W - 1, GDN_CONV_CH), F32),
                   jax.ShapeDtypeStruct((db, GDN_HEADS, GDN_HEAD, GDN_HEAD), F32)],
        compiler_params=_cparams("parallel"),
        name="gdn_step")(proj3, proj3, proj3, buf, st, cw, bias_row, alog_row, ng)


def _mlstm_step_kernel(q_ref, k_ref, v_ref, og_ref, sm_ref, c_ref, n_ref, m_ref, bias_ref, ng_ref,
                       o_ref, nc_ref, nn_ref, nm_ref):
    ng = ng_ref[...]
    hd = ML_HEAD
    lane4 = _iota2((1, ML_HEADS), 1)

    def body(r, carry):
        pre = sm_ref[r][:, :LANE] + bias_ref[:, :LANE]
        logf_all = -_softplus(-pre)
        qr, kr, vr, ogr = q_ref[r], k_ref[r], v_ref[r], og_ref[r]
        m_all = m_ref[r]
        n_all = n_ref[r]
        outs = []
        m_out = jnp.zeros((1, ML_HEADS), F32)
        for h in range(ML_HEADS):
            sl = slice(h * hd, (h + 1) * hd)
            q = qr[:, sl]
            k = kr[:, sl] * (hd ** -0.5)
            v = vr[:, sl]
            i_pre = pre[:, S_MI + h:S_MI + h + 1]
            logf = logf_all[:, S_MF + h:S_MF + h + 1]
            m_prev = m_all[:, h:h + 1]
            m_t = jnp.maximum(logf + m_prev, i_pre)
            w_prev = jnp.exp(logf + m_prev - m_t)
            w_j = jnp.exp(i_pre - m_t)
            s = jnp.sum(q * k, axis=1, keepdims=True) * w_j
            c_prev = c_ref[r, h]
            n_prev = n_all[h:h + 1, :]
            q_col = _row_to_col(q)
            k_col = _row_to_col(k)
            num = w_prev * jnp.sum(q_col * c_prev, axis=0, keepdims=True) + s * v
            den = w_prev * jnp.sum(q * n_prev, axis=1, keepdims=True) + s
            hid = num / jnp.maximum(jnp.abs(den), jnp.exp(-m_t))
            nc_ref[r, h] = c_prev * w_prev + (k_col * w_j) * v
            nn_ref[r, h:h + 1, :] = n_prev * w_prev + k * w_j
            m_out = jnp.where(lane4 == h, m_t, m_out)
            outs.append(_rms(hid, ng) * _sigmoid(ogr[:, sl]))
        nm_ref[r] = m_out
        o_ref[r] = jnp.concatenate(outs, axis=1)
        return carry

    lax.fori_loop(0, STEP_ROWS, body, 0)


def _mlstm_step(proj3, c0, n0, m0, layer, bias_row, ng):
    db = proj3.shape[0]
    rb = STEP_ROWS
    col = lambda j: pl.BlockSpec((rb, 1, BR_W), lambda i: (i, 0, P_MQ // BR_W + j))
    return pl.pallas_call(
        _mlstm_step_kernel,
        grid=(db // rb,),
        in_specs=[col(0), col(1), col(2), col(3),
                  pl.BlockSpec((rb, 1, SMALL_W), lambda i: (i, 0, P_SMALL // SMALL_W)),
                  pl.BlockSpec((None, rb, ML_HEADS, ML_HEAD, ML_HEAD), lambda i: (layer, i, 0, 0, 0)),
                  pl.BlockSpec((None, rb, ML_HEADS, ML_HEAD), lambda i: (layer, i, 0, 0)),
                  pl.BlockSpec((None, rb, 1, ML_HEADS), lambda i: (layer, i, 0, 0)),
                  _full((1, SMALL_W)), _full((1, ML_HEAD))],
        out_specs=[pl.BlockSpec((rb, 1, BR_W), lambda i: (i, 0, 0)),
                   pl.BlockSpec((rb, ML_HEADS, ML_HEAD, ML_HEAD), lambda i: (i, 0, 0, 0)),
                   pl.BlockSpec((rb, ML_HEADS, ML_HEAD), lambda i: (i, 0, 0)),
                   pl.BlockSpec((rb, 1, ML_HEADS), lambda i: (i, 0, 0))],
        out_shape=[jax.ShapeDtypeStruct((db, 1, BR_W), F32),
                   jax.ShapeDtypeStruct((db, ML_HEADS, ML_HEAD, ML_HEAD), F32),
                   jax.ShapeDtypeStruct((db, ML_HEADS, ML_HEAD), F32),
                   jax.ShapeDtypeStruct((db, 1, ML_HEADS), F32)],
        compiler_params=_cparams("parallel"),
        name="mlstm_step")(proj3, proj3, proj3, proj3, proj3, c0, n0, m0, bias_row, ng)


def _merge_kernel(x_ref, a_ref, b_ref, c_ref, d_ref, g0, g1, g2, g3, wb_ref, wo_ref, o_ref):
    acc = None
    for n, (br, gate) in enumerate(((a_ref, g0), (b_ref, g1), (c_ref, g2), (d_ref, g3))):
        t = _sigmoid(gate[...]) * jnp.dot(br[...].astype(BF16), wb_ref[n], preferred_element_type=F32)
        acc = t if acc is None else acc + t
    o_ref[...] = x_ref[...] + jnp.dot(acc.astype(BF16), wo_ref[...], preferred_element_type=F32)


def _merge(x, branches, proj, wb, wo, tm):
    m = x.shape[0]
    br = pl.BlockSpec((tm, BR_W), lambda i: (i, 0))
    gate = lambda n: pl.BlockSpec((tm, D_MODEL), lambda i: (i, P_GATE // D_MODEL + n))
    return pl.pallas_call(
        _merge_kernel,
        grid=(m // tm,),
        in_specs=[pl.BlockSpec((tm, D_MODEL), lambda i: (i, 0)), br, br, br, br,
                  gate(0), gate(1), gate(2), gate(3),
                  _full((N_BRANCH, BR_W, D_MODEL)), _full((D_MODEL, D_MODEL))],
        out_specs=pl.BlockSpec((tm, D_MODEL), lambda i: (i, 0)),
        out_shape=jax.ShapeDtypeStruct((m, D_MODEL), F32),
        compiler_params=_cparams("parallel"),
        name="merge")(x, *branches, proj, proj, proj, proj, wb, wo)


def _cross_prompt_kernel(x_ref, g_ref, mk_ref, mv_ref, wq_ref, wo_ref, o_ref):
    x = x_ref[...]
    h = _rms(x, g_ref[...]).astype(BF16)
    q = jnp.dot(h, wq_ref[...], preferred_element_type=F32) * (X_HEAD ** -0.5)
    outs = []
    for hd in range(X_HEADS):
        sl = slice(hd * X_HEAD, (hd + 1) * X_HEAD)
        s = _dot_nt(q[:, sl], mk_ref[:, sl])
        p = jnp.exp(s - jnp.max(s, axis=1, keepdims=True))
        p = p / jnp.sum(p, axis=1, keepdims=True)
        outs.append(_dot(p, mv_ref[:, sl]))
    o = jnp.concatenate(outs, axis=1).astype(BF16)
    o_ref[...] = x + jnp.dot(o, wo_ref[...], preferred_element_type=F32)


def _cross_prompt(x, g, mkv, wq, wo, bsz, seq, tq=512):
    nq = seq // tq
    d = D_MODEL
    return pl.pallas_call(
        _cross_prompt_kernel,
        grid=(bsz, nq),
        in_specs=[pl.BlockSpec((tq, d), lambda b, i: (b * nq + i, 0)), _full((1, d)),
                  pl.BlockSpec((N_MEM, d), lambda b, i: (b, 0)),
                  pl.BlockSpec((N_MEM, d), lambda b, i: (b, 1)),
                  _full((d, d)), _full((d, d))],
        out_specs=pl.BlockSpec((tq, d), lambda b, i: (b * nq + i, 0)),
        out_shape=jax.ShapeDtypeStruct((bsz * seq, d), F32),
        compiler_params=_cparams("parallel", "arbitrary"),
        name="cross_prompt")(x, g.reshape(1, d), mkv, mkv, wq, wo)


CROSS_ROWS = 4


def _cross_decode_kernel(q_ref, mk_ref, mv_ref, o_ref):
    d = D_MODEL
    seg = ((_iota2((d, LANE), 0) >> 8) == _iota2((d, LANE), 1)).astype(BF16)
    expand = ((_iota2((LANE, d), 0) == (_iota2((LANE, d), 1) >> 8))).astype(BF16)

    def body(r, carry):
        q = q_ref[r] * (X_HEAD ** -0.5)
        s = _dot_split(mk_ref[r] * q, seg)
        e = jnp.exp(s - jnp.max(s, axis=0, keepdims=True))
        p = e / jnp.sum(e, axis=0, keepdims=True)
        o_ref[r] = jnp.sum(_dot_split(p, expand) * mv_ref[r], axis=0, keepdims=True)
        return carry

    lax.fori_loop(0, CROSS_ROWS, body, 0)


def _cross_decode(q3, mem_k, mem_v, layer):
    db = q3.shape[0]
    rb = CROSS_ROWS
    d = D_MODEL
    mem = pl.BlockSpec((None, rb, N_MEM, d), lambda i: (layer, i, 0, 0))
    return pl.pallas_call(
        _cross_decode_kernel,
        grid=(db // rb,),
        in_specs=[pl.BlockSpec((rb, 1, d), lambda i: (i, 0, 0)), mem, mem],
        out_specs=pl.BlockSpec((rb, 1, d), lambda i: (i, 0, 0)),
        out_shape=jax.ShapeDtypeStruct((db, 1, d), F32),
        compiler_params=_cparams("parallel"),
        name="cross_decode")(q3, mem_k, mem_v)


def _swiglu_kernel(x_ref, g_ref, wg_ref, wu_ref, wd_ref, o_ref, h_ref, acc_ref):
    j = pl.program_id(1)

    @pl.when(j == 0)
    def _():
        h_ref[...] = _rms(x_ref[...], g_ref[...]).astype(BF16)
        acc_ref[...] = jnp.zeros(acc_ref.shape, F32)

    h = h_ref[...]
    gate = jnp.dot(h, wg_ref[...], preferred_element_type=F32)
    up = jnp.dot(h, wu_ref[...], preferred_element_type=F32)
    acc_ref[...] += jnp.dot((_silu(gate) * up).astype(BF16), wd_ref[...], preferred_element_type=F32)

    @pl.when(j == pl.num_programs(1) - 1)
    def _():
        o_ref[...] = x_ref[...] + acc_ref[...]


def _swiglu(x, g, wgu, wd, tm, tf=D_FF // 2):
    m, d = x.shape
    nf = D_FF // tf
    return pl.pallas_call(
        _swiglu_kernel,
        grid=(m // tm, nf),
        in_specs=[pl.BlockSpec((tm, d), lambda i, j: (i, 0)), _full((1, d)),
                  pl.BlockSpec((d, tf), lambda i, j: (0, j)),
                  pl.BlockSpec((d, tf), lambda i, j: (0, nf + j)),
                  pl.BlockSpec((tf, d), lambda i, j: (j, 0))],
        out_specs=pl.BlockSpec((tm, d), lambda i, j: (i, 0)),
        out_shape=jax.ShapeDtypeStruct((m, d), F32),
        scratch_shapes=[pltpu.VMEM((tm, d), BF16), pltpu.VMEM((tm, d), F32)],
        compiler_params=_cparams("parallel", "arbitrary"),
        name="swiglu")(x, g.reshape(1, d), wgu, wgu, wd)


def _pack_w_in(w):
    offs = np.cumsum((0,) + IN_SPLITS)
    seg = lambda i: w[:, offs[i]:offs[i + 1]]
    small = jnp.concatenate([seg(5), seg(8), seg(9), seg(14), seg(15),
                             jnp.zeros((w.shape[0], SMALL_W - 24), w.dtype)], axis=1)
    order = (0, 1, 2, 3, 4, 6, 7, 10, 11, 12, 13, 16)
    return jnp.concatenate([seg(i) for i in order] + [small], axis=1).astype(BF16)


def _small_row(parts):
    row = jnp.zeros((SMALL_W,), F32)
    for off, val in parts:
        row = lax.dynamic_update_slice(row, val.astype(F32), (off,))
    return row.reshape(1, SMALL_W)


def _layer_params(l, p):
    lp = dict(
        w_in=_pack_w_in(p["w_in"][l]),
        g_mix=p["g_mix"][l],
        lam=tuple(p[n][l].reshape(1, DA_HEAD) for n in ("da_lq1", "da_lk1", "da_lq2", "da_lk2")),
        lam_init=0.8 - 0.6 * math.exp(-0.3 * l),
        sub_g=p["da_sub_g"][l],
        bias_row=_small_row(((S_DT, p["ssm_dt_bias"][l]), (S_GA, p["gdn_dt_bias"][l]),
                             (S_MI, p["ml_i_bias"][l]), (S_MF, p["ml_f_bias"][l]))),
        alog_row=_small_row(((S_DT, p["ssm_a_log"][l]), (S_GA, p["gdn_a_log"][l]))),
        ssm_cw=p["ssm_conv_w"][l], ssm_cb=p["ssm_conv_b"][l].reshape(1, SSM_CONV_CH),
        ssm_dsk=jnp.repeat(p["ssm_d"][l], SSM_HEAD).reshape(1, BR_W),
        ssm_ng=p["ssm_norm_g"][l].reshape(1, BR_W),
        gdn_cw=p["gdn_conv_w"][l], gdn_ng=p["gdn_norm_g"][l].reshape(1, GDN_HEAD),
        ml_ng=p["ml_norm_g"][l].reshape(1, ML_HEAD),
        w_branch=p["w_branch"][l].astype(BF16), w_out=p["w_out"][l].astype(BF16),
        g_cross=p["g_cross"][l], w_cq=p["w_cq"][l].astype(BF16), w_co=p["w_co"][l].astype(BF16),
        g_ffn=p["g_ffn"][l], w_gu=p["w_gu"][l].astype(BF16), w_down=p["w_down"][l].astype(BF16),
    )
    return lp


def _prompt_layer(x, lp, mkv, bsz, seq):
    proj = _norm_matmul(x, lp["g_mix"], lp["w_in"], tm=1024 if (bsz * seq) % 1024 == 0 else bsz * seq,
                        tn=1152, name="in_proj")
    o_da = _da_prompt(proj, lp["lam"], lp["sub_g"], bsz, seq, lp["lam_init"])
    o_ssm, ssm = _ssd_prompt(proj, lp["ssm_cw"], lp["ssm_cb"], lp["bias_row"], lp["alog_row"], lp["ssm_dsk"],
                             lp["ssm_ng"], bsz, seq)
    o_gdn, gdn = _gdn_prompt(proj, lp["gdn_cw"], lp["bias_row"], lp["alog_row"], lp["gdn_ng"], bsz, seq)
    o_ml, ml_c, ml_n, ml_m = _mlstm_prompt(proj, lp["bias_row"], lp["ml_ng"], bsz, seq)
    x = _merge(x, (o_da, o_ssm, o_gdn, o_ml), proj, lp["w_branch"], lp["w_out"], tm=256)
    x = _cross_prompt(x, lp["g_cross"], mkv, lp["w_cq"], lp["w_co"], bsz, seq, tq=min(512, seq))
    x = _swiglu(x, lp["g_ffn"], lp["w_gu"], lp["w_down"], tm=512 if (bsz * seq) % 512 == 0 else 256)
    p3 = proj.reshape(bsz, seq, PACK_W)
    new = dict(
        k=p3[:, :, P_K:P_K + BR_W].reshape(bsz, seq, DA_HEADS, 2, DA_HEAD),
        v=p3[:, :, P_V:P_V + BR_W].reshape(bsz, seq, DA_HEADS, 2 * DA_HEAD),
        ssm_conv=p3[:, seq - (CONV_W - 1):, P_XBC:P_XBC + SSM_CONV_CH], ssm=ssm,
        gdn_conv=p3[:, seq - (CONV_W - 1):, P_GQKV:P_GQKV + GDN_CONV_CH], gdn=gdn,
        ml_c=ml_c, ml_n=ml_n, ml_m=ml_m[:, :ML_HEADS, 0])
    return x, new


def _sample_layer(x, lp, l, caches, states):
    db = x.shape[0]
    cache_k, cache_v, page_table, mem_k, mem_v = caches
    proj = _norm_matmul(x, lp["g_mix"], lp["w_in"], tm=db, tn=1152, name="in_proj_s")
    proj3 = proj.reshape(db, 1, PACK_W)
    o_da = _da_decode(proj3, lp["lam"], lp["sub_g"], cache_k, cache_v, page_table, l, lp["lam_init"])
    o_ssm, ssm_conv, ssm = _ssd_step(proj3, states["ssm_conv"], states["ssm"], l, lp["ssm_cw"], lp["ssm_cb"],
                                     lp["bias_row"], lp["alog_row"], lp["ssm_dsk"], lp["ssm_ng"])
    o_gdn, gdn_conv, gdn = _gdn_step(proj3, states["gdn_conv"], states["gdn"], l, lp["gdn_cw"],
                                     lp["bias_row"], lp["alog_row"], lp["gdn_ng"])
    o_ml, ml_c, ml_n, ml_m = _mlstm_step(proj3, states["ml_c"], states["ml_n"], states["ml_m"], l,
                                         lp["bias_row"], lp["ml_ng"])
    x = _merge(x, (o_da, o_ssm.reshape(db, BR_W), o_gdn.reshape(db, BR_W), o_ml.reshape(db, BR_W)),
               proj, lp["w_branch"], lp["w_out"], tm=db)
    q = _norm_matmul(x, lp["g_cross"], lp["w_cq"], tm=db, tn=D_MODEL, name="cross_q_s")
    att = _cross_decode(q.reshape(db, 1, D_MODEL), mem_k, mem_v, l)
    x = _matmul_residual(x, att.reshape(db, D_MODEL), lp["w_co"], tm=db, name="cross_o_s")
    x = _swiglu(x, lp["g_ffn"], lp["w_gu"], lp["w_down"], tm=db)
    new = dict(
        k=proj[:, P_K:P_K + BR_W].reshape(db, 1, DA_HEADS, 2, DA_HEAD),
        v=proj[:, P_V:P_V + BR_W].reshape(db, 1, DA_HEADS, 2 * DA_HEAD),
        ssm_conv=ssm_conv, ssm=ssm, gdn_conv=gdn_conv, gdn=gdn,
        ml_c=ml_c, ml_n=ml_n, ml_m=ml_m.reshape(db, ML_HEADS))
    return x, new


_STATE_ORDER = ("ssm_conv", "ssm", "gdn_conv", "gdn", "ml_c", "ml_n", "ml_m")


def kernel(x_prompt, x_sample, cache_k, cache_v, cache_mem_k, cache_mem_v, state_ssm_conv, state_ssm, state_gdn_conv, state_gdn, state_mlstm_c, state_mlstm_n, state_mlstm_m, page_table, mem_prompt, g_mix, w_in, da_lq1, da_lk1, da_lq2, da_lk2, da_sub_g, ssm_conv_w, ssm_conv_b, ssm_dt_bias, ssm_a_log, ssm_d, ssm_norm_g, gdn_conv_w, gdn_dt_bias, gdn_a_log, gdn_norm_g, ml_i_bias, ml_f_bias, ml_norm_g, w_branch, w_out, g_cross, g_mem, w_cq, w_ckv, w_co, g_ffn, w_gu, w_down, g_final):
    p = dict(g_mix=g_mix, w_in=w_in, da_lq1=da_lq1, da_lk1=da_lk1, da_lq2=da_lq2, da_lk2=da_lk2,
             da_sub_g=da_sub_g, ssm_conv_w=ssm_conv_w, ssm_conv_b=ssm_conv_b, ssm_dt_bias=ssm_dt_bias,
             ssm_a_log=ssm_a_log, ssm_d=ssm_d, ssm_norm_g=ssm_norm_g, gdn_conv_w=gdn_conv_w,
             gdn_dt_bias=gdn_dt_bias, gdn_a_log=gdn_a_log, gdn_norm_g=gdn_norm_g,
             ml_i_bias=ml_i_bias, ml_f_bias=ml_f_bias, ml_norm_g=ml_norm_g,
             w_branch=w_branch, w_out=w_out, g_cross=g_cross, w_cq=w_cq, w_co=w_co,
             g_ffn=g_ffn, w_gu=w_gu, w_down=w_down)
    depth = w_in.shape[0]
    bsz, seq, d = x_prompt.shape
    db = x_sample.shape[0]
    n_mem = mem_prompt.shape[1]
    lps = [_layer_params(l, p) for l in range(depth)]

    mem2 = mem_prompt.reshape(bsz * n_mem, d)
    xp = x_prompt.reshape(bsz * seq, d)
    p_new = {n: [] for n in ("k", "v", "mem_k", "mem_v") + _STATE_ORDER}
    for l in range(depth):
        mkv = _norm_matmul(mem2, g_mem[l], w_ckv[l].astype(BF16), tm=min(1024, bsz * n_mem), tn=1024, name="mem_kv")
        xp, new = _prompt_layer(xp, lps[l], mkv, bsz, seq)
        mkv5 = mkv.reshape(bsz, n_mem, 2, X_HEADS, X_HEAD)
        new["mem_k"] = mkv5[:, :, 0]
        new["mem_v"] = mkv5[:, :, 1]
        for n in p_new:
            p_new[n].append(new[n])
    y_prompt = _final_norm(xp, g_final, tm=512 if (bsz * seq) % 512 == 0 else bsz * seq, name="final_norm").reshape(bsz, seq, d)

    n_pool = cache_k.shape[1]
    caches = (cache_k.reshape(depth, n_pool, PAGE_SIZE, BR_W), cache_v.reshape(depth, n_pool, PAGE_SIZE, BR_W),
              page_table, cache_mem_k.reshape(depth, db, n_mem, d), cache_mem_v.reshape(depth, db, n_mem, d))
    states = dict(ssm_conv=state_ssm_conv, ssm=state_ssm, gdn_conv=state_gdn_conv, gdn=state_gdn,
                  ml_c=state_mlstm_c, ml_n=state_mlstm_n, ml_m=state_mlstm_m.reshape(depth, db, 1, ML_HEADS))
    xs = x_sample.reshape(db, d)
    s_new = {n: [] for n in ("k", "v") + _STATE_ORDER}
    for l in range(depth):
        xs, new = _sample_layer(xs, lps[l], l, caches, states)
        for n in s_new:
            s_new[n].append(new[n])
    y_sample = _final_norm(xs, g_final, tm=db, name="final_norm_s").reshape(db, 1, d)

    stk = lambda dct, n: jnp.stack(dct[n])
    return (y_prompt, y_sample,
            stk(p_new, "k"), stk(p_new, "v"), stk(p_new, "mem_k"), stk(p_new, "mem_v"),
            *(stk(p_new, n) for n in _STATE_ORDER),
            stk(s_new, "k"), stk(s_new, "v"), *(stk(s_new, n) for n in _STATE_ORDER))
```

```python
import functools
import math

import numpy as np
import jax
import jax.numpy as jnp
from jax import lax
from jax.experimental import pallas as pl
from jax.experimental.pallas import tpu as pltpu

F32 = jnp.float32
BF16 = jnp.bfloat16

D_MODEL = 1024
DEPTH = 4
PAGE_SIZE = 128
EPS = 1e-6
N_MEM = 256
CONV_W = 4
N_BRANCH = 4
BR_W = D_MODEL // 2
DA_HEADS = 4
DA_HEAD = BR_W // (2 * DA_HEADS)
SSM_HEAD = 64
SSM_HEADS = BR_W // SSM_HEAD
SSM_GROUPS = 2
SSM_STATE = 128
SSM_CONV_CH = BR_W + 2 * SSM_GROUPS * SSM_STATE
SSM_CHUNK = 128
GDN_HEADS = 4
GDN_HEAD = BR_W // GDN_HEADS
GDN_CONV_CH = 3 * BR_W
GDN_CHUNK = 64
ML_HEADS = 4
ML_HEAD = BR_W // ML_HEADS
ML_CHUNK = 128
X_HEADS = 4
X_HEAD = D_MODEL // X_HEADS
D_FF = -(-8 * D_MODEL // (3 * 256)) * 256

IN_SPLITS = (BR_W, BR_W, BR_W, BR_W, SSM_CONV_CH, SSM_HEADS, GDN_CONV_CH, BR_W, GDN_HEADS, GDN_HEADS,
             BR_W, BR_W, BR_W, BR_W, ML_HEADS, ML_HEADS, N_BRANCH * D_MODEL)

P_Q, P_K, P_V = 0, BR_W, 2 * BR_W
P_SZ = 3 * BR_W
P_XBC = P_SZ + BR_W
P_GQKV = P_XBC + SSM_CONV_CH
P_GZ = P_GQKV + GDN_CONV_CH
P_MQ = P_GZ + BR_W
P_GATE = P_MQ + 4 * BR_W
P_SMALL = P_GATE + N_BRANCH * D_MODEL
SMALL_W = 256
PACK_W = P_SMALL + SMALL_W
S_DT, S_GA, S_GB, S_MI, S_MF = 0, 8, 12, 16, 20

LANE = 128
VMEM_LIMIT = 56 * 1024 * 1024


def _cparams(*sem):
    return pltpu.CompilerParams(dimension_semantics=sem, vmem_limit_bytes=VMEM_LIMIT)


def _dot(a, b):
    return jnp.dot(a.astype(BF16), b.astype(BF16), preferred_element_type=F32)


def _dot_nt(a, b):
    return lax.dot_general(a.astype(BF16), b.astype(BF16), (((1,), (1,)), ((), ())), preferred_element_type=F32)


def _dot_f32(a, b):
    return jnp.dot(a, b, precision=lax.Precision.HIGHEST, preferred_element_type=F32)


def _dot_split(a, b_bf16):
    hi = a.astype(BF16)
    lo = (a - hi.astype(F32)).astype(BF16)
    return (jnp.dot(hi, b_bf16, preferred_element_type=F32) + jnp.dot(lo, b_bf16, preferred_element_type=F32))


def _sigmoid(x):
    return 1.0 / (1.0 + jnp.exp(-x))


def _silu(x):
    return x * _sigmoid(x)


def _softplus(x):
    return jnp.maximum(x, 0.0) + jnp.log1p(jnp.exp(-jnp.abs(x)))


def _rms(x, g):
    return x * lax.rsqrt(jnp.mean(x * x, axis=-1, keepdims=True) + EPS) * g


def _iota2(shape, dim):
    return lax.broadcasted_iota(jnp.int32, shape, dim)


def _row_to_col(x):
    n = x.shape[1]
    eye = _iota2((n, n), 0) == _iota2((n, n), 1)
    return jnp.sum(jnp.where(eye, jnp.broadcast_to(x, (n, n)), 0.0), axis=1, keepdims=True)


def _col_to_row(x):
    n = x.shape[0]
    eye = _iota2((n, n), 0) == _iota2((n, n), 1)
    return jnp.sum(jnp.where(eye, jnp.broadcast_to(x, (n, n)), 0.0), axis=0, keepdims=True)


def _tril_f32(c):
    return (_iota2((c, c), 0) >= _iota2((c, c), 1)).astype(F32)


def _head_expander(n_heads, width):
    rows = _iota2((LANE, n_heads * width), 0)
    cols = _iota2((LANE, n_heads * width), 1)
    return (rows * width <= cols) & (cols < (rows + 1) * width)


def _lam(lq1, lk1, lq2, lk2, lam_init):
    return (jnp.exp(jnp.sum(lq1[...] * lk1[...], axis=1, keepdims=True))
            - jnp.exp(jnp.sum(lq2[...] * lk2[...], axis=1, keepdims=True)) + lam_init)


def _norm_matmul_kernel(x_ref, g_ref, w_ref, o_ref, h_ref):
    @pl.when(pl.program_id(1) == 0)
    def _():
        h_ref[...] = _rms(x_ref[...], g_ref[...]).astype(BF16)

    o_ref[...] = jnp.dot(h_ref[...], w_ref[...], preferred_element_type=F32)


def _norm_matmul(x, g, w, tm, tn, name):
    m, k = x.shape
    n = w.shape[1]
    return pl.pallas_call(
        _norm_matmul_kernel,
        grid=(m // tm, n // tn),
        in_specs=[pl.BlockSpec((tm, k), lambda i, j: (i, 0)),
                  pl.BlockSpec((1, k), lambda i, j: (0, 0)),
                  pl.BlockSpec((k, tn), lambda i, j: (0, j))],
        out_specs=pl.BlockSpec((tm, tn), lambda i, j: (i, j)),
        out_shape=jax.ShapeDtypeStruct((m, n), F32),
        scratch_shapes=[pltpu.VMEM((tm, k), BF16)],
        compiler_params=_cparams("parallel", "arbitrary"),
        name=name)(x, g.reshape(1, k), w)


def _matmul_res_kernel(x_ref, a_ref, w_ref, o_ref):
    o_ref[...] = x_ref[...] + jnp.dot(a_ref[...].astype(BF16), w_ref[...], preferred_element_type=F32)


def _matmul_residual(x, a, w, tm, name):
    m, n = x.shape
    k = a.shape[1]
    return pl.pallas_call(
        _matmul_res_kernel,
        grid=(m // tm,),
        in_specs=[pl.BlockSpec((tm, n), lambda i: (i, 0)),
                  pl.BlockSpec((tm, k), lambda i: (i, 0)),
                  pl.BlockSpec((k, n), lambda i: (0, 0))],
        out_specs=pl.BlockSpec((tm, n), lambda i: (i, 0)),
        out_shape=jax.ShapeDtypeStruct((m, n), F32),
        compiler_params=_cparams("parallel"),
        name=name)(x, a, w)


def _final_norm_kernel(x_ref, g_ref, o_ref):
    o_ref[...] = _rms(x_ref[...], g_ref[...])


def _final_norm(x, g, tm, name):
    m, n = x.shape
    return pl.pallas_call(
        _final_norm_kernel,
        grid=(m // tm,),
        in_specs=[pl.BlockSpec((tm, n), lambda i: (i, 0)), pl.BlockSpec((1, n), lambda i: (0, 0))],
        out_specs=pl.BlockSpec((tm, n), lambda i: (i, 0)),
        out_shape=jax.ShapeDtypeStruct((m, n), F32),
        compiler_params=_cparams("parallel"),
        name=name)(x, g.reshape(1, n))


def _da_prompt_kernel(lq1, lk1, lq2, lk2, subg_ref, q_ref, k_ref, v_ref, o_ref, kb_ref, vb_ref, *, lam_init, tq):
    qi = pl.program_id(2)

    @pl.when(qi == 0)
    def _():
        kb_ref[...] = k_ref[...].astype(BF16)
        vb_ref[...] = v_ref[...].astype(BF16)

    lam = _lam(lq1, lk1, lq2, lk2, lam_init)
    q = q_ref[...] * (DA_HEAD ** -0.5)
    lane = _iota2((tq, 2 * DA_HEAD), 1)
    q2 = jnp.concatenate([jnp.where(lane < DA_HEAD, q, 0.0).astype(BF16),
                          jnp.where(lane >= DA_HEAD, q, 0.0).astype(BF16)], axis=0)

    def step(j, carry, masked):
        m, l, acc = carry
        start = pl.multiple_of(j * tq, tq)
        kj = kb_ref[pl.ds(start, tq), :]
        vj = vb_ref[pl.ds(start, tq), :]
        s = lax.dot_general(q2, kj, (((1,), (1,)), ((), ())), preferred_element_type=F32)
        if masked:
            row = _iota2((2 * tq, tq), 0)
            col = _iota2((2 * tq, tq), 1)
            row = jnp.where(row >= tq, row - tq, row)
            s = jnp.where(col <= row, s, -jnp.inf)
        m_new = jnp.maximum(m, jnp.max(s, axis=1, keepdims=True))
        alpha = jnp.exp(m - m_new)
        p = jnp.exp(s - m_new)
        l = alpha * l + jnp.sum(p, axis=1, keepdims=True)
        acc = alpha * acc + jnp.dot(p.astype(BF16), vj, preferred_element_type=F32)
        return m_new, l, acc

    init = (jnp.full((2 * tq, 1), -jnp.inf, F32), jnp.zeros((2 * tq, 1), F32), jnp.zeros((2 * tq, 2 * DA_HEAD), F32))
    carry = lax.fori_loop(0, qi, functools.partial(step, masked=False), init)
    _, l, acc = step(qi, carry, True)
    o = acc / l
    o_ref[...] = _rms(o[:tq] - lam * o[tq:], subg_ref[...]) * (1.0 - lam_init)


def _da_prompt(proj, lam_params, sub_g, bsz, seq, lam_init, tq=256):
    nq = seq // tq
    hw = 2 * DA_HEAD
    small = pl.BlockSpec((1, DA_HEAD), lambda b, h, i: (0, 0))
    kv_spec = lambda off: pl.BlockSpec((seq, hw), lambda b, h, i: (b, off // hw + h))
    return pl.pallas_call(
        functools.partial(_da_prompt_kernel, lam_init=lam_init, tq=tq),
        grid=(bsz, DA_HEADS, nq),
        in_specs=[small, small, small, small,
                  pl.BlockSpec((1, hw), lambda b, h, i: (0, 0)),
                  pl.BlockSpec((tq, hw), lambda b, h, i: (b * nq + i, P_Q // hw + h)),
                  kv_spec(P_K), kv_spec(P_V)],
        out_specs=pl.BlockSpec((tq, hw), lambda b, h, i: (b * nq + i, h)),
        out_shape=jax.ShapeDtypeStruct((bsz * seq, BR_W), F32),
        scratch_shapes=[pltpu.VMEM((seq, hw), BF16), pltpu.VMEM((seq, hw), BF16)],
        compiler_params=_cparams("parallel", "parallel", "arbitrary"),
        name="da_prompt")(*lam_params, sub_g.reshape(1, hw), proj, proj, proj)


def _da_decode_kernel(pt_ref, lq1, lk1, lq2, lk2, subg_ref, q_ref, kn_ref, vn_ref, *rest, lam_init, n_pages):
    del pt_ref
    kt_refs = rest[:n_pages]
    v_refs = rest[n_pages:2 * n_pages]
    o_ref = rest[2 * n_pages]
    nh = DA_HEADS
    hw = 2 * DA_HEAD
    lam = _lam(lq1, lk1, lq2, lk2, lam_init)
    q = q_ref[...] * (DA_HEAD ** -0.5)
    r = _iota2((2 * nh, BR_W), 0)
    seg = _iota2((2 * nh, BR_W), 1) >> 6
    q_bd = jnp.where(((seg & 1) == (r >> 2)) & ((seg >> 1) == (r & 3)), jnp.broadcast_to(q, (2 * nh, BR_W)), 0.0)
    s_new = jnp.sum(q_bd * kn_ref[...], axis=1, keepdims=True)
    qb = q_bd.astype(BF16)
    s = jnp.concatenate([jnp.dot(qb, kt_refs[j][...].astype(BF16), preferred_element_type=F32)
                         for j in range(n_pages)], axis=1)
    m = jnp.maximum(jnp.max(s, axis=1, keepdims=True), s_new)
    e = jnp.exp(s - m)
    e_new = jnp.exp(s_new - m)
    l = jnp.sum(e, axis=1, keepdims=True) + e_new
    coef = jnp.where(_iota2((2 * nh, 1), 0) < nh, 1.0, -lam) / l
    w = e * coef
    w_new = e_new * coef
    p = (w + pltpu.roll(w, nh, axis=0)).astype(BF16)
    p_new = w_new + pltpu.roll(w_new, nh, axis=0)
    g = subg_ref[...]
    vn = vn_ref[...]
    outs = []
    for h in range(nh):
        v_h = jnp.concatenate([v_refs[j][pl.ds(h, PAGE_SIZE, stride=nh), :].astype(BF16) for j in range(n_pages)],
                              axis=0)
        o_h = (jnp.dot(p, v_h, preferred_element_type=F32)[h:h + 1, :]
               + p_new[h:h + 1, :] * vn[:, h * hw:(h + 1) * hw])
        outs.append(_rms(o_h, g) * (1.0 - lam_init))
    o_ref[...] = jnp.concatenate(outs, axis=1)


def _da_decode(proj3, lam_params, sub_g, cache_kt, cache_v, page_table, layer, lam_init):
    db = proj3.shape[0]
    n_pages = page_table.shape[1]
    hw = 2 * DA_HEAD
    small = pl.BlockSpec((1, DA_HEAD), lambda b, pt: (0, 0))
    row = lambda off: pl.BlockSpec((None, 1, BR_W), lambda b, pt: (b, 0, off // BR_W))

    def page(j):
        return pl.BlockSpec((None, None, BR_W, PAGE_SIZE), lambda b, pt: (layer, pt[b * n_pages + j], 0, 0))

    grid_spec = pltpu.PrefetchScalarGridSpec(
        num_scalar_prefetch=1,
        grid=(db,),
        in_specs=[small, small, small, small, pl.BlockSpec((1, hw), lambda b, pt: (0, 0)),
                  row(P_Q), row(P_K), row(P_V)]
                 + [page(j) for j in range(n_pages)] + [page(j) for j in range(n_pages)],
        out_specs=pl.BlockSpec((None, 1, BR_W), lambda b, pt: (b, 0, 0)))
    out = pl.pallas_call(
        functools.partial(_da_decode_kernel, lam_init=lam_init, n_pages=n_pages),
        grid_spec=grid_spec,
        out_shape=jax.ShapeDtypeStruct((db, 1, BR_W), F32),
        compiler_params=_cparams("parallel"),
        name="da_decode")(page_table.reshape(-1), *lam_params, sub_g.reshape(1, hw), proj3, proj3, proj3,
                          *([cache_kt] * n_pages), *([cache_v] * n_pages))
    return out.reshape(db, BR_W)


def _conv_window(win_ref, x_ref, cw, c, zi):
    @pl.when(zi == 0)
    def _():
        win_ref[0:8, :] = jnp.zeros((8, win_ref.shape[1]), F32)

    @pl.when(zi > 0)
    def _():
        win_ref[0:8, :] = win_ref[c:c + 8, :]

    win_ref[8:8 + c, :] = x_ref[...]
    y = win_ref[5:5 + c, :] * cw[0:1, :]
    for j in range(1, CONV_W):
        y = y + win_ref[5 + j:5 + j + c, :] * cw[j:j + 1, :]
    return y


def _ssd_prompt_kernel(z_ref, xbc_ref, sm_ref, cw_ref, cb_ref, bias_ref, alog_ref, dsk_ref, ng_ref,
                       o_ref, st_ref, win_ref, yd_ref, yo_ref, *, c):
    zi = pl.program_id(1)

    @pl.when(zi == 0)
    def _():
        st_ref[...] = jnp.zeros(st_ref.shape, F32)

    xbc = _silu(_conv_window(win_ref, xbc_ref, cw_ref[...], c, zi) + cb_ref[...])
    xs = xbc[:, :BR_W]
    gs = SSM_GROUPS * SSM_STATE
    bm = xbc[:, BR_W:BR_W + gs]
    cm = xbc[:, BR_W + gs:]
    lane = _iota2((1, LANE), 1)
    head_lane = lane < SSM_HEADS
    dt = _softplus(sm_ref[:, :LANE] + bias_ref[:, :LANE])
    a = jnp.where(head_lane, -jnp.exp(alog_ref[:, :LANE]), 0.0)
    dt = jnp.where(head_lane, dt, 0.0)
    acs = _dot_f32(_tril_f32(c), dt * a)
    acs_t = acs.T
    acs_last = acs[c - 1:c, :]
    expander = _head_expander(SSM_HEADS, SSM_HEAD).astype(F32)
    xdt = xs * _dot_f32(dt, expander)
    w_t = (xdt * _dot_f32(jnp.exp(acs_last - acs), expander)).T
    causal = _iota2((c, c), 0) >= _iota2((c, c), 1)
    rep = SSM_HEADS // SSM_GROUPS
    for g in range(SSM_GROUPS):
        bm_g = bm[:, g * SSM_STATE:(g + 1) * SSM_STATE].astype(BF16)
        cm_g = cm[:, g * SSM_STATE:(g + 1) * SSM_STATE].astype(BF16)
        cb = _dot_nt(cm_g, bm_g)
        for h in range(g * rep, (g + 1) * rep):
            sl = slice(h * SSM_HEAD, (h + 1) * SSM_HEAD)
            decay = jnp.exp(jnp.where(causal, acs[:, h:h + 1] - acs_t[h:h + 1, :], -jnp.inf))
            yd_ref[:, sl] = _dot(cb * decay, xdt[:, sl])
            h_prev = st_ref[h]
            yo_ref[:, sl] = _dot_nt(cm_g, h_prev)
            st_ref[h] = h_prev * jnp.exp(acs_last[:, h:h + 1]) + _dot(w_t[sl, :], bm_g)
    y = yd_ref[...] + yo_ref[...] * _dot_f32(jnp.exp(acs), expander) + dsk_ref[...] * xs
    y = y * _silu(z_ref[...])
    gw = BR_W // SSM_GROUPS
    ng = ng_ref[...]
    o_ref[...] = jnp.concatenate([_rms(y[:, g * gw:(g + 1) * gw], ng[:, g * gw:(g + 1) * gw])
                                  for g in range(SSM_GROUPS)], axis=1)


def _full(shape):
    return pl.BlockSpec(shape, lambda *a: (0,) * len(shape))


def _ssd_prompt(proj, cw, cb, bias_row, alog_row, dsk, ng, bsz, seq):
    c = SSM_CHUNK
    nc = seq // c
    return pl.pallas_call(
        functools.partial(_ssd_prompt_kernel, c=c),
        grid=(bsz, nc),
        in_specs=[pl.BlockSpec((c, BR_W), lambda b, z: (b * nc + z, P_SZ // BR_W)),
                  pl.BlockSpec((c, SSM_CONV_CH), lambda b, z: (b * nc + z, P_XBC // SSM_CONV_CH)),
                  pl.BlockSpec((c, SMALL_W), lambda b, z: (b * nc + z, P_SMALL // SMALL_W)),
                  _full((CONV_W, SSM_CONV_CH)), _full((1, SSM_CONV_CH)), _full((1, SMALL_W)), _full((1, SMALL_W)),
                  _full((1, BR_W)), _full((1, BR_W))],
        out_specs=[pl.BlockSpec((c, BR_W), lambda b, z: (b * nc + z, 0)),
                   pl.BlockSpec((None, SSM_HEADS, SSM_HEAD, SSM_STATE), lambda b, z: (b, 0, 0, 0))],
        out_shape=[jax.ShapeDtypeStruct((bsz * seq, BR_W), F32),
                   jax.ShapeDtypeStruct((bsz, SSM_HEADS, SSM_HEAD, SSM_STATE), F32)],
        scratch_shapes=[pltpu.VMEM((c + 8, SSM_CONV_CH), F32), pltpu.VMEM((c, BR_W), F32), pltpu.VMEM((c, BR_W), F32)],
        compiler_params=_cparams("parallel", "arbitrary"),
        name="ssd_prompt")(proj, proj, proj, cw, cb, bias_row, alog_row, dsk, ng)


def _dot3(a, b):
    ah = a.astype(BF16)
    al = (a - ah.astype(F32)).astype(BF16)
    bh = b.astype(BF16)
    bl = (b - bh.astype(F32)).astype(BF16)
    d = lambda x, y: jnp.dot(x, y, preferred_element_type=F32)
    return d(ah, bh) + (d(ah, bl) + d(al, bh))


def _inv_unit_lower(a, n, c):
    row = _iota2((n, n), 0)
    col = _iota2((n, n), 1)
    eye = (row == col).astype(F32)
    sh = 4
    d = jnp.where((row >> sh) == (col >> sh), a, 0.0)
    x = eye - d
    p = d
    for _ in range(sh - 1):
        p = _dot3(p, p)
        x = x + _dot3(x, p)
    while (1 << sh) < c:
        off = jnp.where(((row >> (sh + 1)) == (col >> (sh + 1))) & ((row >> sh) != (col >> sh)), a, 0.0)
        x = x - _dot3(_dot3(x, off), x)
        sh += 1
    return x


def _gdn_prompt_kernel(qkv_ref, z_ref, sm_ref, cw_ref, bias_ref, alog_ref, ng_ref, o_ref, st_ref, win_ref, *, c, nsub):
    zi = pl.program_id(1)
    rows = c * nsub
    nh = GDN_HEADS
    hd = GDN_HEAD
    n = nh * c

    @pl.when(zi == 0)
    def _():
        st_ref[...] = jnp.zeros(st_ref.shape, F32)

    qkv = _silu(_conv_window(win_ref, qkv_ref, cw_ref[...], rows, zi))
    pre = sm_ref[:, :LANE] + bias_ref[:, :LANE]
    g_all = -jnp.exp(alog_ref[:, :LANE]) * _softplus(pre)
    beta_all = _sigmoid(sm_ref[:, :LANE])
    rr = _iota2((rows, rows), 0)
    cc = _iota2((rows, rows), 1)
    tril_chunks = ((rr >= cc) & ((rr // c) == (cc // c))).astype(F32)
    gc = _dot_f32(tril_chunks, g_all)
    gc_t = gc.T
    row = _iota2((n, n), 0)
    col = _iota2((n, n), 1)
    same = (row // c) == (col // c)
    causal = same & (row >= col)
    strict = same & (row > col)
    stack = lambda f: jnp.concatenate([f(h) for h in range(nh)], axis=0)
    l2 = lambda t: t * lax.rsqrt(jnp.sum(t * t, axis=-1, keepdims=True) + EPS)

    pieces = []
    for s in range(nsub):
        r0, r1 = s * c, (s + 1) * c
        q = l2(stack(lambda h: qkv[r0:r1, h * hd:(h + 1) * hd])) * (hd ** -0.5)
        k = l2(stack(lambda h: qkv[r0:r1, BR_W + h * hd:BR_W + (h + 1) * hd]))
        v = stack(lambda h: qkv[r0:r1, 2 * BR_W + h * hd:2 * BR_W + (h + 1) * hd])
        g_col = stack(lambda h: gc[r0:r1, S_GA + h:S_GA + h + 1])
        g_row = jnp.concatenate([gc_t[S_GA + h:S_GA + h + 1, r0:r1] for h in range(nh)], axis=1)
        g_last = [gc[r1 - 1:r1, S_GA + h:S_GA + h + 1] for h in range(nh)]
        g_last_col = stack(lambda h: jnp.broadcast_to(g_last[h], (c, 1)))
        beta = stack(lambda h: beta_all[r0:r1, S_GB + h:S_GB + h + 1])
        decay = jnp.exp(jnp.where(causal, g_col - g_row, -jnp.inf))
        kb = k * beta
        a_low = jnp.where(strict, _dot_nt(kb, k) * decay, 0.0)
        t_inv = _inv_unit_lower(a_low, n, c)
        eg = jnp.exp(g_col)
        pieces.append(dict(u=_dot(t_inv, v * beta), w=_dot(t_inv, kb * eg), attn=_dot_nt(q, k) * decay,
                           qg=q * eg, kg=k * jnp.exp(g_last_col - g_col), g_last=g_last))

    z = z_ref[...]
    ng = ng_ref[...]
    hs = lambda t, h: t[h * c:(h + 1) * c]
    state = [st_ref[h] for h in range(nh)]
    for s, pc in enumerate(pieces):
        v_new = pc["u"] - stack(lambda h: _dot(hs(pc["w"], h), state[h]))
        o = stack(lambda h: _dot(hs(pc["qg"], h), state[h])) + _dot(pc["attn"], v_new)
        state = [state[h] * jnp.exp(pc["g_last"][h]) + _dot(hs(pc["kg"], h).T, hs(v_new, h)) for h in range(nh)]
        on = _rms(o, ng)
        for h in range(nh):
            o_ref[s * c:(s + 1) * c, h * hd:(h + 1) * hd] = (
                hs(on, h) * _silu(z[s * c:(s + 1) * c, h * hd:(h + 1) * hd]))
    for h in range(nh):
        st_ref[h] = state[h]


def _gdn_prompt(proj, cw, bias_row, alog_row, ng, bsz, seq, nsub=2):
    c = GDN_CHUNK
    rows = c * nsub
    nc = seq // rows
    return pl.pallas_call(
        functools.partial(_gdn_prompt_kernel, c=c, nsub=nsub),
        grid=(bsz, nc),
        in_specs=[pl.BlockSpec((rows, GDN_CONV_CH), lambda b, z: (b * nc + z, P_GQKV // GDN_CONV_CH)),
                  pl.BlockSpec((rows, BR_W), lambda b, z: (b * nc + z, P_GZ // BR_W)),
                  pl.BlockSpec((rows, SMALL_W), lambda b, z: (b * nc + z, P_SMALL // SMALL_W)),
                  _full((CONV_W, GDN_CONV_CH)), _full((1, SMALL_W)), _full((1, SMALL_W)), _full((1, GDN_HEAD))],
        out_specs=[pl.BlockSpec((rows, BR_W), lambda b, z: (b * nc + z, 0)),
                   pl.BlockSpec((None, GDN_HEADS, GDN_HEAD, GDN_HEAD), lambda b, z: (b, 0, 0, 0))],
        out_shape=[jax.ShapeDtypeStruct((bsz * seq, BR_W), F32),
                   jax.ShapeDtypeStruct((bsz, GDN_HEADS, GDN_HEAD, GDN_HEAD), F32)],
        scratch_shapes=[pltpu.VMEM((rows + 8, GDN_CONV_CH), F32)],
        compiler_params=_cparams("parallel", "arbitrary"),
        name="gdn_prompt")(proj, proj, proj, cw, bias_row, alog_row, ng)


def _mlstm_prompt_kernel(q_ref, k_ref, v_ref, og_ref, sm_ref, bias_ref, ng_ref, o_ref, c_ref, n_ref, m_ref, *, c):
    zi = pl.program_id(1)

    @pl.when(zi == 0)
    def _():
        c_ref[...] = jnp.zeros(c_ref.shape, F32)
        n_ref[...] = jnp.zeros(n_ref.shape, F32)
        m_ref[...] = jnp.zeros(m_ref.shape, F32)

    pre = sm_ref[:, :LANE] + bias_ref[:, :LANE]
    logf = -_softplus(-pre)
    bcum = _dot_f32(_tril_f32(c), logf)
    bcum_t = bcum.T
    pre_t = pre.T
    causal = _iota2((c, c), 0) >= _iota2((c, c), 1)
    ng = ng_ref[...]
    hd = ML_HEAD
    for h in range(ML_HEADS):
        sl = slice(h * hd, (h + 1) * hd)
        q = q_ref[:, sl]
        k = k_ref[:, sl] * (hd ** -0.5)
        v = v_ref[:, sl]
        b_col = bcum[:, S_MF + h:S_MF + h + 1]
        b_row = bcum_t[S_MF + h:S_MF + h + 1, :]
        i_col = pre[:, S_MI + h:S_MI + h + 1]
        i_row = pre_t[S_MI + h:S_MI + h + 1, :]
        dmat = jnp.where(causal, b_col - b_row + i_row, -jnp.inf)
        dmax = jnp.max(dmat, axis=1, keepdims=True)
        m_prev = m_ref[h:h + 1, 0:1]
        m_t = jnp.maximum(b_col + m_prev, dmax)
        w_prev = jnp.exp(b_col + m_prev - m_t)
        s = _dot_nt(q, k) * jnp.exp(dmat - m_t)
        c_prev = c_ref[h]
        n_prev = n_ref[h:h + 1, :]
        num = w_prev * _dot(q, c_prev) + _dot(s, v)
        den = w_prev * jnp.sum(q * n_prev, axis=1, keepdims=True) + jnp.sum(s, axis=1, keepdims=True)
        hid = num / jnp.maximum(jnp.abs(den), jnp.exp(-m_t))
        m_new = m_t[c - 1:c, :]
        b_last = b_col[c - 1:c, :]
        w_c = jnp.exp(b_last + m_prev - m_new)
        kw = k * jnp.exp(b_last - b_col + i_col - m_new)
        c_ref[h] = c_prev * w_c + _dot(kw.T, v)
        n_ref[h:h + 1, :] = n_prev * w_c + jnp.sum(kw, axis=0, keepdims=True)
        m_ref[h:h + 1, :] = jnp.broadcast_to(m_new, (1, LANE))
        o_ref[:, sl] = _rms(hid, ng) * _sigmoid(og_ref[:, sl])


def _mlstm_prompt(proj, bias_row, ng, bsz, seq):
    c = ML_CHUNK
    nc = seq // c
    col = lambda i: pl.BlockSpec((c, BR_W), lambda b, z: (b * nc + z, P_MQ // BR_W + i))
    return pl.pallas_call(
        functools.partial(_mlstm_prompt_kernel, c=c),
        grid=(bsz, nc),
        in_specs=[col(0), col(1), col(2), col(3),
                  pl.BlockSpec((c, SMALL_W), lambda b, z: (b * nc + z, P_SMALL // SMALL_W)),
                  _full((1, SMALL_W)), _full((1, ML_HEAD))],
        out_specs=[pl.BlockSpec((c, BR_W), lambda b, z: (b * nc + z, 0)),
                   pl.BlockSpec((None, ML_HEADS, ML_HEAD, ML_HEAD), lambda b, z: (b, 0, 0, 0)),
                   pl.BlockSpec((None, ML_HEADS, ML_HEAD), lambda b, z: (b, 0, 0)),
                   pl.BlockSpec((None, 8, LANE), lambda b, z: (b, 0, 0))],
        out_shape=[jax.ShapeDtypeStruct((bsz * seq, BR_W), F32),
                   jax.ShapeDtypeStruct((bsz, ML_HEADS, ML_HEAD, ML_HEAD), F32),
                   jax.ShapeDtypeStruct((bsz, ML_HEADS, ML_HEAD), F32),
                   jax.ShapeDtypeStruct((bsz, 8, LANE), F32)],
        compiler_params=_cparams("parallel", "arbitrary"),
        name="mlstm_prompt")(proj, proj, proj, proj, proj, bias_row, ng)


STEP_ROWS = 8


def _conv_step(x, buf, cw, nb_ref, r):
    y = buf[0:1] * cw[0:1] + buf[1:2] * cw[1:2] + buf[2:3] * cw[2:3] + x * cw[3:4]
    nb_ref[r, 0:2, :] = buf[1:3]
    nb_ref[r, 2:3, :] = x
    return y


def _ssd_step_kernel(z_ref, xbc_ref, sm_ref, buf_ref, st_ref, cw_ref, cb_ref, bias_ref, alog_ref, dsk_ref, ng_ref,
                     o_ref, nb_ref, nst_ref):
    cw = cw_ref[...]
    gw = BR_W // SSM_GROUPS
    ng = ng_ref[...]
    dsk = dsk_ref[...]

    def body(r, carry):
        xbc = _silu(_conv_step(xbc_ref[r], buf_ref[r], cw, nb_ref, r) + cb_ref[...])
        xs = xbc[:, :BR_W]
        bm = xbc[:, BR_W:BR_W + SSM_GROUPS * SSM_STATE]
        cm = xbc[:, BR_W + SSM_GROUPS * SSM_STATE:]
        sm = sm_ref[r]
        dt = _softplus(sm[:, :LANE] + bias_ref[:, :LANE])
        d_a = jnp.exp(-jnp.exp(alog_ref[:, :LANE]) * dt)
        ys = []
        for h in range(SSM_HEADS):
            g = h // (SSM_HEADS // SSM_GROUPS)
            xs_h = xs[:, h * SSM_HEAD:(h + 1) * SSM_HEAD]
            x_col = _row_to_col(xs_h * dt[:, h:h + 1])
            bm_g = bm[:, g * SSM_STATE:(g + 1) * SSM_STATE]
            cm_g = cm[:, g * SSM_STATE:(g + 1) * SSM_STATE]
            h_new = st_ref[r, h] * d_a[:, h:h + 1] + x_col * bm_g
            nst_ref[r, h] = h_new
            y_col = jnp.sum(h_new * cm_g, axis=1, keepdims=True)
            ys.append(_col_to_row(y_col))
        y = (jnp.concatenate(ys, axis=1) + dsk * xs) * _silu(z_ref[r])
        o_ref[r] = jnp.concatenate([_rms(y[:, g * gw:(g + 1) * gw], ng[:, g * gw:(g + 1) * gw])
                                    for g in range(SSM_GROUPS)], axis=1)
        return carry

    lax.fori_loop(0, STEP_ROWS, body, 0)


def _ssd_step(proj3, buf, st, layer, cw, cb, bias_row, alog_row, dsk, ng):
    db = proj3.shape[0]
    rb = STEP_ROWS
    return pl.pallas_call(
        _ssd_step_kernel,
        grid=(db // rb,),
        in_specs=[pl.BlockSpec((rb, 1, BR_W), lambda i: (i, 0, P_SZ // BR_W)),
                  pl.BlockSpec((rb, 1, SSM_CONV_CH), lambda i: (i, 0, P_XBC // SSM_CONV_CH)),
                  pl.BlockSpec((rb, 1, SMALL_W), lambda i: (i, 0, P_SMALL // SMALL_W)),
                  pl.BlockSpec((None, rb, CONV_W - 1, SSM_CONV_CH), lambda i: (layer, i, 0, 0)),
                  pl.BlockSpec((None, rb, SSM_HEADS, SSM_HEAD, SSM_STATE), lambda i: (layer, i, 0, 0, 0)),
                  _full((CONV_W, SSM_CONV_CH)), _full((1, SSM_CONV_CH)), _full((1, SMALL_W)), _full((1, SMALL_W)),
                  _full((1, BR_W)), _full((1, BR_W))],
        out_specs=[pl.BlockSpec((rb, 1, BR_W), lambda i: (i, 0, 0)),
                   pl.BlockSpec((rb, CONV_W - 1, SSM_CONV_CH), lambda i: (i, 0, 0)),
                   pl.BlockSpec((rb, SSM_HEADS, SSM_HEAD, SSM_STATE), lambda i: (i, 0, 0, 0))],
        out_shape=[jax.ShapeDtypeStruct((db, 1, BR_W), F32),
                   jax.ShapeDtypeStruct((db, CONV_W - 1, SSM_CONV_CH), F32),
                   jax.ShapeDtypeStruct((db, SSM_HEADS, SSM_HEAD, SSM_STATE), F32)],
        compiler_params=_cparams("parallel"),
        name="ssd_step")(proj3, proj3, proj3, buf, st, cw, cb, bias_row, alog_row, dsk, ng)


def _gdn_step_kernel(qkv_ref, z_ref, sm_ref, buf_ref, st_ref, cw_ref, bias_ref, alog_ref, ng_ref,
                     o_ref, nb_ref, nst_ref):
    cw = cw_ref[...]
    ng = ng_ref[...]
    hd = GDN_HEAD

    def body(r, carry):
        qkv = _silu(_conv_step(qkv_ref[r], buf_ref[r], cw, nb_ref, r))
        sm = sm_ref[r][:, :LANE]
        g_all = -jnp.exp(alog_ref[:, :LANE]) * _softplus(sm + bias_ref[:, :LANE])
        eg_all = jnp.exp(g_all)
        beta_all = _sigmoid(sm)
        z = z_ref[r]
        outs = []
        for h in range(GDN_HEADS):
            q = qkv[:, h * hd:(h + 1) * hd]
            k = qkv[:, BR_W + h * hd:BR_W + (h + 1) * hd]
            v = qkv[:, 2 * BR_W + h * hd:2 * BR_W + (h + 1) * hd]
            q = q * lax.rsqrt(jnp.sum(q * q, axis=-1, keepdims=True) + EPS) * (hd ** -0.5)
            k = k * lax.rsqrt(jnp.sum(k * k, axis=-1, keepdims=True) + EPS)
            eg = eg_all[:, S_GA + h:S_GA + h + 1]
            beta = beta_all[:, S_GB + h:S_GB + h + 1]
            s_prev = st_ref[r, h]
            k_col = _row_to_col(k)
            q_col = _row_to_col(q)
            v_new = v * beta - jnp.sum((k_col * (beta * eg)) * s_prev, axis=0, keepdims=True)
            attn = jnp.sum(q * k, axis=1, keepdims=True)
            o = jnp.sum((q_col * eg) * s_prev, axis=0, keepdims=True) + attn * v_new
            nst_ref[r, h] = s_prev * eg + k_col * v_new
            outs.append(_rms(o, ng) * _silu(z[:, h * hd:(h + 1) * hd]))
        o_ref[r] = jnp.concatenate(outs, axis=1)
        return carry

    lax.fori_loop(0, STEP_ROWS, body, 0)


def _gdn_step(proj3, buf, st, layer, cw, bias_row, alog_row, ng):
    db = proj3.shape[0]
    rb = STEP_ROWS
    return pl.pallas_call(
        _gdn_step_kernel,
        grid=(db // rb,),
        in_specs=[pl.BlockSpec((rb, 1, GDN_CONV_CH), lambda i: (i, 0, P_GQKV // GDN_CONV_CH)),
                  pl.BlockSpec((rb, 1, BR_W), lambda i: (i, 0, P_GZ // BR_W)),
                  pl.BlockSpec((rb, 1, SMALL_W), lambda i: (i, 0, P_SMALL // SMALL_W)),
                  pl.BlockSpec((None, rb, CONV_W - 1, GDN_CONV_CH), lambda i: (layer, i, 0, 0)),
                  pl.BlockSpec((None, rb, GDN_HEADS, GDN_HEAD, GDN_HEAD), lambda i: (layer, i, 0, 0, 0)),
                  _full((CONV_W, GDN_CONV_CH)), _full((1, SMALL_W)), _full((1, SMALL_W)), _full((1, GDN_HEAD))],
        out_specs=[pl.BlockSpec((rb, 1, BR_W), lambda i: (i, 0, 0)),
                   pl.BlockSpec((rb, CONV_W - 1, GDN_CONV_CH), lambda i: (i, 0, 0)),
                   pl.BlockSpec((rb, GDN_HEADS, GDN_HEAD, GDN_HEAD), lambda i: (i, 0, 0, 0))],
        out_shape=[jax.ShapeDtypeStruct((db, 1, BR_W), F32),
                   jax.ShapeDtypeStruct((db, CONV_W - 1, GDN_CONV_CH), F32),
                   jax.ShapeDtypeStruct((db, GDN_HEADS, GDN_HEAD, GDN_HEAD), F32)],
        compiler_params=_cparams("parallel"),
        name="gdn_step")(proj3, proj3, proj3, buf, st, cw, bias_row, alog_row, ng)


def _mlstm_step_kernel(q_ref, k_ref, v_ref, og_ref, sm_ref, c_ref, n_ref, m_ref, bias_ref, ng_ref,
                       o_ref, nc_ref, nn_ref, nm_ref):
    ng = ng_ref[...]
    hd = ML_HEAD
    lane4 = _iota2((1, ML_HEADS), 1)

    def body(r, carry):
        pre = sm_ref[r][:, :LANE] + bias_ref[:, :LANE]
        logf_all = -_softplus(-pre)
        qr, kr, vr, ogr = q_ref[r], k_ref[r], v_ref[r], og_ref[r]
        m_all = m_ref[r]
        n_all = n_ref[r]
        outs = []
        m_out = jnp.zeros((1, ML_HEADS), F32)
        for h in range(ML_HEADS):
            sl = slice(h * hd, (h + 1) * hd)
            q = qr[:, sl]
            k = kr[:, sl] * (hd ** -0.5)
            v = vr[:, sl]
            i_pre = pre[:, S_MI + h:S_MI + h + 1]
            logf = logf_all[:, S_MF + h:S_MF + h + 1]
            m_prev = m_all[:, h:h + 1]
            m_t = jnp.maximum(logf + m_prev, i_pre)
            w_prev = jnp.exp(logf + m_prev - m_t)
            w_j = jnp.exp(i_pre - m_t)
            s = jnp.sum(q * k, axis=1, keepdims=True) * w_j
            c_prev = c_ref[r, h]
            n_prev = n_all[h:h + 1, :]
            q_col = _row_to_col(q)
            k_col = _row_to_col(k)
            num = w_prev * jnp.sum(q_col * c_prev, axis=0, keepdims=True) + s * v
            den = w_prev * jnp.sum(q * n_prev, axis=1, keepdims=True) + s
            hid = num / jnp.maximum(jnp.abs(den), jnp.exp(-m_t))
            nc_ref[r, h] = c_prev * w_prev + (k_col * w_j) * v
            nn_ref[r, h:h + 1, :] = n_prev * w_prev + k * w_j
            m_out = jnp.where(lane4 == h, m_t, m_out)
            outs.append(_rms(hid, ng) * _sigmoid(ogr[:, sl]))
        nm_ref[r] = m_out
        o_ref[r] = jnp.concatenate(outs, axis=1)
        return carry

    lax.fori_loop(0, STEP_ROWS, body, 0)


def _mlstm_step(proj3, c0, n0, m0, layer, bias_row, ng):
    db = proj3.shape[0]
    rb = STEP_ROWS
    col = lambda j: pl.BlockSpec((rb, 1, BR_W), lambda i: (i, 0, P_MQ // BR_W + j))
    return pl.pallas_call(
        _mlstm_step_kernel,
        grid=(db // rb,),
        in_specs=[col(0), col(1), col(2), col(3),
                  pl.BlockSpec((rb, 1, SMALL_W), lambda i: (i, 0, P_SMALL // SMALL_W)),
                  pl.BlockSpec((None, rb, ML_HEADS, ML_HEAD, ML_HEAD), lambda i: (layer, i, 0, 0, 0)),
                  pl.BlockSpec((None, rb, ML_HEADS, ML_HEAD), lambda i: (layer, i, 0, 0)),
                  pl.BlockSpec((None, rb, 1, ML_HEADS), lambda i: (layer, i, 0, 0)),
                  _full((1, SMALL_W)), _full((1, ML_HEAD))],
        out_specs=[pl.BlockSpec((rb, 1, BR_W), lambda i: (i, 0, 0)),
                   pl.BlockSpec((rb, ML_HEADS, ML_HEAD, ML_HEAD), lambda i: (i, 0, 0, 0)),
                   pl.BlockSpec((rb, ML_HEADS, ML_HEAD), lambda i: (i, 0, 0)),
                   pl.BlockSpec((rb, 1, ML_HEADS), lambda i: (i, 0, 0))],
        out_shape=[jax.ShapeDtypeStruct((db, 1, BR_W), F32),
                   jax.ShapeDtypeStruct((db, ML_HEADS, ML_HEAD, ML_HEAD), F32),
                   jax.ShapeDtypeStruct((db, ML_HEADS, ML_HEAD), F32),
                   jax.ShapeDtypeStruct((db, 1, ML_HEADS), F32)],
        compiler_params=_cparams("parallel"),
        name="mlstm_step")(proj3, proj3, proj3, proj3, proj3, c0, n0, m0, bias_row, ng)


def _merge_kernel(x_ref, a_ref, b_ref, c_ref, d_ref, g0, g1, g2, g3, wb_ref, wo_ref, o_ref):
    acc = None
    for n, (br, gate) in enumerate(((a_ref, g0), (b_ref, g1), (c_ref, g2), (d_ref, g3))):
        t = _sigmoid(gate[...]) * jnp.dot(br[...].astype(BF16), wb_ref[n], preferred_element_type=F32)
        acc = t if acc is None else acc + t
    o_ref[...] = x_ref[...] + jnp.dot(acc.astype(BF16), wo_ref[...], preferred_element_type=F32)


def _merge(x, branches, proj, wb, wo, tm):
    m = x.shape[0]
    br = pl.BlockSpec((tm, BR_W), lambda i: (i, 0))
    gate = lambda n: pl.BlockSpec((tm, D_MODEL), lambda i: (i, P_GATE // D_MODEL + n))
    return pl.pallas_call(
        _merge_kernel,
        grid=(m // tm,),
        in_specs=[pl.BlockSpec((tm, D_MODEL), lambda i: (i, 0)), br, br, br, br,
                  gate(0), gate(1), gate(2), gate(3),
                  _full((N_BRANCH, BR_W, D_MODEL)), _full((D_MODEL, D_MODEL))],
        out_specs=pl.BlockSpec((tm, D_MODEL), lambda i: (i, 0)),
        out_shape=jax.ShapeDtypeStruct((m, D_MODEL), F32),
        compiler_params=_cparams("parallel"),
        name="merge")(x, *branches, proj, proj, proj, proj, wb, wo)


def _cross_prompt_kernel(x_ref, g_ref, mk_ref, mv_ref, wq_ref, wo_ref, o_ref):
    x = x_ref[...]
    h = _rms(x, g_ref[...]).astype(BF16)
    q = jnp.dot(h, wq_ref[...], preferred_element_type=F32) * (X_HEAD ** -0.5)
    outs = []
    for hd in range(X_HEADS):
        sl = slice(hd * X_HEAD, (hd + 1) * X_HEAD)
        s = _dot_nt(q[:, sl], mk_ref[:, sl])
        p = jnp.exp(s - jnp.max(s, axis=1, keepdims=True))
        p = p / jnp.sum(p, axis=1, keepdims=True)
        outs.append(_dot(p, mv_ref[:, sl]))
    o = jnp.concatenate(outs, axis=1).astype(BF16)
    o_ref[...] = x + jnp.dot(o, wo_ref[...], preferred_element_type=F32)


def _cross_prompt(x, g, mkv, wq, wo, bsz, seq, tq=512):
    nq = seq // tq
    d = D_MODEL
    return pl.pallas_call(
        _cross_prompt_kernel,
        grid=(bsz, nq),
        in_specs=[pl.BlockSpec((tq, d), lambda b, i: (b * nq + i, 0)), _full((1, d)),
                  pl.BlockSpec((N_MEM, d), lambda b, i: (b, 0)),
                  pl.BlockSpec((N_MEM, d), lambda b, i: (b, 1)),
                  _full((d, d)), _full((d, d))],
        out_specs=pl.BlockSpec((tq, d), lambda b, i: (b * nq + i, 0)),
        out_shape=jax.ShapeDtypeStruct((bsz * seq, d), F32),
        compiler_params=_cparams("parallel", "arbitrary"),
        name="cross_prompt")(x, g.reshape(1, d), mkv, mkv, wq, wo)


CROSS_ROWS = 4


def _cross_decode_kernel(q_ref, mk_ref, mv_ref, o_ref):
    halves = X_HEAD // LANE
    rows_per_tok = halves * X_HEADS

    def head_slab(ref, r, h):
        return jnp.concatenate([ref[r, pl.ds(t * X_HEADS + h, N_MEM, stride=rows_per_tok), :].astype(BF16)
                                for t in range(halves)], axis=1)

    def body(r, carry):
        q = q_ref[r] * (X_HEAD ** -0.5)
        outs = []
        for h in range(X_HEADS):
            q_h = jnp.broadcast_to(q[:, h * X_HEAD:(h + 1) * X_HEAD], (8, X_HEAD)).astype(BF16)
            s = lax.dot_general(q_h, head_slab(mk_ref, r, h), (((1,), (1,)), ((), ())),
                                preferred_element_type=F32)
            e = jnp.exp(s - jnp.max(s, axis=1, keepdims=True))
            p = (e / jnp.sum(e, axis=1, keepdims=True)).astype(BF16)
            outs.append(jnp.dot(p, head_slab(mv_ref, r, h), preferred_element_type=F32)[0:1])
        o_ref[r] = jnp.concatenate(outs, axis=1)
        return carry

    lax.fori_loop(0, CROSS_ROWS, body, 0, unroll=True)


def _mem_rows(mem):
    depth, db = mem.shape[:2]
    halves = X_HEAD // LANE
    m = mem.reshape(depth, db, N_MEM, X_HEADS, halves, LANE).transpose(0, 1, 2, 4, 3, 5)
    return m.reshape(depth, db, N_MEM * halves * X_HEADS, LANE)


def _cross_decode(q3, mem_k, mem_v, layer):
    db = q3.shape[0]
    rb = CROSS_ROWS
    d = D_MODEL
    mem = pl.BlockSpec((None, rb, mem_k.shape[2], LANE), lambda i: (layer, i, 0, 0))
    return pl.pallas_call(
        _cross_decode_kernel,
        grid=(db // rb,),
        in_specs=[pl.BlockSpec((rb, 1, d), lambda i: (i, 0, 0)), mem, mem],
        out_specs=pl.BlockSpec((rb, 1, d), lambda i: (i, 0, 0)),
        out_shape=jax.ShapeDtypeStruct((db, 1, d), F32),
        compiler_params=_cparams("parallel"),
        name="cross_decode")(q3, mem_k, mem_v)


def _swiglu_kernel(x_ref, g_ref, wg_ref, wu_ref, wd_ref, o_ref, h_ref, acc_ref):
    j = pl.program_id(1)

    @pl.when(j == 0)
    def _():
        h_ref[...] = _rms(x_ref[...], g_ref[...]).astype(BF16)
        acc_ref[...] = jnp.zeros(acc_ref.shape, F32)

    h = h_ref[...]
    gate = jnp.dot(h, wg_ref[...], preferred_element_type=F32)
    up = jnp.dot(h, wu_ref[...], preferred_element_type=F32)
    acc_ref[...] += jnp.dot((_silu(gate) * up).astype(BF16), wd_ref[...], preferred_element_type=F32)

    @pl.when(j == pl.num_programs(1) - 1)
    def _():
        o_ref[...] = x_ref[...] + acc_ref[...]


def _swiglu(x, g, wgu, wd, tm, tf=D_FF // 2):
    m, d = x.shape
    nf = D_FF // tf
    return pl.pallas_call(
        _swiglu_kernel,
        grid=(m // tm, nf),
        in_specs=[pl.BlockSpec((tm, d), lambda i, j: (i, 0)), _full((1, d)),
                  pl.BlockSpec((d, tf), lambda i, j: (0, j)),
                  pl.BlockSpec((d, tf), lambda i, j: (0, nf + j)),
                  pl.BlockSpec((tf, d), lambda i, j: (j, 0))],
        out_specs=pl.BlockSpec((tm, d), lambda i, j: (i, 0)),
        out_shape=jax.ShapeDtypeStruct((m, d), F32),
        scratch_shapes=[pltpu.VMEM((tm, d), BF16), pltpu.VMEM((tm, d), F32)],
        compiler_params=_cparams("parallel", "arbitrary"),
        name="swiglu")(x, g.reshape(1, d), wgu, wgu, wd)


def _pack_w_in(w):
    offs = np.cumsum((0,) + IN_SPLITS)
    seg = lambda i: w[:, offs[i]:offs[i + 1]]
    small = jnp.concatenate([seg(5), seg(8), seg(9), seg(14), seg(15),
                             jnp.zeros((w.shape[0], SMALL_W - 24), w.dtype)], axis=1)
    order = (0, 1, 2, 3, 4, 6, 7, 10, 11, 12, 13, 16)
    return jnp.concatenate([seg(i) for i in order] + [small], axis=1).astype(BF16)


def _small_row(parts):
    row = jnp.zeros((SMALL_W,), F32)
    for off, val in parts:
        row = lax.dynamic_update_slice(row, val.astype(F32), (off,))
    return row.reshape(1, SMALL_W)


def _layer_params(l, p):
    lp = dict(
        w_in=_pack_w_in(p["w_in"][l]),
        g_mix=p["g_mix"][l],
        lam=tuple(p[n][l].reshape(1, DA_HEAD) for n in ("da_lq1", "da_lk1", "da_lq2", "da_lk2")),
        lam_init=0.8 - 0.6 * math.exp(-0.3 * l),
        sub_g=p["da_sub_g"][l],
        bias_row=_small_row(((S_DT, p["ssm_dt_bias"][l]), (S_GA, p["gdn_dt_bias"][l]),
                             (S_MI, p["ml_i_bias"][l]), (S_MF, p["ml_f_bias"][l]))),
        alog_row=_small_row(((S_DT, p["ssm_a_log"][l]), (S_GA, p["gdn_a_log"][l]))),
        ssm_cw=p["ssm_conv_w"][l], ssm_cb=p["ssm_conv_b"][l].reshape(1, SSM_CONV_CH),
        ssm_dsk=jnp.repeat(p["ssm_d"][l], SSM_HEAD).reshape(1, BR_W),
        ssm_ng=p["ssm_norm_g"][l].reshape(1, BR_W),
        gdn_cw=p["gdn_conv_w"][l], gdn_ng=p["gdn_norm_g"][l].reshape(1, GDN_HEAD),
        ml_ng=p["ml_norm_g"][l].reshape(1, ML_HEAD),
        w_branch=p["w_branch"][l].astype(BF16), w_out=p["w_out"][l].astype(BF16),
        g_cross=p["g_cross"][l], w_cq=p["w_cq"][l].astype(BF16), w_co=p["w_co"][l].astype(BF16),
        g_ffn=p["g_ffn"][l], w_gu=p["w_gu"][l].astype(BF16), w_down=p["w_down"][l].astype(BF16),
    )
    return lp


def _prompt_layer(x, lp, mkv, bsz, seq):
    proj = _norm_matmul(x, lp["g_mix"], lp["w_in"], tm=1024 if (bsz * seq) % 1024 == 0 else bsz * seq,
                        tn=1152, name="in_proj")
    o_da = _da_prompt(proj, lp["lam"], lp["sub_g"], bsz, seq, lp["lam_init"])
    o_ssm, ssm = _ssd_prompt(proj, lp["ssm_cw"], lp["ssm_cb"], lp["bias_row"], lp["alog_row"], lp["ssm_dsk"],
                             lp["ssm_ng"], bsz, seq)
    o_gdn, gdn = _gdn_prompt(proj, lp["gdn_cw"], lp["bias_row"], lp["alog_row"], lp["gdn_ng"], bsz, seq)
    o_ml, ml_c, ml_n, ml_m = _mlstm_prompt(proj, lp["bias_row"], lp["ml_ng"], bsz, seq)
    x = _merge(x, (o_da, o_ssm, o_gdn, o_ml), proj, lp["w_branch"], lp["w_out"], tm=256)
    x = _cross_prompt(x, lp["g_cross"], mkv, lp["w_cq"], lp["w_co"], bsz, seq, tq=min(512, seq))
    x = _swiglu(x, lp["g_ffn"], lp["w_gu"], lp["w_down"], tm=512 if (bsz * seq) % 512 == 0 else 256)
    p3 = proj.reshape(bsz, seq, PACK_W)
    new = dict(
        k=p3[:, :, P_K:P_K + BR_W].reshape(bsz, seq, DA_HEADS, 2, DA_HEAD),
        v=p3[:, :, P_V:P_V + BR_W].reshape(bsz, seq, DA_HEADS, 2 * DA_HEAD),
        ssm_conv=p3[:, seq - (CONV_W - 1):, P_XBC:P_XBC + SSM_CONV_CH], ssm=ssm,
        gdn_conv=p3[:, seq - (CONV_W - 1):, P_GQKV:P_GQKV + GDN_CONV_CH], gdn=gdn,
        ml_c=ml_c, ml_n=ml_n, ml_m=ml_m[:, :ML_HEADS, 0])
    return x, new


def _sample_layer(x, lp, l, caches, states):
    db = x.shape[0]
    cache_k, cache_v, page_table, mem_k, mem_v = caches
    proj = _norm_matmul(x, lp["g_mix"], lp["w_in"], tm=db, tn=1152, name="in_proj_s")
    proj3 = proj.reshape(db, 1, PACK_W)
    o_da = _da_decode(proj3, lp["lam"], lp["sub_g"], cache_k, cache_v, page_table, l, lp["lam_init"])
    o_ssm, ssm_conv, ssm = _ssd_step(proj3, states["ssm_conv"], states["ssm"], l, lp["ssm_cw"], lp["ssm_cb"],
                                     lp["bias_row"], lp["alog_row"], lp["ssm_dsk"], lp["ssm_ng"])
    o_gdn, gdn_conv, gdn = _gdn_step(proj3, states["gdn_conv"], states["gdn"], l, lp["gdn_cw"],
                                     lp["bias_row"], lp["alog_row"], lp["gdn_ng"])
    o_ml, ml_c, ml_n, ml_m = _mlstm_step(proj3, states["ml_c"], states["ml_n"], states["ml_m"], l,
                                         lp["bias_row"], lp["ml_ng"])
    x = _merge(x, (o_da, o_ssm.reshape(db, BR_W), o_gdn.reshape(db, BR_W), o_ml.reshape(db, BR_W)),
               proj, lp["w_branch"], lp["w_out"], tm=db)
    q = _norm_matmul(x, lp["g_cross"], lp["w_cq"], tm=db, tn=D_MODEL, name="cross_q_s")
    att = _cross_decode(q.reshape(db, 1, D_MODEL), mem_k, mem_v, l)
    x = _matmul_residual(x, att.reshape(db, D_MODEL), lp["w_co"], tm=db, name="cross_o_s")
    x = _swiglu(x, lp["g_ffn"], lp["w_gu"], lp["w_down"], tm=db)
    new = dict(
        k=proj[:, P_K:P_K + BR_W].reshape(db, 1, DA_HEADS, 2, DA_HEAD),
        v=proj[:, P_V:P_V + BR_W].reshape(db, 1, DA_HEADS, 2 * DA_HEAD),
        ssm_conv=ssm_conv, ssm=ssm, gdn_conv=gdn_conv, gdn=gdn,
        ml_c=ml_c, ml_n=ml_n, ml_m=ml_m.reshape(db, ML_HEADS))
    return x, new


_STATE_ORDER = ("ssm_conv", "ssm", "gdn_conv", "gdn", "ml_c", "ml_n", "ml_m")


def kernel(x_prompt, x_sample, cache_k, cache_v, cache_mem_k, cache_mem_v, state_ssm_conv, state_ssm, state_gdn_conv, state_gdn, state_mlstm_c, state_mlstm_n, state_mlstm_m, page_table, mem_prompt, g_mix, w_in, da_lq1, da_lk1, da_lq2, da_lk2, da_sub_g, ssm_conv_w, ssm_conv_b, ssm_dt_bias, ssm_a_log, ssm_d, ssm_norm_g, gdn_conv_w, gdn_dt_bias, gdn_a_log, gdn_norm_g, ml_i_bias, ml_f_bias, ml_norm_g, w_branch, w_out, g_cross, g_mem, w_cq, w_ckv, w_co, g_ffn, w_gu, w_down, g_final):
    p = dict(g_mix=g_mix, w_in=w_in, da_lq1=da_lq1, da_lk1=da_lk1, da_lq2=da_lq2, da_lk2=da_lk2,
             da_sub_g=da_sub_g, ssm_conv_w=ssm_conv_w, ssm_conv_b=ssm_conv_b, ssm_dt_bias=ssm_dt_bias,
             ssm_a_log=ssm_a_log, ssm_d=ssm_d, ssm_norm_g=ssm_norm_g, gdn_conv_w=gdn_conv_w,
             gdn_dt_bias=gdn_dt_bias, gdn_a_log=gdn_a_log, gdn_norm_g=gdn_norm_g,
             ml_i_bias=ml_i_bias, ml_f_bias=ml_f_bias, ml_norm_g=ml_norm_g,
             w_branch=w_branch, w_out=w_out, g_cross=g_cross, w_cq=w_cq, w_co=w_co,
             g_ffn=g_ffn, w_gu=w_gu, w_down=w_down)
    depth = w_in.shape[0]
    bsz, seq, d = x_prompt.shape
    db = x_sample.shape[0]
    n_mem = mem_prompt.shape[1]
    lps = [_layer_params(l, p) for l in range(depth)]

    mem2 = mem_prompt.reshape(bsz * n_mem, d)
    xp = x_prompt.reshape(bsz * seq, d)
    p_new = {n: [] for n in ("k", "v", "mem_k", "mem_v") + _STATE_ORDER}
    for l in range(depth):
        mkv = _norm_matmul(mem2, g_mem[l], w_ckv[l].astype(BF16), tm=min(1024, bsz * n_mem), tn=1024, name="mem_kv")
        xp, new = _prompt_layer(xp, lps[l], mkv, bsz, seq)
        mkv5 = mkv.reshape(bsz, n_mem, 2, X_HEADS, X_HEAD)
        new["mem_k"] = mkv5[:, :, 0]
        new["mem_v"] = mkv5[:, :, 1]
        for n in p_new:
            p_new[n].append(new[n])
    y_prompt = _final_norm(xp, g_final, tm=512 if (bsz * seq) % 512 == 0 else bsz * seq, name="final_norm").reshape(bsz, seq, d)

    n_pool = cache_k.shape[1]
    caches = (cache_k.transpose(0, 1, 3, 4, 5, 2).reshape(depth, n_pool, BR_W, PAGE_SIZE),
              cache_v.reshape(depth, n_pool, PAGE_SIZE * DA_HEADS, 2 * DA_HEAD),
              page_table, _mem_rows(cache_mem_k), _mem_rows(cache_mem_v))
    states = dict(ssm_conv=state_ssm_conv, ssm=state_ssm, gdn_conv=state_gdn_conv, gdn=state_gdn,
                  ml_c=state_mlstm_c, ml_n=state_mlstm_n, ml_m=state_mlstm_m.reshape(depth, db, 1, ML_HEADS))
    xs = x_sample.reshape(db, d)
    s_new = {n: [] for n in ("k", "v") + _STATE_ORDER}
    for l in range(depth):
        xs, new = _sample_layer(xs, lps[l], l, caches, states)
        for n in s_new:
            s_new[n].append(new[n])
    y_sample = _final_norm(xs, g_final, tm=db, name="final_norm_s").reshape(db, 1, d)

    stk = lambda dct, n: jnp.stack(dct[n])
    return (y_prompt, y_sample,
            stk(p_new, "k"), stk(p_new, "v"), stk(p_new, "mem_k"), stk(p_new, "mem_v"),
            *(stk(p_new, n) for n in _STATE_ORDER),
            stk(s_new, "k"), stk(s_new, "v"), *(stk(s_new, n) for n in _STATE_ORDER))
```

```python
import functools
import math

import numpy as np
import jax
import jax.numpy as jnp
from jax import lax
from jax.experimental import pallas as pl
from jax.experimental.pallas import tpu as pltpu

F32 = jnp.float32
BF16 = jnp.bfloat16

D_MODEL = 1024
DEPTH = 4
PAGE_SIZE = 128
EPS = 1e-6
N_MEM = 256
CONV_W = 4
N_BRANCH = 4
BR_W = D_MODEL // 2
DA_HEADS = 4
DA_HEAD = BR_W // (2 * DA_HEADS)
SSM_HEAD = 64
SSM_HEADS = BR_W // SSM_HEAD
SSM_GROUPS = 2
SSM_STATE = 128
SSM_CONV_CH = BR_W + 2 * SSM_GROUPS * SSM_STATE
SSM_CHUNK = 128
GDN_HEADS = 4
GDN_HEAD = BR_W // GDN_HEADS
GDN_CONV_CH = 3 * BR_W
GDN_CHUNK = 64
ML_HEADS = 4
ML_HEAD = BR_W // ML_HEADS
ML_CHUNK = 128
X_HEADS = 4
X_HEAD = D_MODEL // X_HEADS
D_FF = -(-8 * D_MODEL // (3 * 256)) * 256

IN_SPLITS = (BR_W, BR_W, BR_W, BR_W, SSM_CONV_CH, SSM_HEADS, GDN_CONV_CH, BR_W, GDN_HEADS, GDN_HEADS,
             BR_W, BR_W, BR_W, BR_W, ML_HEADS, ML_HEADS, N_BRANCH * D_MODEL)

P_Q, P_K, P_V = 0, BR_W, 2 * BR_W
P_SZ = 3 * BR_W
P_XBC = P_SZ + BR_W
P_GQKV = P_XBC + SSM_CONV_CH
P_GZ = P_GQKV + GDN_CONV_CH
P_MQ = P_GZ + BR_W
P_GATE = P_MQ + 4 * BR_W
P_SMALL = P_GATE + N_BRANCH * D_MODEL
SMALL_W = 256
PACK_W = P_SMALL + SMALL_W
S_DT, S_GA, S_GB, S_MI, S_MF = 0, 8, 12, 16, 20

LANE = 128
VMEM_LIMIT = 56 * 1024 * 1024


def _cparams(*sem):
    return pltpu.CompilerParams(dimension_semantics=sem, vmem_limit_bytes=VMEM_LIMIT)


def _dot(a, b):
    return jnp.dot(a.astype(BF16), b.astype(BF16), preferred_element_type=F32)


def _dot_nt(a, b):
    return lax.dot_general(a.astype(BF16), b.astype(BF16), (((1,), (1,)), ((), ())), preferred_element_type=F32)


def _dot_f32(a, b):
    return jnp.dot(a, b, precision=lax.Precision.HIGHEST, preferred_element_type=F32)


def _dot_split(a, b_bf16):
    hi = a.astype(BF16)
    lo = (a - hi.astype(F32)).astype(BF16)
    return (jnp.dot(hi, b_bf16, preferred_element_type=F32) + jnp.dot(lo, b_bf16, preferred_element_type=F32))


def _sigmoid(x):
    return 1.0 / (1.0 + jnp.exp(-x))


def _silu(x):
    return x * _sigmoid(x)


def _softplus(x):
    return jnp.maximum(x, 0.0) + jnp.log(1.0 + jnp.exp(-jnp.abs(x)))


def _rms(x, g):
    return x * lax.rsqrt(jnp.mean(x * x, axis=-1, keepdims=True) + EPS) * g


def _iota2(shape, dim):
    return lax.broadcasted_iota(jnp.int32, shape, dim)


def _row_to_col(x):
    n = x.shape[1]
    eye = _iota2((n, n), 0) == _iota2((n, n), 1)
    return jnp.sum(jnp.where(eye, jnp.broadcast_to(x, (n, n)), 0.0), axis=1, keepdims=True)


def _col_to_row(x):
    n = x.shape[0]
    eye = _iota2((n, n), 0) == _iota2((n, n), 1)
    return jnp.sum(jnp.where(eye, jnp.broadcast_to(x, (n, n)), 0.0), axis=0, keepdims=True)


def _tril_f32(c):
    return (_iota2((c, c), 0) >= _iota2((c, c), 1)).astype(F32)


def _head_expander(n_heads, width):
    rows = _iota2((LANE, n_heads * width), 0)
    cols = _iota2((LANE, n_heads * width), 1)
    return (rows * width <= cols) & (cols < (rows + 1) * width)


def _lam(lq1, lk1, lq2, lk2, lam_init):
    return (jnp.exp(jnp.sum(lq1[...] * lk1[...], axis=1, keepdims=True))
            - jnp.exp(jnp.sum(lq2[...] * lk2[...], axis=1, keepdims=True)) + lam_init)


def _norm_matmul_kernel(x_ref, g_ref, w_ref, o_ref, h_ref):
    @pl.when(pl.program_id(1) == 0)
    def _():
        h_ref[...] = _rms(x_ref[...], g_ref[...]).astype(BF16)

    o_ref[...] = jnp.dot(h_ref[...], w_ref[...], preferred_element_type=F32)


def _norm_matmul(x, g, w, tm, tn, name):
    m, k = x.shape
    n = w.shape[1]
    return pl.pallas_call(
        _norm_matmul_kernel,
        grid=(m // tm, n // tn),
        in_specs=[pl.BlockSpec((tm, k), lambda i, j: (i, 0)),
                  pl.BlockSpec((1, k), lambda i, j: (0, 0)),
                  pl.BlockSpec((k, tn), lambda i, j: (0, j))],
        out_specs=pl.BlockSpec((tm, tn), lambda i, j: (i, j)),
        out_shape=jax.ShapeDtypeStruct((m, n), F32),
        scratch_shapes=[pltpu.VMEM((tm, k), BF16)],
        compiler_params=_cparams("parallel", "arbitrary"),
        name=name)(x, g.reshape(1, k), w)


def _matmul_res_kernel(x_ref, a_ref, w_ref, o_ref):
    o_ref[...] = x_ref[...] + jnp.dot(a_ref[...].astype(BF16), w_ref[...], preferred_element_type=F32)


def _matmul_residual(x, a, w, tm, name):
    m, n = x.shape
    k = a.shape[1]
    return pl.pallas_call(
        _matmul_res_kernel,
        grid=(m // tm,),
        in_specs=[pl.BlockSpec((tm, n), lambda i: (i, 0)),
                  pl.BlockSpec((tm, k), lambda i: (i, 0)),
                  pl.BlockSpec((k, n), lambda i: (0, 0))],
        out_specs=pl.BlockSpec((tm, n), lambda i: (i, 0)),
        out_shape=jax.ShapeDtypeStruct((m, n), F32),
        compiler_params=_cparams("parallel"),
        name=name)(x, a, w)


def _final_norm_kernel(x_ref, g_ref, o_ref):
    o_ref[...] = _rms(x_ref[...], g_ref[...])


def _final_norm(x, g, tm, name):
    m, n = x.shape
    return pl.pallas_call(
        _final_norm_kernel,
        grid=(m // tm,),
        in_specs=[pl.BlockSpec((tm, n), lambda i: (i, 0)), pl.BlockSpec((1, n), lambda i: (0, 0))],
        out_specs=pl.BlockSpec((tm, n), lambda i: (i, 0)),
        out_shape=jax.ShapeDtypeStruct((m, n), F32),
        compiler_params=_cparams("parallel"),
        name=name)(x, g.reshape(1, n))


def _da_prompt_kernel(lq1, lk1, lq2, lk2, subg_ref, q_ref, k_ref, v_ref, o_ref, kb_ref, vt_ref, acc_ref,
                      *, lam_init, tq, cw):
    qi = pl.program_id(2)
    hw = 2 * DA_HEAD

    @pl.when(qi == 0)
    def _():
        kb_ref[...] = k_ref[...].astype(BF16)
        for t in range(vt_ref.shape[0]):
            vt_ref[t] = v_ref[t * tq:(t + 1) * tq, :].T.astype(BF16)

    lam = _lam(lq1, lk1, lq2, lk2, lam_init)
    qt = (q_ref[...] * (DA_HEAD ** -0.5)).T
    sub = _iota2((hw, tq), 0)
    q2t = jnp.concatenate([jnp.where(sub < DA_HEAD, qt, 0.0), jnp.where(sub >= DA_HEAD, qt, 0.0)],
                          axis=1).astype(BF16)
    acc_ref[...] = jnp.zeros(acc_ref.shape, F32)

    nch = 2 * tq // cw

    chunks = [slice(c * cw, (c + 1) * cw) for c in range(nch)]

    def scores(j):
        kj = kb_ref[pl.ds(pl.multiple_of(j * tq, tq), tq), :]
        return tuple(jnp.dot(kj, q2t[:, cols], preferred_element_type=F32) for cols in chunks)

    def update(j, sts, stats, masked):
        new_stats, scaled = [], []
        for c, st in enumerate(sts):
            m, l = stats[c]
            if masked:
                qpos = _iota2((tq, cw), 1) + (c * cw) % tq
                st = jnp.where(_iota2((tq, cw), 0) <= qpos, st, -jnp.inf)
            m_new = jnp.maximum(m, jnp.max(st, axis=0, keepdims=True))
            alpha = jnp.exp(m - m_new)
            p = jnp.exp(st - m_new)
            new_stats.append((m_new, alpha * l + jnp.sum(p, axis=0, keepdims=True)))
            scaled.append((alpha, p.astype(BF16)))
        vtj = vt_ref[j]
        for cols, (alpha, p) in zip(chunks, scaled):
            acc_ref[:, cols] = alpha * acc_ref[:, cols] + jnp.dot(vtj, p, preferred_element_type=F32)
        return tuple(new_stats)

    def body(j, carry):
        sts, stats = carry
        nxt = scores(j + 1)
        return nxt, update(j, sts, stats, False)

    init = tuple((jnp.full((1, cw), -jnp.inf, F32), jnp.zeros((1, cw), F32)) for _ in range(nch))
    sts, stats = lax.fori_loop(0, qi, body, (scores(0), init))
    stats = update(qi, sts, stats, True)
    l = jnp.concatenate([ml[1] for ml in stats], axis=1)
    ot = acc_ref[...] / l
    odt = ot[:, :tq] - lam * ot[:, tq:]
    yt = odt * lax.rsqrt(jnp.mean(odt * odt, axis=0, keepdims=True) + EPS) * subg_ref[...] * (1.0 - lam_init)
    o_ref[...] = yt.T


def _da_prompt(proj, lam_params, sub_g, bsz, seq, lam_init, tq=256, cw=128):
    nq = seq // tq
    hw = 2 * DA_HEAD
    small = pl.BlockSpec((1, DA_HEAD), lambda b, h, i: (0, 0))
    kv_spec = lambda off: pl.BlockSpec((seq, hw), lambda b, h, i: (b, off // hw + h))
    return pl.pallas_call(
        functools.partial(_da_prompt_kernel, lam_init=lam_init, tq=tq, cw=cw),
        grid=(bsz, DA_HEADS, nq),
        in_specs=[small, small, small, small,
                  pl.BlockSpec((hw, 1), lambda b, h, i: (0, 0)),
                  pl.BlockSpec((tq, hw), lambda b, h, i: (b * nq + i, P_Q // hw + h)),
                  kv_spec(P_K), kv_spec(P_V)],
        out_specs=pl.BlockSpec((tq, hw), lambda b, h, i: (b * nq + i, h)),
        out_shape=jax.ShapeDtypeStruct((bsz * seq, BR_W), F32),
        scratch_shapes=[pltpu.VMEM((seq, hw), BF16), pltpu.VMEM((nq, hw, tq), BF16), pltpu.VMEM((hw, 2 * tq), F32)],
        compiler_params=_cparams("parallel", "parallel", "arbitrary"),
        name="da_prompt")(*lam_params, sub_g.reshape(hw, 1), proj, proj, proj)


def _da_decode_kernel(pt_ref, lq1, lk1, lq2, lk2, subg_ref, q_ref, kn_ref, vn_ref, *rest, lam_init, n_pages):
    del pt_ref
    kt_refs = rest[:n_pages]
    v_refs = rest[n_pages:2 * n_pages]
    o_ref = rest[2 * n_pages]
    nh = DA_HEADS
    hw = 2 * DA_HEAD
    lam = _lam(lq1, lk1, lq2, lk2, lam_init)
    q = q_ref[...] * (DA_HEAD ** -0.5)
    r = _iota2((2 * nh, BR_W), 0)
    seg = _iota2((2 * nh, BR_W), 1) >> 6
    q_bd = jnp.where(((seg & 1) == (r >> 2)) & ((seg >> 1) == (r & 3)), jnp.broadcast_to(q, (2 * nh, BR_W)), 0.0)
    s_new = jnp.sum(q_bd * kn_ref[...], axis=1, keepdims=True)
    qb = q_bd.astype(BF16)
    s = jnp.concatenate([jnp.dot(qb, kt_refs[j][...].astype(BF16), preferred_element_type=F32)
                         for j in range(n_pages)], axis=1)
    m = jnp.maximum(jnp.max(s, axis=1, keepdims=True), s_new)
    e = jnp.exp(s - m)
    e_new = jnp.exp(s_new - m)
    l = jnp.sum(e, axis=1, keepdims=True) + e_new
    coef = jnp.where(_iota2((2 * nh, 1), 0) < nh, 1.0, -lam) / l
    w = e * coef
    w_new = e_new * coef
    p = (w + pltpu.roll(w, nh, axis=0)).astype(BF16)
    p_new = w_new + pltpu.roll(w_new, nh, axis=0)
    g = subg_ref[...]
    vn = vn_ref[...]
    outs = []
    for h in range(nh):
        v_h = jnp.concatenate([v_refs[j][pl.ds(h, PAGE_SIZE, stride=nh), :].astype(BF16) for j in range(n_pages)],
                              axis=0)
        o_h = (jnp.dot(p, v_h, preferred_element_type=F32)[h:h + 1, :]
               + p_new[h:h + 1, :] * vn[:, h * hw:(h + 1) * hw])
        outs.append(_rms(o_h, g) * (1.0 - lam_init))
    o_ref[...] = jnp.concatenate(outs, axis=1)


def _da_decode(proj3, lam_params, sub_g, cache_kt, cache_v, page_table, layer, lam_init):
    db = proj3.shape[0]
    n_pages = page_table.shape[1]
    hw = 2 * DA_HEAD
    small = pl.BlockSpec((1, DA_HEAD), lambda b, pt: (0, 0))
    row = lambda off: pl.BlockSpec((None, 1, BR_W), lambda b, pt: (b, 0, off // BR_W))

    def page(j):
        return pl.BlockSpec((None, None, BR_W, PAGE_SIZE), lambda b, pt: (layer, pt[b * n_pages + j], 0, 0))

    grid_spec = pltpu.PrefetchScalarGridSpec(
        num_scalar_prefetch=1,
        grid=(db,),
        in_specs=[small, small, small, small, pl.BlockSpec((1, hw), lambda b, pt: (0, 0)),
                  row(P_Q), row(P_K), row(P_V)]
                 + [page(j) for j in range(n_pages)] + [page(j) for j in range(n_pages)],
        out_specs=pl.BlockSpec((None, 1, BR_W), lambda b, pt: (b, 0, 0)))
    out = pl.pallas_call(
        functools.partial(_da_decode_kernel, lam_init=lam_init, n_pages=n_pages),
        grid_spec=grid_spec,
        out_shape=jax.ShapeDtypeStruct((db, 1, BR_W), F32),
        compiler_params=_cparams("parallel"),
        name="da_decode")(page_table.reshape(-1), *lam_params, sub_g.reshape(1, hw), proj3, proj3, proj3,
                          *([cache_kt] * n_pages), *([cache_v] * n_pages))
    return out.reshape(db, BR_W)


def _conv_window(win_ref, x_ref, cw, c, zi):
    @pl.when(zi == 0)
    def _():
        win_ref[0:8, :] = jnp.zeros((8, win_ref.shape[1]), F32)

    @pl.when(zi > 0)
    def _():
        win_ref[0:8, :] = win_ref[c:c + 8, :]

    win_ref[8:8 + c, :] = x_ref[...]
    y = win_ref[5:5 + c, :] * cw[0:1, :]
    for j in range(1, CONV_W):
        y = y + win_ref[5 + j:5 + j + c, :] * cw[j:j + 1, :]
    return y


def _ssd_prompt_kernel(z_ref, xbc_ref, sm_ref, cw_ref, cb_ref, bias_ref, alog_ref, dsk_ref, ng_ref,
                       o_ref, st_ref, win_ref, yd_ref, yo_ref, *, c):
    zi = pl.program_id(1)

    @pl.when(zi == 0)
    def _():
        st_ref[...] = jnp.zeros(st_ref.shape, F32)

    xbc = _silu(_conv_window(win_ref, xbc_ref, cw_ref[...], c, zi) + cb_ref[...])
    xs = xbc[:, :BR_W]
    gs = SSM_GROUPS * SSM_STATE
    bm = xbc[:, BR_W:BR_W + gs]
    cm = xbc[:, BR_W + gs:]
    lane = _iota2((1, LANE), 1)
    head_lane = lane < SSM_HEADS
    dt = _softplus(sm_ref[:, :LANE] + bias_ref[:, :LANE])
    a = jnp.where(head_lane, -jnp.exp(alog_ref[:, :LANE]), 0.0)
    dt = jnp.where(head_lane, dt, 0.0)
    acs = _dot_f32(_tril_f32(c), dt * a)
    acs_t = acs.T
    acs_last = acs[c - 1:c, :]
    expander = _head_expander(SSM_HEADS, SSM_HEAD).astype(F32)
    xdt = xs * _dot_f32(dt, expander)
    w_t = (xdt * _dot_f32(jnp.exp(acs_last - acs), expander)).T
    causal = _iota2((c, c), 0) >= _iota2((c, c), 1)
    rep = SSM_HEADS // SSM_GROUPS
    heads = range(SSM_HEADS)
    sl = [slice(h * SSM_HEAD, (h + 1) * SSM_HEAD) for h in heads]
    bm_g = [bm[:, g * SSM_STATE:(g + 1) * SSM_STATE].astype(BF16) for g in range(SSM_GROUPS)]
    cm_g = [cm[:, g * SSM_STATE:(g + 1) * SSM_STATE].astype(BF16) for g in range(SSM_GROUPS)]
    cb = [_dot_nt(cm_g[g], bm_g[g]) for g in range(SSM_GROUPS)]
    h_prev = [st_ref[h] for h in heads]
    y_off = [_dot_nt(cm_g[h // rep], h_prev[h]) for h in heads]
    st_add = [_dot(w_t[sl[h], :], bm_g[h // rep]) for h in heads]
    decay = [jnp.exp(jnp.where(causal, acs[:, h:h + 1] - acs_t[h:h + 1, :], -jnp.inf)) for h in heads]
    y_diag = [_dot(cb[h // rep] * decay[h], xdt[:, sl[h]]) for h in heads]
    for h in heads:
        yd_ref[:, sl[h]] = y_diag[h]
        yo_ref[:, sl[h]] = y_off[h]
        st_ref[h] = h_prev[h] * jnp.exp(acs_last[:, h:h + 1]) + st_add[h]
    y = yd_ref[...] + yo_ref[...] * _dot_f32(jnp.exp(acs), expander) + dsk_ref[...] * xs
    y = y * _silu(z_ref[...])
    gw = BR_W // SSM_GROUPS
    ng = ng_ref[...]
    o_ref[...] = jnp.concatenate([_rms(y[:, g * gw:(g + 1) * gw], ng[:, g * gw:(g + 1) * gw])
                                  for g in range(SSM_GROUPS)], axis=1)


def _full(shape):
    return pl.BlockSpec(shape, lambda *a: (0,) * len(shape))


def _ssd_prompt(proj, cw, cb, bias_row, alog_row, dsk, ng, bsz, seq):
    c = SSM_CHUNK
    nc = seq // c
    return pl.pallas_call(
        functools.partial(_ssd_prompt_kernel, c=c),
        grid=(bsz, nc),
        in_specs=[pl.BlockSpec((c, BR_W), lambda b, z: (b * nc + z, P_SZ // BR_W)),
                  pl.BlockSpec((c, SSM_CONV_CH), lambda b, z: (b * nc + z, P_XBC // SSM_CONV_CH)),
                  pl.BlockSpec((c, SMALL_W), lambda b, z: (b * nc + z, P_SMALL // SMALL_W)),
                  _full((CONV_W, SSM_CONV_CH)), _full((1, SSM_CONV_CH)), _full((1, SMALL_W)), _full((1, SMALL_W)),
                  _full((1, BR_W)), _full((1, BR_W))],
        out_specs=[pl.BlockSpec((c, BR_W), lambda b, z: (b * nc + z, 0)),
                   pl.BlockSpec((None, SSM_HEADS, SSM_HEAD, SSM_STATE), lambda b, z: (b, 0, 0, 0))],
        out_shape=[jax.ShapeDtypeStruct((bsz * seq, BR_W), F32),
                   jax.ShapeDtypeStruct((bsz, SSM_HEADS, SSM_HEAD, SSM_STATE), F32)],
        scratch_shapes=[pltpu.VMEM((c + 8, SSM_CONV_CH), F32), pltpu.VMEM((c, BR_W), F32), pltpu.VMEM((c, BR_W), F32)],
        compiler_params=_cparams("parallel", "arbitrary"),
        name="ssd_prompt")(proj, proj, proj, cw, cb, bias_row, alog_row, dsk, ng)


def _inv_unit_lower_minus_eye(mats, n, c):
    row = _iota2((n, n), 0)
    col = _iota2((n, n), 1)
    sh = 4
    ps = [jnp.where((row >> sh) == (col >> sh), a, 0.0) for a in mats]
    ys = [-p for p in ps]
    for _ in range(sh - 1):
        ps = [_dot(p, p) for p in ps]
        yp = [_dot(y, p) for y, p in zip(ys, ps)]
        ys = [y + p + t for y, p, t in zip(ys, ps, yp)]
    while (1 << sh) < c:
        mask = ((row >> (sh + 1)) == (col >> (sh + 1))) & ((row >> sh) != (col >> sh))
        offs = [jnp.where(mask, a, 0.0) for a in mats]
        ts = [off + _dot(y, off) for y, off in zip(ys, offs)]
        ty = [_dot(t, y) for t, y in zip(ts, ys)]
        ys = [y - (t + u) for y, t, u in zip(ys, ts, ty)]
        sh += 1
    return ys


def _gdn_prompt_kernel(qkv_ref, z_ref, sm_ref, cw_ref, bias_ref, alog_ref, ng_ref, o_ref, st_ref, win_ref, *, c):
    zi = pl.program_id(1)
    nh = GDN_HEADS
    hd = GDN_HEAD
    heads = range(nh)

    @pl.when(zi == 0)
    def _():
        st_ref[...] = jnp.zeros(st_ref.shape, F32)

    qkv = _silu(_conv_window(win_ref, qkv_ref, cw_ref[...], c, zi))
    pre = sm_ref[:, :LANE] + bias_ref[:, :LANE]
    g_all = -jnp.exp(alog_ref[:, :LANE]) * _softplus(pre)
    beta_all = _sigmoid(sm_ref[:, :LANE])
    gc = _dot_f32(_tril_f32(c), g_all)
    gc_t = gc.T
    row = _iota2((c, c), 0)
    col = _iota2((c, c), 1)
    l2 = lambda t: t * lax.rsqrt(jnp.sum(t * t, axis=-1, keepdims=True) + EPS)

    q = [l2(qkv[:, h * hd:(h + 1) * hd]) * (hd ** -0.5) for h in heads]
    k = [l2(qkv[:, BR_W + h * hd:BR_W + (h + 1) * hd]) for h in heads]
    v = [qkv[:, 2 * BR_W + h * hd:2 * BR_W + (h + 1) * hd] for h in heads]
    g_col = [gc[:, S_GA + h:S_GA + h + 1] for h in heads]
    g_last = [gc[c - 1:c, S_GA + h:S_GA + h + 1] for h in heads]
    beta = [beta_all[:, S_GB + h:S_GB + h + 1] for h in heads]
    decay = [jnp.exp(jnp.where(row >= col, g_col[h] - gc_t[S_GA + h:S_GA + h + 1, :], -jnp.inf)) for h in heads]
    kb = [k[h] * beta[h] for h in heads]
    eg = [jnp.exp(g_col[h]) for h in heads]
    a_low = [jnp.where(row > col, _dot_nt(kb[h], k[h]) * decay[h], 0.0) for h in heads]
    attn = [_dot_nt(q[h], k[h]) * decay[h] for h in heads]
    t_dev = _inv_unit_lower_minus_eye(a_low, c, c)
    vb = [v[h] * beta[h] for h in heads]
    kbe = [kb[h] * eg[h] for h in heads]
    u = [vb[h] + _dot(t_dev[h], vb[h]) for h in heads]
    w = [kbe[h] + _dot(t_dev[h], kbe[h]) for h in heads]
    state = [st_ref[h] for h in heads]
    v_new = [u[h] - _dot(w[h], state[h]) for h in heads]
    o = [_dot(q[h] * eg[h], state[h]) + _dot(attn[h], v_new[h]) for h in heads]
    kg_t = [(k[h] * jnp.exp(g_last[h] - g_col[h])).T for h in heads]
    for h in heads:
        st_ref[h] = state[h] * jnp.exp(g_last[h]) + _dot(kg_t[h], v_new[h])
    ng = ng_ref[...]
    for h in heads:
        o_ref[:, h * hd:(h + 1) * hd] = _rms(o[h], ng) * _silu(z_ref[:, h * hd:(h + 1) * hd])


GDN_KERNEL_CHUNK = 128


def _gdn_prompt(proj, cw, bias_row, alog_row, ng, bsz, seq):
    rows = GDN_KERNEL_CHUNK
    nc = seq // rows
    return pl.pallas_call(
        functools.partial(_gdn_prompt_kernel, c=rows),
        grid=(bsz, nc),
        in_specs=[pl.BlockSpec((rows, GDN_CONV_CH), lambda b, z: (b * nc + z, P_GQKV // GDN_CONV_CH)),
                  pl.BlockSpec((rows, BR_W), lambda b, z: (b * nc + z, P_GZ // BR_W)),
                  pl.BlockSpec((rows, SMALL_W), lambda b, z: (b * nc + z, P_SMALL // SMALL_W)),
                  _full((CONV_W, GDN_CONV_CH)), _full((1, SMALL_W)), _full((1, SMALL_W)), _full((1, GDN_HEAD))],
        out_specs=[pl.BlockSpec((rows, BR_W), lambda b, z: (b * nc + z, 0)),
                   pl.BlockSpec((None, GDN_HEADS, GDN_HEAD, GDN_HEAD), lambda b, z: (b, 0, 0, 0))],
        out_shape=[jax.ShapeDtypeStruct((bsz * seq, BR_W), F32),
                   jax.ShapeDtypeStruct((bsz, GDN_HEADS, GDN_HEAD, GDN_HEAD), F32)],
        scratch_shapes=[pltpu.VMEM((rows + 8, GDN_CONV_CH), F32)],
        compiler_params=_cparams("parallel", "arbitrary"),
        name="gdn_prompt")(proj, proj, proj, cw, bias_row, alog_row, ng)


def _mlstm_prompt_kernel(q_ref, k_ref, v_ref, og_ref, sm_ref, bias_ref, ng_ref, o_ref, c_ref, n_ref, m_ref, *, c):
    zi = pl.program_id(1)

    @pl.when(zi == 0)
    def _():
        c_ref[...] = jnp.zeros(c_ref.shape, F32)
        n_ref[...] = jnp.zeros(n_ref.shape, F32)
        m_ref[...] = jnp.zeros(m_ref.shape, F32)

    pre = sm_ref[:, :LANE] + bias_ref[:, :LANE]
    logf = -_softplus(-pre)
    bcum = _dot_f32(_tril_f32(c), logf)
    bcum_t = bcum.T
    pre_t = pre.T
    causal = _iota2((c, c), 0) >= _iota2((c, c), 1)
    ng = ng_ref[...]
    hd = ML_HEAD
    heads = range(ML_HEADS)
    sl = [slice(h * hd, (h + 1) * hd) for h in heads]
    q = [q_ref[:, sl[h]] for h in heads]
    k = [k_ref[:, sl[h]] * (hd ** -0.5) for h in heads]
    v = [v_ref[:, sl[h]] for h in heads]
    c_prev = [c_ref[h] for h in heads]
    n_prev = [n_ref[h:h + 1, :] for h in heads]
    m_prev = [m_ref[h:h + 1, 0:1] for h in heads]
    qk = [_dot_nt(q[h], k[h]) for h in heads]
    qc = [_dot(q[h], c_prev[h]) for h in heads]
    b_col = [bcum[:, S_MF + h:S_MF + h + 1] for h in heads]
    b_last = [b_col[h][c - 1:c, :] for h in heads]
    i_col = [pre[:, S_MI + h:S_MI + h + 1] for h in heads]
    dmat = [jnp.where(causal, b_col[h] - bcum_t[S_MF + h:S_MF + h + 1, :] + pre_t[S_MI + h:S_MI + h + 1, :], -jnp.inf)
            for h in heads]
    m_t = [jnp.maximum(b_col[h] + m_prev[h], jnp.max(dmat[h], axis=1, keepdims=True)) for h in heads]
    m_new = [m_t[h][c - 1:c, :] for h in heads]
    s = [qk[h] * jnp.exp(dmat[h] - m_t[h]) for h in heads]
    kw = [k[h] * jnp.exp(b_last[h] - b_col[h] + i_col[h] - m_new[h]) for h in heads]
    sv = [_dot(s[h], v[h]) for h in heads]
    kv = [_dot(kw[h].T, v[h]) for h in heads]
    for h in heads:
        w_prev = jnp.exp(b_col[h] + m_prev[h] - m_t[h])
        num = w_prev * qc[h] + sv[h]
        den = w_prev * jnp.sum(q[h] * n_prev[h], axis=1, keepdims=True) + jnp.sum(s[h], axis=1, keepdims=True)
        hid = num / jnp.maximum(jnp.abs(den), jnp.exp(-m_t[h]))
        w_c = jnp.exp(b_last[h] + m_prev[h] - m_new[h])
        c_ref[h] = c_prev[h] * w_c + kv[h]
        n_ref[h:h + 1, :] = n_prev[h] * w_c + jnp.sum(kw[h], axis=0, keepdims=True)
        m_ref[h:h + 1, :] = jnp.broadcast_to(m_new[h], (1, LANE))
        o_ref[:, sl[h]] = _rms(hid, ng) * _sigmoid(og_ref[:, sl[h]])


def _mlstm_prompt(proj, bias_row, ng, bsz, seq):
    c = ML_CHUNK
    nc = seq // c
    col = lambda i: pl.BlockSpec((c, BR_W), lambda b, z: (b * nc + z, P_MQ // BR_W + i))
    return pl.pallas_call(
        functools.partial(_mlstm_prompt_kernel, c=c),
        grid=(bsz, nc),
        in_specs=[col(0), col(1), col(2), col(3),
                  pl.BlockSpec((c, SMALL_W), lambda b, z: (b * nc + z, P_SMALL // SMALL_W)),
                  _full((1, SMALL_W)), _full((1, ML_HEAD))],
        out_specs=[pl.BlockSpec((c, BR_W), lambda b, z: (b * nc + z, 0)),
                   pl.BlockSpec((None, ML_HEADS, ML_HEAD, ML_HEAD), lambda b, z: (b, 0, 0, 0)),
                   pl.BlockSpec((None, ML_HEADS, ML_HEAD), lambda b, z: (b, 0, 0)),
                   pl.BlockSpec((None, 8, LANE), lambda b, z: (b, 0, 0))],
        out_shape=[jax.ShapeDtypeStruct((bsz * seq, BR_W), F32),
                   jax.ShapeDtypeStruct((bsz, ML_HEADS, ML_HEAD, ML_HEAD), F32),
                   jax.ShapeDtypeStruct((bsz, ML_HEADS, ML_HEAD), F32),
                   jax.ShapeDtypeStruct((bsz, 8, LANE), F32)],
        compiler_params=_cparams("parallel", "arbitrary"),
        name="mlstm_prompt")(proj, proj, proj, proj, proj, bias_row, ng)


STEP_ROWS = 8


def _conv_step(x, buf, cw, nb_ref, r):
    y = buf[0:1] * cw[0:1] + buf[1:2] * cw[1:2] + buf[2:3] * cw[2:3] + x * cw[3:4]
    nb_ref[r, 0:2, :] = buf[1:3]
    nb_ref[r, 2:3, :] = x
    return y


def _ssd_step_kernel(z_ref, xbc_ref, sm_ref, buf_ref, st_ref, cw_ref, cb_ref, bias_ref, alog_ref, dsk_ref, ng_ref,
                     o_ref, nb_ref, nst_ref):
    cw = cw_ref[...]
    gw = BR_W // SSM_GROUPS
    ng = ng_ref[...]
    dsk = dsk_ref[...]

    def body(r, carry):
        xbc = _silu(_conv_step(xbc_ref[r], buf_ref[r], cw, nb_ref, r) + cb_ref[...])
        xs = xbc[:, :BR_W]
        bm = xbc[:, BR_W:BR_W + SSM_GROUPS * SSM_STATE]
        cm = xbc[:, BR_W + SSM_GROUPS * SSM_STATE:]
        sm = sm_ref[r]
        dt = _softplus(sm[:, :LANE] + bias_ref[:, :LANE])
        d_a = jnp.exp(-jnp.exp(alog_ref[:, :LANE]) * dt)
        ys = []
        for h in range(SSM_HEADS):
            g = h // (SSM_HEADS // SSM_GROUPS)
            xs_h = xs[:, h * SSM_HEAD:(h + 1) * SSM_HEAD]
            x_col = _row_to_col(xs_h * dt[:, h:h + 1])
            bm_g = bm[:, g * SSM_STATE:(g + 1) * SSM_STATE]
            cm_g = cm[:, g * SSM_STATE:(g + 1) * SSM_STATE]
            h_new = st_ref[r, h] * d_a[:, h:h + 1] + x_col * bm_g
            nst_ref[r, h] = h_new
            y_col = jnp.sum(h_new * cm_g, axis=1, keepdims=True)
            ys.append(_col_to_row(y_col))
        y = (jnp.concatenate(ys, axis=1) + dsk * xs) * _silu(z_ref[r])
        o_ref[r] = jnp.concatenate([_rms(y[:, g * gw:(g + 1) * gw], ng[:, g * gw:(g + 1) * gw])
                                    for g in range(SSM_GROUPS)], axis=1)
        return carry

    lax.fori_loop(0, STEP_ROWS, body, 0)


def _ssd_step(proj3, buf, st, layer, cw, cb, bias_row, alog_row, dsk, ng):
    db = proj3.shape[0]
    rb = STEP_ROWS
    return pl.pallas_call(
        _ssd_step_kernel,
        grid=(db // rb,),
        in_specs=[pl.BlockSpec((rb, 1, BR_W), lambda i: (i, 0, P_SZ // BR_W)),
                  pl.BlockSpec((rb, 1, SSM_CONV_CH), lambda i: (i, 0, P_XBC // SSM_CONV_CH)),
                  pl.BlockSpec((rb, 1, SMALL_W), lambda i: (i, 0, P_SMALL // SMALL_W)),
                  pl.BlockSpec((None, rb, CONV_W - 1, SSM_CONV_CH), lambda i: (layer, i, 0, 0)),
                  pl.BlockSpec((None, rb, SSM_HEADS, SSM_HEAD, SSM_STATE), lambda i: (layer, i, 0, 0, 0)),
                  _full((CONV_W, SSM_CONV_CH)), _full((1, SSM_CONV_CH)), _full((1, SMALL_W)), _full((1, SMALL_W)),
                  _full((1, BR_W)), _full((1, BR_W))],
        out_specs=[pl.BlockSpec((rb, 1, BR_W), lambda i: (i, 0, 0)),
                   pl.BlockSpec((rb, CONV_W - 1, SSM_CONV_CH), lambda i: (i, 0, 0)),
                   pl.BlockSpec((rb, SSM_HEADS, SSM_HEAD, SSM_STATE), lambda i: (i, 0, 0, 0))],
        out_shape=[jax.ShapeDtypeStruct((db, 1, BR_W), F32),
                   jax.ShapeDtypeStruct((db, CONV_W - 1, SSM_CONV_CH), F32),
                   jax.ShapeDtypeStruct((db, SSM_HEADS, SSM_HEAD, SSM_STATE), F32)],
        compiler_params=_cparams("parallel"),
        name="ssd_step")(proj3, proj3, proj3, buf, st, cw, cb, bias_row, alog_row, dsk, ng)


def _gdn_step_kernel(qkv_ref, z_ref, sm_ref, buf_ref, st_ref, cw_ref, bias_ref, alog_ref, ng_ref,
                     o_ref, nb_ref, nst_ref):
    cw = cw_ref[...]
    ng = ng_ref[...]
    hd = GDN_HEAD

    def body(r, carry):
        qkv = _silu(_conv_step(qkv_ref[r], buf_ref[r], cw, nb_ref, r))
        sm = sm_ref[r][:, :LANE]
        g_all = -jnp.exp(alog_ref[:, :LANE]) * _softplus(sm + bias_ref[:, :LANE])
        eg_all = jnp.exp(g_all)
        beta_all = _sigmoid(sm)
        z = z_ref[r]
        outs = []
        for h in range(GDN_HEADS):
            q = qkv[:, h * hd:(h + 1) * hd]
            k = qkv[:, BR_W + h * hd:BR_W + (h + 1) * hd]
            v = qkv[:, 2 * BR_W + h * hd:2 * BR_W + (h + 1) * hd]
            q = q * lax.rsqrt(jnp.sum(q * q, axis=-1, keepdims=True) + EPS) * (hd ** -0.5)
            k = k * lax.rsqrt(jnp.sum(k * k, axis=-1, keepdims=True) + EPS)
            eg = eg_all[:, S_GA + h:S_GA + h + 1]
            beta = beta_all[:, S_GB + h:S_GB + h + 1]
            s_prev = st_ref[r, h]
            k_col = _row_to_col(k)
            q_col = _row_to_col(q)
            v_new = v * beta - jnp.sum((k_col * (beta * eg)) * s_prev, axis=0, keepdims=True)
            attn = jnp.sum(q * k, axis=1, keepdims=True)
            o = jnp.sum((q_col * eg) * s_prev, axis=0, keepdims=True) + attn * v_new
            nst_ref[r, h] = s_prev * eg + k_col * v_new
            outs.append(_rms(o, ng) * _silu(z[:, h * hd:(h + 1) * hd]))
        o_ref[r] = jnp.concatenate(outs, axis=1)
        return carry

    lax.fori_loop(0, STEP_ROWS, body, 0)


def _gdn_step(proj3, buf, st, layer, cw, bias_row, alog_row, ng):
    db = proj3.shape[0]
    rb = STEP_ROWS
    return pl.pallas_call(
        _gdn_step_kernel,
        grid=(db // rb,),
        in_specs=[pl.BlockSpec((rb, 1, GDN_CONV_CH), lambda i: (i, 0, P_GQKV // GDN_CONV_CH)),
                  pl.BlockSpec((rb, 1, BR_W), lambda i: (i, 0, P_GZ // BR_W)),
                  pl.BlockSpec((rb, 1, SMALL_W), lambda i: (i, 0, P_SMALL // SMALL_W)),
                  pl.BlockSpec((None, rb, CONV_W - 1, GDN_CONV_CH), lambda i: (layer, i, 0, 0)),
                  pl.BlockSpec((None, rb, GDN_HEADS, GDN_HEAD, GDN_HEAD), lambda i: (layer, i, 0, 0, 0)),
                  _full((CONV_W, GDN_CONV_CH)), _full((1, SMALL_W)), _full((1, SMALL_W)), _full((1, GDN_HEAD))],
        out_specs=[pl.BlockSpec((rb, 1, BR_W), lambda i: (i, 0, 0)),
                   pl.BlockSpec((rb, CONV_W - 1, GDN_CONV_CH), lambda i: (i, 0, 0)),
                   pl.BlockSpec((rb, GDN_HEADS, GDN_HEAD, GDN_HEAD), lambda i: (i, 0, 0, 0))],
        out_shape=[jax.ShapeDtypeStruct((db, 1, BR_W), F32),
                   jax.ShapeDtypeStruct((db, CONV_W - 1, GDN_CONV_CH), F32),
                   jax.ShapeDtypeStruct((db, GDN_HEADS, GDN_HEAD, GDN_HEAD), F32)],
        compiler_params=_cparams("parallel"),
        name="gdn_step")(proj3, proj3, proj3, buf, st, cw, bias_row, alog_row, ng)


def _mlstm_step_kernel(q_ref, k_ref, v_ref, og_ref, sm_ref, c_ref, n_ref, m_ref, bias_ref, ng_ref,
                       o_ref, nc_ref, nn_ref, nm_ref):
    ng = ng_ref[...]
    hd = ML_HEAD
    lane4 = _iota2((1, ML_HEADS), 1)

    def body(r, carry):
        pre = sm_ref[r][:, :LANE] + bias_ref[:, :LANE]
        logf_all = -_softplus(-pre)
        qr, kr, vr, ogr = q_ref[r], k_ref[r], v_ref[r], og_ref[r]
        m_all = m_ref[r]
        n_all = n_ref[r]
        outs = []
        m_out = jnp.zeros((1, ML_HEADS), F32)
        for h in range(ML_HEADS):
            sl = slice(h * hd, (h + 1) * hd)
            q = qr[:, sl]
            k = kr[:, sl] * (hd ** -0.5)
            v = vr[:, sl]
            i_pre = pre[:, S_MI + h:S_MI + h + 1]
            logf = logf_all[:, S_MF + h:S_MF + h + 1]
            m_prev = m_all[:, h:h + 1]
            m_t = jnp.maximum(logf + m_prev, i_pre)
            w_prev = jnp.exp(logf + m_prev - m_t)
            w_j = jnp.exp(i_pre - m_t)
            s = jnp.sum(q * k, axis=1, keepdims=True) * w_j
            c_prev = c_ref[r, h]
            n_prev = n_all[h:h + 1, :]
            q_col = _row_to_col(q)
            k_col = _row_to_col(k)
            num = w_prev * jnp.sum(q_col * c_prev, axis=0, keepdims=True) + s * v
            den = w_prev * jnp.sum(q * n_prev, axis=1, keepdims=True) + s
            hid = num / jnp.maximum(jnp.abs(den), jnp.exp(-m_t))
            nc_ref[r, h] = c_prev * w_prev + (k_col * w_j) * v
            nn_ref[r, h:h + 1, :] = n_prev * w_prev + k * w_j
            m_out = jnp.where(lane4 == h, m_t, m_out)
            outs.append(_rms(hid, ng) * _sigmoid(ogr[:, sl]))
        nm_ref[r] = m_out
        o_ref[r] = jnp.concatenate(outs, axis=1)
        return carry

    lax.fori_loop(0, STEP_ROWS, body, 0)


def _mlstm_step(proj3, c0, n0, m0, layer, bias_row, ng):
    db = proj3.shape[0]
    rb = STEP_ROWS
    col = lambda j: pl.BlockSpec((rb, 1, BR_W), lambda i: (i, 0, P_MQ // BR_W + j))
    return pl.pallas_call(
        _mlstm_step_kernel,
        grid=(db // rb,),
        in_specs=[col(0), col(1), col(2), col(3),
                  pl.BlockSpec((rb, 1, SMALL_W), lambda i: (i, 0, P_SMALL // SMALL_W)),
                  pl.BlockSpec((None, rb, ML_HEADS, ML_HEAD, ML_HEAD), lambda i: (layer, i, 0, 0, 0)),
                  pl.BlockSpec((None, rb, ML_HEADS, ML_HEAD), lambda i: (layer, i, 0, 0)),
                  pl.BlockSpec((None, rb, 1, ML_HEADS), lambda i: (layer, i, 0, 0)),
                  _full((1, SMALL_W)), _full((1, ML_HEAD))],
        out_specs=[pl.BlockSpec((rb, 1, BR_W), lambda i: (i, 0, 0)),
                   pl.BlockSpec((rb, ML_HEADS, ML_HEAD, ML_HEAD), lambda i: (i, 0, 0, 0)),
                   pl.BlockSpec((rb, ML_HEADS, ML_HEAD), lambda i: (i, 0, 0)),
                   pl.BlockSpec((rb, 1, ML_HEADS), lambda i: (i, 0, 0))],
        out_shape=[jax.ShapeDtypeStruct((db, 1, BR_W), F32),
                   jax.ShapeDtypeStruct((db, ML_HEADS, ML_HEAD, ML_HEAD), F32),
                   jax.ShapeDtypeStruct((db, ML_HEADS, ML_HEAD), F32),
                   jax.ShapeDtypeStruct((db, 1, ML_HEADS), F32)],
        compiler_params=_cparams("parallel"),
        name="mlstm_step")(proj3, proj3, proj3, proj3, proj3, c0, n0, m0, bias_row, ng)


def _merge_kernel(x_ref, a_ref, b_ref, c_ref, d_ref, g0, g1, g2, g3, wb_ref, wo_ref, o_ref):
    acc = None
    for n, (br, gate) in enumerate(((a_ref, g0), (b_ref, g1), (c_ref, g2), (d_ref, g3))):
        t = _sigmoid(gate[...]) * jnp.dot(br[...].astype(BF16), wb_ref[n], preferred_element_type=F32)
        acc = t if acc is None else acc + t
    o_ref[...] = x_ref[...] + jnp.dot(acc.astype(BF16), wo_ref[...], preferred_element_type=F32)


def _merge(x, branches, proj, wb, wo, tm):
    m = x.shape[0]
    br = pl.BlockSpec((tm, BR_W), lambda i: (i, 0))
    gate = lambda n: pl.BlockSpec((tm, D_MODEL), lambda i: (i, P_GATE // D_MODEL + n))
    return pl.pallas_call(
        _merge_kernel,
        grid=(m // tm,),
        in_specs=[pl.BlockSpec((tm, D_MODEL), lambda i: (i, 0)), br, br, br, br,
                  gate(0), gate(1), gate(2), gate(3),
                  _full((N_BRANCH, BR_W, D_MODEL)), _full((D_MODEL, D_MODEL))],
        out_specs=pl.BlockSpec((tm, D_MODEL), lambda i: (i, 0)),
        out_shape=jax.ShapeDtypeStruct((m, D_MODEL), F32),
        compiler_params=_cparams("parallel"),
        name="merge")(x, *branches, proj, proj, proj, proj, wb, wo)


def _cross_prompt_kernel(x_ref, g_ref, mk_ref, mv_ref, wq_ref, wo_ref, o_ref):
    x = x_ref[...]
    h = _rms(x, g_ref[...]).astype(BF16)
    q = jnp.dot(h, wq_ref[...], preferred_element_type=F32) * (X_HEAD ** -0.5)
    outs = []
    for hd in range(X_HEADS):
        sl = slice(hd * X_HEAD, (hd + 1) * X_HEAD)
        s = _dot_nt(q[:, sl], mk_ref[:, sl])
        p = jnp.exp(s - jnp.max(s, axis=1, keepdims=True))
        p = p / jnp.sum(p, axis=1, keepdims=True)
        outs.append(_dot(p, mv_ref[:, sl]))
    o = jnp.concatenate(outs, axis=1).astype(BF16)
    o_ref[...] = x + jnp.dot(o, wo_ref[...], preferred_element_type=F32)


def _cross_prompt(x, g, mkv, wq, wo, bsz, seq, tq=512):
    nq = seq // tq
    d = D_MODEL
    return pl.pallas_call(
        _cross_prompt_kernel,
        grid=(bsz, nq),
        in_specs=[pl.BlockSpec((tq, d), lambda b, i: (b * nq + i, 0)), _full((1, d)),
                  pl.BlockSpec((N_MEM, d), lambda b, i: (b, 0)),
                  pl.BlockSpec((N_MEM, d), lambda b, i: (b, 1)),
                  _full((d, d)), _full((d, d))],
        out_specs=pl.BlockSpec((tq, d), lambda b, i: (b * nq + i, 0)),
        out_shape=jax.ShapeDtypeStruct((bsz * seq, d), F32),
        compiler_params=_cparams("parallel", "arbitrary"),
        name="cross_prompt")(x, g.reshape(1, d), mkv, mkv, wq, wo)


CROSS_ROWS = 4


def _cross_decode_kernel(q_ref, mk_ref, mv_ref, o_ref):
    halves = X_HEAD // LANE
    rows_per_tok = halves * X_HEADS

    def head_slab(ref, r, h):
        return jnp.concatenate([ref[r, pl.ds(t * X_HEADS + h, N_MEM, stride=rows_per_tok), :].astype(BF16)
                                for t in range(halves)], axis=1)

    def body(r, carry):
        q = q_ref[r] * (X_HEAD ** -0.5)
        outs = []
        for h in range(X_HEADS):
            q_h = jnp.broadcast_to(q[:, h * X_HEAD:(h + 1) * X_HEAD], (8, X_HEAD)).astype(BF16)
            s = lax.dot_general(q_h, head_slab(mk_ref, r, h), (((1,), (1,)), ((), ())),
                                preferred_element_type=F32)
            e = jnp.exp(s - jnp.max(s, axis=1, keepdims=True))
            p = (e / jnp.sum(e, axis=1, keepdims=True)).astype(BF16)
            outs.append(jnp.dot(p, head_slab(mv_ref, r, h), preferred_element_type=F32)[0:1])
        o_ref[r] = jnp.concatenate(outs, axis=1)
        return carry

    lax.fori_loop(0, CROSS_ROWS, body, 0, unroll=True)


def _mem_rows(mem):
    depth, db = mem.shape[:2]
    halves = X_HEAD // LANE
    m = mem.reshape(depth, db, N_MEM, X_HEADS, halves, LANE).transpose(0, 1, 2, 4, 3, 5)
    return m.reshape(depth, db, N_MEM * halves * X_HEADS, LANE)


def _cross_decode(q3, mem_k, mem_v, layer):
    db = q3.shape[0]
    rb = CROSS_ROWS
    d = D_MODEL
    mem = pl.BlockSpec((None, rb, mem_k.shape[2], LANE), lambda i: (layer, i, 0, 0))
    return pl.pallas_call(
        _cross_decode_kernel,
        grid=(db // rb,),
        in_specs=[pl.BlockSpec((rb, 1, d), lambda i: (i, 0, 0)), mem, mem],
        out_specs=pl.BlockSpec((rb, 1, d), lambda i: (i, 0, 0)),
        out_shape=jax.ShapeDtypeStruct((db, 1, d), F32),
        compiler_params=_cparams("parallel"),
        name="cross_decode")(q3, mem_k, mem_v)


def _swiglu_kernel(x_ref, g_ref, wg_ref, wu_ref, wd_ref, o_ref, h_ref, acc_ref):
    j = pl.program_id(1)

    @pl.when(j == 0)
    def _():
        h_ref[...] = _rms(x_ref[...], g_ref[...]).astype(BF16)
        acc_ref[...] = jnp.zeros(acc_ref.shape, F32)

    h = h_ref[...]
    gate = jnp.dot(h, wg_ref[...], preferred_element_type=F32)
    up = jnp.dot(h, wu_ref[...], preferred_element_type=F32)
    acc_ref[...] += jnp.dot((_silu(gate) * up).astype(BF16), wd_ref[...], preferred_element_type=F32)

    @pl.when(j == pl.num_programs(1) - 1)
    def _():
        o_ref[...] = x_ref[...] + acc_ref[...]


def _swiglu(x, g, wgu, wd, tm, tf=D_FF // 2):
    m, d = x.shape
    nf = D_FF // tf
    return pl.pallas_call(
        _swiglu_kernel,
        grid=(m // tm, nf),
        in_specs=[pl.BlockSpec((tm, d), lambda i, j: (i, 0)), _full((1, d)),
                  pl.BlockSpec((d, tf), lambda i, j: (0, j)),
                  pl.BlockSpec((d, tf), lambda i, j: (0, nf + j)),
                  pl.BlockSpec((tf, d), lambda i, j: (j, 0))],
        out_specs=pl.BlockSpec((tm, d), lambda i, j: (i, 0)),
        out_shape=jax.ShapeDtypeStruct((m, d), F32),
        scratch_shapes=[pltpu.VMEM((tm, d), BF16), pltpu.VMEM((tm, d), F32)],
        compiler_params=_cparams("parallel", "arbitrary"),
        name="swiglu")(x, g.reshape(1, d), wgu, wgu, wd)


def _pack_w_in(w):
    offs = np.cumsum((0,) + IN_SPLITS)
    seg = lambda i: w[:, offs[i]:offs[i + 1]]
    small = jnp.concatenate([seg(5), seg(8), seg(9), seg(14), seg(15),
                             jnp.zeros((w.shape[0], SMALL_W - 24), w.dtype)], axis=1)
    order = (0, 1, 2, 3, 4, 6, 7, 10, 11, 12, 13, 16)
    return jnp.concatenate([seg(i) for i in order] + [small], axis=1).astype(BF16)


def _small_row(parts):
    row = jnp.zeros((SMALL_W,), F32)
    for off, val in parts:
        row = lax.dynamic_update_slice(row, val.astype(F32), (off,))
    return row.reshape(1, SMALL_W)


def _layer_params(l, p):
    lp = dict(
        w_in=_pack_w_in(p["w_in"][l]),
        g_mix=p["g_mix"][l],
        lam=tuple(p[n][l].reshape(1, DA_HEAD) for n in ("da_lq1", "da_lk1", "da_lq2", "da_lk2")),
        lam_init=0.8 - 0.6 * math.exp(-0.3 * l),
        sub_g=p["da_sub_g"][l],
        bias_row=_small_row(((S_DT, p["ssm_dt_bias"][l]), (S_GA, p["gdn_dt_bias"][l]),
                             (S_MI, p["ml_i_bias"][l]), (S_MF, p["ml_f_bias"][l]))),
        alog_row=_small_row(((S_DT, p["ssm_a_log"][l]), (S_GA, p["gdn_a_log"][l]))),
        ssm_cw=p["ssm_conv_w"][l], ssm_cb=p["ssm_conv_b"][l].reshape(1, SSM_CONV_CH),
        ssm_dsk=jnp.repeat(p["ssm_d"][l], SSM_HEAD).reshape(1, BR_W),
        ssm_ng=p["ssm_norm_g"][l].reshape(1, BR_W),
        gdn_cw=p["gdn_conv_w"][l], gdn_ng=p["gdn_norm_g"][l].reshape(1, GDN_HEAD),
        ml_ng=p["ml_norm_g"][l].reshape(1, ML_HEAD),
        w_branch=p["w_branch"][l].astype(BF16), w_out=p["w_out"][l].astype(BF16),
        g_cross=p["g_cross"][l], w_cq=p["w_cq"][l].astype(BF16), w_co=p["w_co"][l].astype(BF16),
        g_ffn=p["g_ffn"][l], w_gu=p["w_gu"][l].astype(BF16), w_down=p["w_down"][l].astype(BF16),
    )
    return lp


def _prompt_layer(x, lp, mkv, bsz, seq):
    proj = _norm_matmul(x, lp["g_mix"], lp["w_in"], tm=1024 if (bsz * seq) % 1024 == 0 else bsz * seq,
                        tn=1152, name="in_proj")
    o_da = _da_prompt(proj, lp["lam"], lp["sub_g"], bsz, seq, lp["lam_init"])
    o_ssm, ssm = _ssd_prompt(proj, lp["ssm_cw"], lp["ssm_cb"], lp["bias_row"], lp["alog_row"], lp["ssm_dsk"],
                             lp["ssm_ng"], bsz, seq)
    o_gdn, gdn = _gdn_prompt(proj, lp["gdn_cw"], lp["bias_row"], lp["alog_row"], lp["gdn_ng"], bsz, seq)
    o_ml, ml_c, ml_n, ml_m = _mlstm_prompt(proj, lp["bias_row"], lp["ml_ng"], bsz, seq)
    x = _merge(x, (o_da, o_ssm, o_gdn, o_ml), proj, lp["w_branch"], lp["w_out"], tm=256)
    x = _cross_prompt(x, lp["g_cross"], mkv, lp["w_cq"], lp["w_co"], bsz, seq, tq=min(512, seq))
    x = _swiglu(x, lp["g_ffn"], lp["w_gu"], lp["w_down"], tm=512 if (bsz * seq) % 512 == 0 else 256)
    p3 = proj.reshape(bsz, seq, PACK_W)
    new = dict(
        k=p3[:, :, P_K:P_K + BR_W].reshape(bsz, seq, DA_HEADS, 2, DA_HEAD),
        v=p3[:, :, P_V:P_V + BR_W].reshape(bsz, seq, DA_HEADS, 2 * DA_HEAD),
        ssm_conv=p3[:, seq - (CONV_W - 1):, P_XBC:P_XBC + SSM_CONV_CH], ssm=ssm,
        gdn_conv=p3[:, seq - (CONV_W - 1):, P_GQKV:P_GQKV + GDN_CONV_CH], gdn=gdn,
        ml_c=ml_c, ml_n=ml_n, ml_m=ml_m[:, :ML_HEADS, 0])
    return x, new


def _sample_layer(x, lp, l, caches, states):
    db = x.shape[0]
    cache_k, cache_v, page_table, mem_k, mem_v = caches
    proj = _norm_matmul(x, lp["g_mix"], lp["w_in"], tm=db, tn=1152, name="in_proj_s")
    proj3 = proj.reshape(db, 1, PACK_W)
    o_da = _da_decode(proj3, lp["lam"], lp["sub_g"], cache_k, cache_v, page_table, l, lp["lam_init"])
    o_ssm, ssm_conv, ssm = _ssd_step(proj3, states["ssm_conv"], states["ssm"], l, lp["ssm_cw"], lp["ssm_cb"],
                                     lp["bias_row"], lp["alog_row"], lp["ssm_dsk"], lp["ssm_ng"])
    o_gdn, gdn_conv, gdn = _gdn_step(proj3, states["gdn_conv"], states["gdn"], l, lp["gdn_cw"],
                                     lp["bias_row"], lp["alog_row"], lp["gdn_ng"])
    o_ml, ml_c, ml_n, ml_m = _mlstm_step(proj3, states["ml_c"], states["ml_n"], states["ml_m"], l,
                                         lp["bias_row"], lp["ml_ng"])
    x = _merge(x, (o_da, o_ssm.reshape(db, BR_W), o_gdn.reshape(db, BR_W), o_ml.reshape(db, BR_W)),
               proj, lp["w_branch"], lp["w_out"], tm=db)
    q = _norm_matmul(x, lp["g_cross"], lp["w_cq"], tm=db, tn=D_MODEL, name="cross_q_s")
    att = _cross_decode(q.reshape(db, 1, D_MODEL), mem_k, mem_v, l)
    x = _matmul_residual(x, att.reshape(db, D_MODEL), lp["w_co"], tm=db, name="cross_o_s")
    x = _swiglu(x, lp["g_ffn"], lp["w_gu"], lp["w_down"], tm=db)
    new = dict(
        k=proj[:, P_K:P_K + BR_W].reshape(db, 1, DA_HEADS, 2, DA_HEAD),
        v=proj[:, P_V:P_V + BR_W].reshape(db, 1, DA_HEADS, 2 * DA_HEAD),
        ssm_conv=ssm_conv, ssm=ssm, gdn_conv=gdn_conv, gdn=gdn,
        ml_c=ml_c, ml_n=ml_n, ml_m=ml_m.reshape(db, ML_HEADS))
    return x, new


_STATE_ORDER = ("ssm_conv", "ssm", "gdn_conv", "gdn", "ml_c", "ml_n", "ml_m")


def kernel(x_prompt, x_sample, cache_k, cache_v, cache_mem_k, cache_mem_v, state_ssm_conv, state_ssm, state_gdn_conv, state_gdn, state_mlstm_c, state_mlstm_n, state_mlstm_m, page_table, mem_prompt, g_mix, w_in, da_lq1, da_lk1, da_lq2, da_lk2, da_sub_g, ssm_conv_w, ssm_conv_b, ssm_dt_bias, ssm_a_log, ssm_d, ssm_norm_g, gdn_conv_w, gdn_dt_bias, gdn_a_log, gdn_norm_g, ml_i_bias, ml_f_bias, ml_norm_g, w_branch, w_out, g_cross, g_mem, w_cq, w_ckv, w_co, g_ffn, w_gu, w_down, g_final):
    p = dict(g_mix=g_mix, w_in=w_in, da_lq1=da_lq1, da_lk1=da_lk1, da_lq2=da_lq2, da_lk2=da_lk2,
             da_sub_g=da_sub_g, ssm_conv_w=ssm_conv_w, ssm_conv_b=ssm_conv_b, ssm_dt_bias=ssm_dt_bias,
             ssm_a_log=ssm_a_log, ssm_d=ssm_d, ssm_norm_g=ssm_norm_g, gdn_conv_w=gdn_conv_w,
             gdn_dt_bias=gdn_dt_bias, gdn_a_log=gdn_a_log, gdn_norm_g=gdn_norm_g,
             ml_i_bias=ml_i_bias, ml_f_bias=ml_f_bias, ml_norm_g=ml_norm_g,
             w_branch=w_branch, w_out=w_out, g_cross=g_cross, w_cq=w_cq, w_co=w_co,
             g_ffn=g_ffn, w_gu=w_gu, w_down=w_down)
    depth = w_in.shape[0]
    bsz, seq, d = x_prompt.shape
    db = x_sample.shape[0]
    n_mem = mem_prompt.shape[1]
    lps = [_layer_params(l, p) for l in range(depth)]

    mem2 = mem_prompt.reshape(bsz * n_mem, d)
    xp = x_prompt.reshape(bsz * seq, d)
    p_new = {n: [] for n in ("k", "v", "mem_k", "mem_v") + _STATE_ORDER}
    for l in range(depth):
        mkv = _norm_matmul(mem2, g_mem[l], w_ckv[l].astype(BF16), tm=min(1024, bsz * n_mem), tn=1024, name="mem_kv")
        xp, new = _prompt_layer(xp, lps[l], mkv, bsz, seq)
        mkv5 = mkv.reshape(bsz, n_mem, 2, X_HEADS, X_HEAD)
        new["mem_k"] = mkv5[:, :, 0]
        new["mem_v"] = mkv5[:, :, 1]
        for n in p_new:
            p_new[n].append(new[n])
    y_prompt = _final_norm(xp, g_final, tm=512 if (bsz * seq) % 512 == 0 else bsz * seq, name="final_norm").reshape(bsz, seq, d)

    n_pool = cache_k.shape[1]
    caches = (cache_k.transpose(0, 1, 3, 4, 5, 2).reshape(depth, n_pool, BR_W, PAGE_SIZE),
              cache_v.reshape(depth, n_pool, PAGE_SIZE * DA_HEADS, 2 * DA_HEAD),
              page_table, _mem_rows(cache_mem_k), _mem_rows(cache_mem_v))
    states = dict(ssm_conv=state_ssm_conv, ssm=state_ssm, gdn_conv=state_gdn_conv, gdn=state_gdn,
                  ml_c=state_mlstm_c, ml_n=state_mlstm_n, ml_m=state_mlstm_m.reshape(depth, db, 1, ML_HEADS))
    xs = x_sample.reshape(db, d)
    s_new = {n: [] for n in ("k", "v") + _STATE_ORDER}
    for l in range(depth):
        xs, new = _sample_layer(xs, lps[l], l, caches, states)
        for n in s_new:
            s_new[n].append(new[n])
    y_sample = _final_norm(xs, g_final, tm=db, name="final_norm_s").reshape(db, 1, d)

    stk = lambda dct, n: jnp.stack(dct[n])
    return (y_prompt, y_sample,
            stk(p_new, "k"), stk(p_new, "v"), stk(p_new, "mem_k"), stk(p_new, "mem_v"),
            *(stk(p_new, n) for n in _STATE_ORDER),
            stk(s_new, "k"), stk(s_new, "v"), *(stk(s_new, n) for n in _STATE_ORDER))
```

```python
import functools
import math

import numpy as np
import jax
import jax.numpy as jnp
from jax import lax
from jax.experimental import pallas as pl
from jax.experimental.pallas import tpu as pltpu

F32 = jnp.float32
BF16 = jnp.bfloat16

D_MODEL = 1024
DEPTH = 4
PAGE_SIZE = 128
EPS = 1e-6
N_MEM = 256
CONV_W = 4
N_BRANCH = 4
BR_W = D_MODEL // 2
DA_HEADS = 4
DA_HEAD = BR_W // (2 * DA_HEADS)
SSM_HEAD = 64
SSM_HEADS = BR_W // SSM_HEAD
SSM_GROUPS = 2
SSM_STATE = 128
SSM_CONV_CH = BR_W + 2 * SSM_GROUPS * SSM_STATE
SSM_CHUNK = 128
GDN_HEADS = 4
GDN_HEAD = BR_W // GDN_HEADS
GDN_CONV_CH = 3 * BR_W
GDN_CHUNK = 64
ML_HEADS = 4
ML_HEAD = BR_W // ML_HEADS
ML_CHUNK = 128
X_HEADS = 4
X_HEAD = D_MODEL // X_HEADS
D_FF = -(-8 * D_MODEL // (3 * 256)) * 256

IN_SPLITS = (BR_W, BR_W, BR_W, BR_W, SSM_CONV_CH, SSM_HEADS, GDN_CONV_CH, BR_W, GDN_HEADS, GDN_HEADS,
             BR_W, BR_W, BR_W, BR_W, ML_HEADS, ML_HEADS, N_BRANCH * D_MODEL)

P_Q, P_K, P_V = 0, BR_W, 2 * BR_W
P_SZ = 3 * BR_W
P_XBC = P_SZ + BR_W
P_GQKV = P_XBC + SSM_CONV_CH
P_GZ = P_GQKV + GDN_CONV_CH
P_MQ = P_GZ + BR_W
P_GATE = P_MQ + 4 * BR_W
P_SMALL = P_GATE + N_BRANCH * D_MODEL
SMALL_W = 256
PACK_W = P_SMALL + SMALL_W
S_DT, S_GA, S_GB, S_MI, S_MF = 0, 8, 12, 16, 20

LANE = 128
VMEM_LIMIT = 56 * 1024 * 1024


def _cparams(*sem):
    return pltpu.CompilerParams(dimension_semantics=sem, vmem_limit_bytes=VMEM_LIMIT)


def _dot(a, b):
    return jnp.dot(a.astype(BF16), b.astype(BF16), preferred_element_type=F32)


def _dot_nt(a, b):
    return lax.dot_general(a.astype(BF16), b.astype(BF16), (((1,), (1,)), ((), ())), preferred_element_type=F32)


def _dot_f32(a, b):
    return jnp.dot(a, b, precision=lax.Precision.HIGHEST, preferred_element_type=F32)


def _dot_split(a, b_bf16):
    hi = a.astype(BF16)
    lo = (a - hi.astype(F32)).astype(BF16)
    return (jnp.dot(hi, b_bf16, preferred_element_type=F32) + jnp.dot(lo, b_bf16, preferred_element_type=F32))


def _sigmoid(x):
    return 1.0 / (1.0 + jnp.exp(-x))


def _silu(x):
    return x * _sigmoid(x)


def _softplus(x):
    return jnp.maximum(x, 0.0) + jnp.log(1.0 + jnp.exp(-jnp.abs(x)))


def _rms(x, g):
    return x * lax.rsqrt(jnp.mean(x * x, axis=-1, keepdims=True) + EPS) * g


def _iota2(shape, dim):
    return lax.broadcasted_iota(jnp.int32, shape, dim)


def _row_to_col(x):
    n = x.shape[1]
    eye = _iota2((n, n), 0) == _iota2((n, n), 1)
    return jnp.sum(jnp.where(eye, jnp.broadcast_to(x, (n, n)), 0.0), axis=1, keepdims=True)


def _col_to_row(x):
    n = x.shape[0]
    eye = _iota2((n, n), 0) == _iota2((n, n), 1)
    return jnp.sum(jnp.where(eye, jnp.broadcast_to(x, (n, n)), 0.0), axis=0, keepdims=True)


def _tril_f32(c):
    return (_iota2((c, c), 0) >= _iota2((c, c), 1)).astype(F32)


def _head_expander(n_heads, width):
    rows = _iota2((LANE, n_heads * width), 0)
    cols = _iota2((LANE, n_heads * width), 1)
    return (rows * width <= cols) & (cols < (rows + 1) * width)


def _lam(lq1, lk1, lq2, lk2, lam_init):
    return (jnp.exp(jnp.sum(lq1[...] * lk1[...], axis=1, keepdims=True))
            - jnp.exp(jnp.sum(lq2[...] * lk2[...], axis=1, keepdims=True)) + lam_init)


def _norm_matmul_kernel(x_ref, g_ref, w_ref, o_ref, h_ref):
    @pl.when(pl.program_id(1) == 0)
    def _():
        h_ref[...] = _rms(x_ref[...], g_ref[...]).astype(BF16)

    o_ref[...] = jnp.dot(h_ref[...], w_ref[...], preferred_element_type=F32)


def _norm_matmul(x, g, w, tm, tn, name):
    m, k = x.shape
    n = w.shape[1]
    return pl.pallas_call(
        _norm_matmul_kernel,
        grid=(m // tm, n // tn),
        in_specs=[pl.BlockSpec((tm, k), lambda i, j: (i, 0)),
                  pl.BlockSpec((1, k), lambda i, j: (0, 0)),
                  pl.BlockSpec((k, tn), lambda i, j: (0, j))],
        out_specs=pl.BlockSpec((tm, tn), lambda i, j: (i, j)),
        out_shape=jax.ShapeDtypeStruct((m, n), F32),
        scratch_shapes=[pltpu.VMEM((tm, k), BF16)],
        compiler_params=_cparams("parallel", "arbitrary"),
        name=name)(x, g.reshape(1, k), w)


def _matmul_res_kernel(x_ref, a_ref, w_ref, o_ref):
    o_ref[...] = x_ref[...] + jnp.dot(a_ref[...].astype(BF16), w_ref[...], preferred_element_type=F32)


def _matmul_residual(x, a, w, tm, name):
    m, n = x.shape
    k = a.shape[1]
    return pl.pallas_call(
        _matmul_res_kernel,
        grid=(m // tm,),
        in_specs=[pl.BlockSpec((tm, n), lambda i: (i, 0)),
                  pl.BlockSpec((tm, k), lambda i: (i, 0)),
                  pl.BlockSpec((k, n), lambda i: (0, 0))],
        out_specs=pl.BlockSpec((tm, n), lambda i: (i, 0)),
        out_shape=jax.ShapeDtypeStruct((m, n), F32),
        compiler_params=_cparams("parallel"),
        name=name)(x, a, w)


def _final_norm_kernel(x_ref, g_ref, o_ref):
    o_ref[...] = _rms(x_ref[...], g_ref[...])


def _final_norm(x, g, tm, name):
    m, n = x.shape
    return pl.pallas_call(
        _final_norm_kernel,
        grid=(m // tm,),
        in_specs=[pl.BlockSpec((tm, n), lambda i: (i, 0)), pl.BlockSpec((1, n), lambda i: (0, 0))],
        out_specs=pl.BlockSpec((tm, n), lambda i: (i, 0)),
        out_shape=jax.ShapeDtypeStruct((m, n), F32),
        compiler_params=_cparams("parallel"),
        name=name)(x, g.reshape(1, n))


def _da_prompt_kernel(lq1, lk1, lq2, lk2, subg_ref, q_ref, k_ref, v_ref, o_ref, kb_ref, vt_ref, acc_ref,
                      *, lam_init, tq, cw):
    hw = 2 * DA_HEAD
    nq = vt_ref.shape[0]
    kb_ref[...] = k_ref[...].astype(BF16)
    for t in range(nq):
        vt_ref[t] = v_ref[t * tq:(t + 1) * tq, :].T.astype(BF16)
    lam = _lam(lq1, lk1, lq2, lk2, lam_init)
    subg = subg_ref[...]
    sub = _iota2((hw, tq), 0)
    nch = 2 * tq // cw
    chunks = [slice(c * cw, (c + 1) * cw) for c in range(nch)]

    def scores(j, q2t):
        kj = kb_ref[pl.ds(pl.multiple_of(j * tq, tq), tq), :]
        return tuple(jnp.dot(kj, q2t[:, cols], preferred_element_type=F32) for cols in chunks)

    def update(j, sts, stats, masked):
        new_stats, scaled = [], []
        for c, st in enumerate(sts):
            m, l = stats[c]
            if masked:
                qpos = _iota2((tq, cw), 1) + (c * cw) % tq
                st = jnp.where(_iota2((tq, cw), 0) <= qpos, st, -jnp.inf)
            m_new = jnp.maximum(m, jnp.max(st, axis=0, keepdims=True))
            alpha = jnp.exp(m - m_new)
            p = jnp.exp(st - m_new)
            new_stats.append((m_new, alpha * l + jnp.sum(p, axis=0, keepdims=True)))
            scaled.append((alpha, p.astype(BF16)))
        vtj = vt_ref[j]
        for cols, (alpha, p) in zip(chunks, scaled):
            acc_ref[:, cols] = alpha * acc_ref[:, cols] + jnp.dot(vtj, p, preferred_element_type=F32)
        return tuple(new_stats)

    def q_tile(qi, carry):
        rows = pl.ds(pl.multiple_of(qi * tq, tq), tq)
        qt = (q_ref[rows, :] * (DA_HEAD ** -0.5)).T
        q2t = jnp.concatenate([jnp.where(sub < DA_HEAD, qt, 0.0), jnp.where(sub >= DA_HEAD, qt, 0.0)],
                              axis=1).astype(BF16)
        acc_ref[...] = jnp.zeros(acc_ref.shape, F32)

        def body(j, state):
            sts, stats = state
            nxt = scores(j + 1, q2t)
            return nxt, update(j, sts, stats, False)

        init = tuple((jnp.full((1, cw), -jnp.inf, F32), jnp.zeros((1, cw), F32)) for _ in range(nch))
        sts, stats = lax.fori_loop(0, qi, body, (scores(0, q2t), init))
        stats = update(qi, sts, stats, True)
        l = jnp.concatenate([ml[1] for ml in stats], axis=1)
        ot = acc_ref[...] / l
        odt = ot[:, :tq] - lam * ot[:, tq:]
        yt = odt * lax.rsqrt(jnp.mean(odt * odt, axis=0, keepdims=True) + EPS) * subg * (1.0 - lam_init)
        o_ref[rows, :] = yt.T
        return carry

    lax.fori_loop(0, nq, q_tile, 0)


def _da_prompt(proj, lam_params, sub_g, bsz, seq, lam_init, tq=256, cw=128):
    nq = seq // tq
    hw = 2 * DA_HEAD
    small = pl.BlockSpec((1, DA_HEAD), lambda b, h: (0, 0))
    col = lambda off: pl.BlockSpec((seq, hw), lambda b, h: (b, off // hw + h))
    return pl.pallas_call(
        functools.partial(_da_prompt_kernel, lam_init=lam_init, tq=tq, cw=cw),
        grid=(bsz, DA_HEADS),
        in_specs=[small, small, small, small, pl.BlockSpec((hw, 1), lambda b, h: (0, 0)),
                  col(P_Q), col(P_K), col(P_V)],
        out_specs=pl.BlockSpec((seq, hw), lambda b, h: (b, h)),
        out_shape=jax.ShapeDtypeStruct((bsz * seq, BR_W), F32),
        scratch_shapes=[pltpu.VMEM((seq, hw), BF16), pltpu.VMEM((nq, hw, tq), BF16), pltpu.VMEM((hw, 2 * tq), F32)],
        compiler_params=_cparams("parallel", "parallel"),
        name="da_prompt")(*lam_params, sub_g.reshape(hw, 1), proj, proj, proj)


def _da_decode_kernel(pt_ref, lq1, lk1, lq2, lk2, subg_ref, q_ref, kn_ref, vn_ref, *rest, lam_init, n_pages):
    del pt_ref
    kt_refs = rest[:n_pages]
    v_refs = rest[n_pages:2 * n_pages]
    o_ref = rest[2 * n_pages]
    nh = DA_HEADS
    hw = 2 * DA_HEAD
    lam = _lam(lq1, lk1, lq2, lk2, lam_init)
    q = q_ref[...] * (DA_HEAD ** -0.5)
    r = _iota2((2 * nh, BR_W), 0)
    seg = _iota2((2 * nh, BR_W), 1) >> 6
    q_bd = jnp.where(((seg & 1) == (r >> 2)) & ((seg >> 1) == (r & 3)), jnp.broadcast_to(q, (2 * nh, BR_W)), 0.0)
    s_new = jnp.sum(q_bd * kn_ref[...], axis=1, keepdims=True)
    qb = q_bd.astype(BF16)
    s = jnp.concatenate([jnp.dot(qb, kt_refs[j][...].astype(BF16), preferred_element_type=F32)
                         for j in range(n_pages)], axis=1)
    m = jnp.maximum(jnp.max(s, axis=1, keepdims=True), s_new)
    e = jnp.exp(s - m)
    e_new = jnp.exp(s_new - m)
    l = jnp.sum(e, axis=1, keepdims=True) + e_new
    coef = jnp.where(_iota2((2 * nh, 1), 0) < nh, 1.0, -lam) / l
    w = e * coef
    w_new = e_new * coef
    p = (w + pltpu.roll(w, nh, axis=0)).astype(BF16)
    p_new = w_new + pltpu.roll(w_new, nh, axis=0)
    g = subg_ref[...]
    vn = vn_ref[...]
    outs = []
    for h in range(nh):
        v_h = jnp.concatenate([v_refs[j][pl.ds(h, PAGE_SIZE, stride=nh), :].astype(BF16) for j in range(n_pages)],
                              axis=0)
        o_h = (jnp.dot(p, v_h, preferred_element_type=F32)[h:h + 1, :]
               + p_new[h:h + 1, :] * vn[:, h * hw:(h + 1) * hw])
        outs.append(_rms(o_h, g) * (1.0 - lam_init))
    o_ref[...] = jnp.concatenate(outs, axis=1)


def _da_decode(proj3, lam_params, sub_g, cache_kt, cache_v, page_table, layer, lam_init):
    db = proj3.shape[0]
    n_pages = page_table.shape[1]
    hw = 2 * DA_HEAD
    small = pl.BlockSpec((1, DA_HEAD), lambda b, pt: (0, 0))
    row = lambda off: pl.BlockSpec((None, 1, BR_W), lambda b, pt: (b, 0, off // BR_W))

    def page(j):
        return pl.BlockSpec((None, None, BR_W, PAGE_SIZE), lambda b, pt: (layer, pt[b * n_pages + j], 0, 0))

    grid_spec = pltpu.PrefetchScalarGridSpec(
        num_scalar_prefetch=1,
        grid=(db,),
        in_specs=[small, small, small, small, pl.BlockSpec((1, hw), lambda b, pt: (0, 0)),
                  row(P_Q), row(P_K), row(P_V)]
                 + [page(j) for j in range(n_pages)] + [page(j) for j in range(n_pages)],
        out_specs=pl.BlockSpec((None, 1, BR_W), lambda b, pt: (b, 0, 0)))
    out = pl.pallas_call(
        functools.partial(_da_decode_kernel, lam_init=lam_init, n_pages=n_pages),
        grid_spec=grid_spec,
        out_shape=jax.ShapeDtypeStruct((db, 1, BR_W), F32),
        compiler_params=_cparams("parallel"),
        name="da_decode")(page_table.reshape(-1), *lam_params, sub_g.reshape(1, hw), proj3, proj3, proj3,
                          *([cache_kt] * n_pages), *([cache_v] * n_pages))
    return out.reshape(db, BR_W)


def _conv_window(win_ref, x_ref, cw, c, zi):
    @pl.when(zi == 0)
    def _():
        win_ref[0:8, :] = jnp.zeros((8, win_ref.shape[1]), F32)

    @pl.when(zi > 0)
    def _():
        win_ref[0:8, :] = win_ref[c:c + 8, :]

    win_ref[8:8 + c, :] = x_ref[...]
    y = win_ref[5:5 + c, :] * cw[0:1, :]
    for j in range(1, CONV_W):
        y = y + win_ref[5 + j:5 + j + c, :] * cw[j:j + 1, :]
    return y


def _ssd_prompt_kernel(z_ref, xbc_ref, sm_ref, cw_ref, cb_ref, bias_ref, alog_ref, dsk_ref, ng_ref,
                       o_ref, st_ref, win_ref, yd_ref, yo_ref, *, c):
    zi = pl.program_id(1)

    @pl.when(zi == 0)
    def _():
        st_ref[...] = jnp.zeros(st_ref.shape, F32)

    xbc = _silu(_conv_window(win_ref, xbc_ref, cw_ref[...], c, zi) + cb_ref[...])
    xs = xbc[:, :BR_W]
    gs = SSM_GROUPS * SSM_STATE
    bm = xbc[:, BR_W:BR_W + gs]
    cm = xbc[:, BR_W + gs:]
    lane = _iota2((1, LANE), 1)
    head_lane = lane < SSM_HEADS
    dt = _softplus(sm_ref[:, :LANE] + bias_ref[:, :LANE])
    a = jnp.where(head_lane, -jnp.exp(alog_ref[:, :LANE]), 0.0)
    dt = jnp.where(head_lane, dt, 0.0)
    acs = _dot_f32(_tril_f32(c), dt * a)
    acs_t = acs.T
    acs_last = acs[c - 1:c, :]
    expander = _head_expander(SSM_HEADS, SSM_HEAD).astype(F32)
    xdt = xs * _dot_f32(dt, expander)
    w_t = (xdt * _dot_f32(jnp.exp(acs_last - acs), expander)).T
    causal = _iota2((c, c), 0) >= _iota2((c, c), 1)
    rep = SSM_HEADS // SSM_GROUPS
    heads = range(SSM_HEADS)
    sl = [slice(h * SSM_HEAD, (h + 1) * SSM_HEAD) for h in heads]
    bm_g = [bm[:, g * SSM_STATE:(g + 1) * SSM_STATE].astype(BF16) for g in range(SSM_GROUPS)]
    cm_g = [cm[:, g * SSM_STATE:(g + 1) * SSM_STATE].astype(BF16) for g in range(SSM_GROUPS)]
    cb = [_dot_nt(cm_g[g], bm_g[g]) for g in range(SSM_GROUPS)]
    h_prev = [st_ref[h] for h in heads]
    y_off = [_dot_nt(cm_g[h // rep], h_prev[h]) for h in heads]
    st_add = [_dot(w_t[sl[h], :], bm_g[h // rep]) for h in heads]
    decay = [jnp.exp(jnp.where(causal, acs[:, h:h + 1] - acs_t[h:h + 1, :], -jnp.inf)) for h in heads]
    y_diag = [_dot(cb[h // rep] * decay[h], xdt[:, sl[h]]) for h in heads]
    for h in heads:
        yd_ref[:, sl[h]] = y_diag[h]
        yo_ref[:, sl[h]] = y_off[h]
        st_ref[h] = h_prev[h] * jnp.exp(acs_last[:, h:h + 1]) + st_add[h]
    y = yd_ref[...] + yo_ref[...] * _dot_f32(jnp.exp(acs), expander) + dsk_ref[...] * xs
    y = y * _silu(z_ref[...])
    gw = BR_W // SSM_GROUPS
    ng = ng_ref[...]
    o_ref[...] = jnp.concatenate([_rms(y[:, g * gw:(g + 1) * gw], ng[:, g * gw:(g + 1) * gw])
                                  for g in range(SSM_GROUPS)], axis=1)


def _full(shape):
    return pl.BlockSpec(shape, lambda *a: (0,) * len(shape))


def _ssd_prompt(proj, cw, cb, bias_row, alog_row, dsk, ng, bsz, seq):
    c = SSM_CHUNK
    nc = seq // c
    return pl.pallas_call(
        functools.partial(_ssd_prompt_kernel, c=c),
        grid=(bsz, nc),
        in_specs=[pl.BlockSpec((c, BR_W), lambda b, z: (b * nc + z, P_SZ // BR_W)),
                  pl.BlockSpec((c, SSM_CONV_CH), lambda b, z: (b * nc + z, P_XBC // SSM_CONV_CH)),
                  pl.BlockSpec((c, SMALL_W), lambda b, z: (b * nc + z, P_SMALL // SMALL_W)),
                  _full((CONV_W, SSM_CONV_CH)), _full((1, SSM_CONV_CH)), _full((1, SMALL_W)), _full((1, SMALL_W)),
                  _full((1, BR_W)), _full((1, BR_W))],
        out_specs=[pl.BlockSpec((c, BR_W), lambda b, z: (b * nc + z, 0)),
                   pl.BlockSpec((None, SSM_HEADS, SSM_HEAD, SSM_STATE), lambda b, z: (b, 0, 0, 0))],
        out_shape=[jax.ShapeDtypeStruct((bsz * seq, BR_W), F32),
                   jax.ShapeDtypeStruct((bsz, SSM_HEADS, SSM_HEAD, SSM_STATE), F32)],
        scratch_shapes=[pltpu.VMEM((c + 8, SSM_CONV_CH), F32), pltpu.VMEM((c, BR_W), F32), pltpu.VMEM((c, BR_W), F32)],
        compiler_params=_cparams("parallel", "arbitrary"),
        name="ssd_prompt")(proj, proj, proj, cw, cb, bias_row, alog_row, dsk, ng)


def _inv_unit_lower_minus_eye(mats, n, c):
    row = _iota2((n, n), 0)
    col = _iota2((n, n), 1)
    sh = 4
    ps = [jnp.where((row >> sh) == (col >> sh), a, 0.0) for a in mats]
    ys = [-p for p in ps]
    for _ in range(sh - 1):
        ps = [_dot(p, p) for p in ps]
        yp = [_dot(y, p) for y, p in zip(ys, ps)]
        ys = [y + p + t for y, p, t in zip(ys, ps, yp)]
    while (1 << sh) < c:
        mask = ((row >> (sh + 1)) == (col >> (sh + 1))) & ((row >> sh) != (col >> sh))
        offs = [jnp.where(mask, a, 0.0) for a in mats]
        ts = [off + _dot(y, off) for y, off in zip(ys, offs)]
        ty = [_dot(t, y) for t, y in zip(ts, ys)]
        ys = [y - (t + u) for y, t, u in zip(ys, ts, ty)]
        sh += 1
    return ys


def _gdn_prompt_kernel(qkv_ref, z_ref, sm_ref, cw_ref, bias_ref, alog_ref, ng_ref, o_ref, st_ref, win_ref, *, c):
    zi = pl.program_id(1)
    nh = GDN_HEADS
    hd = GDN_HEAD
    heads = range(nh)

    @pl.when(zi == 0)
    def _():
        st_ref[...] = jnp.zeros(st_ref.shape, F32)

    qkv = _silu(_conv_window(win_ref, qkv_ref, cw_ref[...], c, zi))
    pre = sm_ref[:, :LANE] + bias_ref[:, :LANE]
    g_all = -jnp.exp(alog_ref[:, :LANE]) * _softplus(pre)
    beta_all = _sigmoid(sm_ref[:, :LANE])
    gc = _dot_f32(_tril_f32(c), g_all)
    gc_t = gc.T
    row = _iota2((c, c), 0)
    col = _iota2((c, c), 1)
    l2 = lambda t: t * lax.rsqrt(jnp.sum(t * t, axis=-1, keepdims=True) + EPS)

    q = [l2(qkv[:, h * hd:(h + 1) * hd]) * (hd ** -0.5) for h in heads]
    k = [l2(qkv[:, BR_W + h * hd:BR_W + (h + 1) * hd]) for h in heads]
    v = [qkv[:, 2 * BR_W + h * hd:2 * BR_W + (h + 1) * hd] for h in heads]
    g_col = [gc[:, S_GA + h:S_GA + h + 1] for h in heads]
    g_last = [gc[c - 1:c, S_GA + h:S_GA + h + 1] for h in heads]
    beta = [beta_all[:, S_GB + h:S_GB + h + 1] for h in heads]
    decay = [jnp.exp(jnp.where(row >= col, g_col[h] - gc_t[S_GA + h:S_GA + h + 1, :], -jnp.inf)) for h in heads]
    kb = [k[h] * beta[h] for h in heads]
    eg = [jnp.exp(g_col[h]) for h in heads]
    a_low = [jnp.where(row > col, _dot_nt(kb[h], k[h]) * decay[h], 0.0) for h in heads]
    attn = [_dot_nt(q[h], k[h]) * decay[h] for h in heads]
    t_dev = _inv_unit_lower_minus_eye(a_low, c, c)
    vb = [v[h] * beta[h] for h in heads]
    kbe = [kb[h] * eg[h] for h in heads]
    u = [vb[h] + _dot(t_dev[h], vb[h]) for h in heads]
    w = [kbe[h] + _dot(t_dev[h], kbe[h]) for h in heads]
    state = [st_ref[h] for h in heads]
    v_new = [u[h] - _dot(w[h], state[h]) for h in heads]
    o = [_dot(q[h] * eg[h], state[h]) + _dot(attn[h], v_new[h]) for h in heads]
    kg_t = [(k[h] * jnp.exp(g_last[h] - g_col[h])).T for h in heads]
    for h in heads:
        st_ref[h] = state[h] * jnp.exp(g_last[h]) + _dot(kg_t[h], v_new[h])
    ng = ng_ref[...]
    for h in heads:
        o_ref[:, h * hd:(h + 1) * hd] = _rms(o[h], ng) * _silu(z_ref[:, h * hd:(h + 1) * hd])


GDN_KERNEL_CHUNK = 128


def _gdn_prompt(proj, cw, bias_row, alog_row, ng, bsz, seq):
    rows = GDN_KERNEL_CHUNK
    nc = seq // rows
    return pl.pallas_call(
        functools.partial(_gdn_prompt_kernel, c=rows),
        grid=(bsz, nc),
        in_specs=[pl.BlockSpec((rows, GDN_CONV_CH), lambda b, z: (b * nc + z, P_GQKV // GDN_CONV_CH)),
                  pl.BlockSpec((rows, BR_W), lambda b, z: (b * nc + z, P_GZ // BR_W)),
                  pl.BlockSpec((rows, SMALL_W), lambda b, z: (b * nc + z, P_SMALL // SMALL_W)),
                  _full((CONV_W, GDN_CONV_CH)), _full((1, SMALL_W)), _full((1, SMALL_W)), _full((1, GDN_HEAD))],
        out_specs=[pl.BlockSpec((rows, BR_W), lambda b, z: (b * nc + z, 0)),
                   pl.BlockSpec((None, GDN_HEADS, GDN_HEAD, GDN_HEAD), lambda b, z: (b, 0, 0, 0))],
        out_shape=[jax.ShapeDtypeStruct((bsz * seq, BR_W), F32),
                   jax.ShapeDtypeStruct((bsz, GDN_HEADS, GDN_HEAD, GDN_HEAD), F32)],
        scratch_shapes=[pltpu.VMEM((rows + 8, GDN_CONV_CH), F32)],
        compiler_params=_cparams("parallel", "arbitrary"),
        name="gdn_prompt")(proj, proj, proj, cw, bias_row, alog_row, ng)


def _mlstm_prompt_kernel(q_ref, k_ref, v_ref, og_ref, sm_ref, bias_ref, ng_ref, o_ref, c_ref, n_ref, m_ref, *, c):
    zi = pl.program_id(1)

    @pl.when(zi == 0)
    def _():
        c_ref[...] = jnp.zeros(c_ref.shape, F32)
        n_ref[...] = jnp.zeros(n_ref.shape, F32)
        m_ref[...] = jnp.zeros(m_ref.shape, F32)

    pre = sm_ref[:, :LANE] + bias_ref[:, :LANE]
    logf = -_softplus(-pre)
    bcum = _dot_f32(_tril_f32(c), logf)
    bcum_t = bcum.T
    pre_t = pre.T
    causal = _iota2((c, c), 0) >= _iota2((c, c), 1)
    ng = ng_ref[...]
    hd = ML_HEAD
    heads = range(ML_HEADS)
    sl = [slice(h * hd, (h + 1) * hd) for h in heads]
    q = [q_ref[:, sl[h]] for h in heads]
    k = [k_ref[:, sl[h]] * (hd ** -0.5) for h in heads]
    v = [v_ref[:, sl[h]] for h in heads]
    c_prev = [c_ref[h] for h in heads]
    n_prev = [n_ref[h:h + 1, :] for h in heads]
    m_prev = [m_ref[h:h + 1, 0:1] for h in heads]
    qk = [_dot_nt(q[h], k[h]) for h in heads]
    qc = [_dot(q[h], c_prev[h]) for h in heads]
    b_col = [bcum[:, S_MF + h:S_MF + h + 1] for h in heads]
    b_last = [b_col[h][c - 1:c, :] for h in heads]
    i_col = [pre[:, S_MI + h:S_MI + h + 1] for h in heads]
    dmat = [jnp.where(causal, b_col[h] - bcum_t[S_MF + h:S_MF + h + 1, :] + pre_t[S_MI + h:S_MI + h + 1, :], -jnp.inf)
            for h in heads]
    m_t = [jnp.maximum(b_col[h] + m_prev[h], jnp.max(dmat[h], axis=1, keepdims=True)) for h in heads]
    m_new = [m_t[h][c - 1:c, :] for h in heads]
    s = [qk[h] * jnp.exp(dmat[h] - m_t[h]) for h in heads]
    kw = [k[h] * jnp.exp(b_last[h] - b_col[h] + i_col[h] - m_new[h]) for h in heads]
    sv = [_dot(s[h], v[h]) for h in heads]
    kv = [_dot(kw[h].T, v[h]) for h in heads]
    for h in heads:
        w_prev = jnp.exp(b_col[h] + m_prev[h] - m_t[h])
        num = w_prev * qc[h] + sv[h]
        den = w_prev * jnp.sum(q[h] * n_prev[h], axis=1, keepdims=True) + jnp.sum(s[h], axis=1, keepdims=True)
        hid = num / jnp.maximum(jnp.abs(den), jnp.exp(-m_t[h]))
        w_c = jnp.exp(b_last[h] + m_prev[h] - m_new[h])
        c_ref[h] = c_prev[h] * w_c + kv[h]
        n_ref[h:h + 1, :] = n_prev[h] * w_c + jnp.sum(kw[h], axis=0, keepdims=True)
        m_ref[h:h + 1, :] = jnp.broadcast_to(m_new[h], (1, LANE))
        o_ref[:, sl[h]] = _rms(hid, ng) * _sigmoid(og_ref[:, sl[h]])


def _mlstm_prompt(proj, bias_row, ng, bsz, seq):
    c = ML_CHUNK
    nc = seq // c
    col = lambda i: pl.BlockSpec((c, BR_W), lambda b, z: (b * nc + z, P_MQ // BR_W + i))
    return pl.pallas_call(
        functools.partial(_mlstm_prompt_kernel, c=c),
        grid=(bsz, nc),
        in_specs=[col(0), col(1), col(2), col(3),
                  pl.BlockSpec((c, SMALL_W), lambda b, z: (b * nc + z, P_SMALL // SMALL_W)),
                  _full((1, SMALL_W)), _full((1, ML_HEAD))],
        out_specs=[pl.BlockSpec((c, BR_W), lambda b, z: (b * nc + z, 0)),
                   pl.BlockSpec((None, ML_HEADS, ML_HEAD, ML_HEAD), lambda b, z: (b, 0, 0, 0)),
                   pl.BlockSpec((None, ML_HEADS, ML_HEAD), lambda b, z: (b, 0, 0)),
                   pl.BlockSpec((None, 8, LANE), lambda b, z: (b, 0, 0))],
        out_shape=[jax.ShapeDtypeStruct((bsz * seq, BR_W), F32),
                   jax.ShapeDtypeStruct((bsz, ML_HEADS, ML_HEAD, ML_HEAD), F32),
                   jax.ShapeDtypeStruct((bsz, ML_HEADS, ML_HEAD), F32),
                   jax.ShapeDtypeStruct((bsz, 8, LANE), F32)],
        compiler_params=_cparams("parallel", "arbitrary"),
        name="mlstm_prompt")(proj, proj, proj, proj, proj, bias_row, ng)


STEP_ROWS = 8


def _split_f32(x):
    hi = x.astype(BF16).astype(F32)
    return hi, (x - hi).astype(BF16).astype(F32)


def _outer_lhs(rows):
    hi, lo = _split_f32(rows)
    return jnp.concatenate([hi, hi, lo, jnp.zeros_like(hi)], axis=0).T.astype(BF16)


def _outer_rhs(rows, r):
    keep = _iota2(rows.shape, 0) == r
    hi, lo = _split_f32(jnp.where(keep, rows, 0.0))
    return jnp.concatenate([hi, lo, hi, jnp.zeros_like(hi)], axis=0).astype(BF16)


def _pick_rows(results):
    rowi = _iota2(results[0].shape, 0)
    out = results[0]
    for r in range(1, len(results)):
        out = jnp.where(rowi == r, results[r], out)
    return out


def _step_conv(x_ref, buf_ref, cw_ref, nb_ref):
    cw = cw_ref[...]
    x, b0, b1, b2 = x_ref[...], buf_ref[0], buf_ref[1], buf_ref[2]
    nb_ref[0] = b1
    nb_ref[1] = b2
    nb_ref[2] = x
    return b0 * cw[0:1] + b1 * cw[1:2] + b2 * cw[2:3] + x * cw[3:4]


def _ssd_step_kernel(z_ref, xbc_ref, sm_ref, buf_ref, st_ref, cw_ref, cb_ref, bias_ref, alog_ref, dsk_ref, ng_ref,
                     o_ref, nb_ref, nst_ref):
    rb = STEP_ROWS
    xbc = _silu(_step_conv(xbc_ref, buf_ref, cw_ref, nb_ref) + cb_ref[...])
    xs = xbc[:, :BR_W]
    gs = SSM_GROUPS * SSM_STATE
    bm = xbc[:, BR_W:BR_W + gs]
    cm = xbc[:, BR_W + gs:]
    dt = _softplus(sm_ref[:, :LANE] + bias_ref[:, :LANE])
    d_a = jnp.exp(-jnp.exp(alog_ref[:, :LANE]) * dt)
    lane = _iota2((rb, 2 * SSM_HEAD), 1)
    sub = _iota2((2 * SSM_HEAD, 1), 0)
    pairs_per_group = SSM_HEADS // SSM_GROUPS // 2
    ys = []
    for j in range(SSM_HEADS // 2):
        g = j // pairs_per_group
        h0, h1 = 2 * j, 2 * j + 1
        dt_pair = jnp.where(lane < SSM_HEAD, dt[:, h0:h0 + 1], dt[:, h1:h1 + 1])
        xt = _outer_lhs(xs[:, j * 2 * SSM_HEAD:(j + 1) * 2 * SSM_HEAD] * dt_pair)
        bm_g = bm[:, g * SSM_STATE:(g + 1) * SSM_STATE]
        cm_t = cm[:, g * SSM_STATE:(g + 1) * SSM_STATE].T.astype(BF16)
        coli = _iota2((2 * SSM_HEAD, rb), 1)
        y_t = jnp.zeros((2 * SSM_HEAD, rb), F32)
        for r in range(rb):
            decay = jnp.where(sub < SSM_HEAD, d_a[r:r + 1, h0:h0 + 1], d_a[r:r + 1, h1:h1 + 1])
            h_new = st_ref[r, j] * decay + jnp.dot(xt, _outer_rhs(bm_g, r), preferred_element_type=F32)
            nst_ref[r, j] = h_new
            y_t = jnp.where(coli == r, jnp.dot(h_new.astype(BF16), cm_t, preferred_element_type=F32), y_t)
        ys.append(y_t.T)
    y = (jnp.concatenate(ys, axis=1) + dsk_ref[...] * xs) * _silu(z_ref[...])
    gw = BR_W // SSM_GROUPS
    ng = ng_ref[...]
    o_ref[...] = jnp.concatenate([_rms(y[:, g * gw:(g + 1) * gw], ng[:, g * gw:(g + 1) * gw])
                                  for g in range(SSM_GROUPS)], axis=1)


def _ssd_step(proj, buf, st, layer, cw, cb, bias_row, alog_row, dsk, ng):
    db = proj.shape[0]
    rb = STEP_ROWS
    npair = SSM_HEADS // 2
    sdim = 2 * SSM_HEAD
    return pl.pallas_call(
        _ssd_step_kernel,
        grid=(db // rb,),
        in_specs=[pl.BlockSpec((rb, BR_W), lambda i: (i, P_SZ // BR_W)),
                  pl.BlockSpec((rb, SSM_CONV_CH), lambda i: (i, P_XBC // SSM_CONV_CH)),
                  pl.BlockSpec((rb, SMALL_W), lambda i: (i, P_SMALL // SMALL_W)),
                  pl.BlockSpec((None, CONV_W - 1, rb, SSM_CONV_CH), lambda i: (layer, 0, i, 0)),
                  pl.BlockSpec((None, rb, npair, sdim, SSM_STATE), lambda i: (layer, i, 0, 0, 0)),
                  _full((CONV_W, SSM_CONV_CH)), _full((1, SSM_CONV_CH)), _full((1, SMALL_W)), _full((1, SMALL_W)),
                  _full((1, BR_W)), _full((1, BR_W))],
        out_specs=[pl.BlockSpec((rb, BR_W), lambda i: (i, 0)),
                   pl.BlockSpec((CONV_W - 1, rb, SSM_CONV_CH), lambda i: (0, i, 0)),
                   pl.BlockSpec((rb, npair, sdim, SSM_STATE), lambda i: (i, 0, 0, 0))],
        out_shape=[jax.ShapeDtypeStruct((db, BR_W), F32),
                   jax.ShapeDtypeStruct((CONV_W - 1, db, SSM_CONV_CH), F32),
                   jax.ShapeDtypeStruct((db, npair, sdim, SSM_STATE), F32)],
        compiler_params=_cparams("parallel"),
        name="ssd_step")(proj, proj, proj, buf, st, cw, cb, bias_row, alog_row, dsk, ng)


def _gdn_step_kernel(qkv_ref, z_ref, sm_ref, buf_ref, st_ref, cw_ref, bias_ref, alog_ref, ng_ref,
                     o_ref, nb_ref, nst_ref):
    rb = STEP_ROWS
    hd = GDN_HEAD
    ng = ng_ref[...]
    qkv = _silu(_step_conv(qkv_ref, buf_ref, cw_ref, nb_ref))
    sm = sm_ref[:, :LANE]
    eg_all = jnp.exp(-jnp.exp(alog_ref[:, :LANE]) * _softplus(sm + bias_ref[:, :LANE]))
    beta_all = _sigmoid(sm)
    l2 = lambda t: t * lax.rsqrt(jnp.sum(t * t, axis=-1, keepdims=True) + EPS)
    for h in range(GDN_HEADS):
        q = l2(qkv[:, h * hd:(h + 1) * hd]) * (hd ** -0.5)
        k = l2(qkv[:, BR_W + h * hd:BR_W + (h + 1) * hd])
        v = qkv[:, 2 * BR_W + h * hd:2 * BR_W + (h + 1) * hd]
        eg = eg_all[:, S_GA + h:S_GA + h + 1]
        beta = beta_all[:, S_GB + h:S_GB + h + 1]
        lhs = jnp.concatenate([k * (beta * eg), q * eg], axis=0).astype(BF16)
        res = [jnp.dot(lhs, st_ref[r, h].astype(BF16), preferred_element_type=F32) for r in range(rb)]
        v_new = v * beta - _pick_rows([t[:rb] for t in res])
        o = _pick_rows([t[rb:] for t in res]) + jnp.sum(q * k, axis=1, keepdims=True) * v_new
        kt = _outer_lhs(k)
        for r in range(rb):
            nst_ref[r, h] = (st_ref[r, h] * eg[r:r + 1, :]
                             + jnp.dot(kt, _outer_rhs(v_new, r), preferred_element_type=F32))
        o_ref[:, h * hd:(h + 1) * hd] = _rms(o, ng) * _silu(z_ref[:, h * hd:(h + 1) * hd])


def _gdn_step(proj, buf, st, layer, cw, bias_row, alog_row, ng):
    db = proj.shape[0]
    rb = STEP_ROWS
    return pl.pallas_call(
        _gdn_step_kernel,
        grid=(db // rb,),
        in_specs=[pl.BlockSpec((rb, GDN_CONV_CH), lambda i: (i, P_GQKV // GDN_CONV_CH)),
                  pl.BlockSpec((rb, BR_W), lambda i: (i, P_GZ // BR_W)),
                  pl.BlockSpec((rb, SMALL_W), lambda i: (i, P_SMALL // SMALL_W)),
                  pl.BlockSpec((None, CONV_W - 1, rb, GDN_CONV_CH), lambda i: (layer, 0, i, 0)),
                  pl.BlockSpec((None, rb, GDN_HEADS, GDN_HEAD, GDN_HEAD), lambda i: (layer, i, 0, 0, 0)),
                  _full((CONV_W, GDN_CONV_CH)), _full((1, SMALL_W)), _full((1, SMALL_W)), _full((1, GDN_HEAD))],
        out_specs=[pl.BlockSpec((rb, BR_W), lambda i: (i, 0)),
                   pl.BlockSpec((CONV_W - 1, rb, GDN_CONV_CH), lambda i: (0, i, 0)),
                   pl.BlockSpec((rb, GDN_HEADS, GDN_HEAD, GDN_HEAD), lambda i: (i, 0, 0, 0))],
        out_shape=[jax.ShapeDtypeStruct((db, BR_W), F32),
                   jax.ShapeDtypeStruct((CONV_W - 1, db, GDN_CONV_CH), F32),
                   jax.ShapeDtypeStruct((db, GDN_HEADS, GDN_HEAD, GDN_HEAD), F32)],
        compiler_params=_cparams("parallel"),
        name="gdn_step")(proj, proj, proj, buf, st, cw, bias_row, alog_row, ng)


def _mlstm_step_kernel(q_ref, k_ref, v_ref, og_ref, sm_ref, c_ref, n_ref, m_ref, bias_ref, ng_ref,
                       o_ref, nc_ref, nn_ref, nm_ref):
    rb = STEP_ROWS
    hd = ML_HEAD
    ng = ng_ref[...]
    pre = sm_ref[:, :LANE] + bias_ref[:, :LANE]
    logf_all = -_softplus(-pre)
    m_all = m_ref[...]
    lane4 = _iota2((rb, ML_HEADS), 1)
    m_out = jnp.zeros((rb, ML_HEADS), F32)
    for h in range(ML_HEADS):
        sl = slice(h * hd, (h + 1) * hd)
        q = q_ref[:, sl]
        k = k_ref[:, sl] * (hd ** -0.5)
        v = v_ref[:, sl]
        i_pre = pre[:, S_MI + h:S_MI + h + 1]
        logf = logf_all[:, S_MF + h:S_MF + h + 1]
        m_prev = m_all[:, h:h + 1]
        m_t = jnp.maximum(logf + m_prev, i_pre)
        w_prev = jnp.exp(logf + m_prev - m_t)
        w_j = jnp.exp(i_pre - m_t)
        s = jnp.sum(q * k, axis=1, keepdims=True) * w_j
        n_prev = n_ref[pl.ds(h, rb, stride=ML_HEADS), :]
        qb = q.astype(BF16)
        qc = _pick_rows([jnp.dot(qb, c_ref[r, h].astype(BF16), preferred_element_type=F32) for r in range(rb)])
        num = w_prev * qc + s * v
        den = w_prev * jnp.sum(q * n_prev, axis=1, keepdims=True) + s
        hid = num / jnp.maximum(jnp.abs(den), jnp.exp(-m_t))
        kw = k * w_j
        kt = _outer_lhs(kw)
        for r in range(rb):
            nc_ref[r, h] = (c_ref[r, h] * w_prev[r:r + 1, :]
                            + jnp.dot(kt, _outer_rhs(v, r), preferred_element_type=F32))
        nn_ref[h] = n_prev * w_prev + kw
        m_out = jnp.where(lane4 == h, m_t, m_out)
        o_ref[:, sl] = _rms(hid, ng) * _sigmoid(og_ref[:, sl])
    nm_ref[...] = m_out


def _mlstm_step(proj, c0, n0, m0, layer, bias_row, ng):
    db = proj.shape[0]
    rb = STEP_ROWS
    col = lambda j: pl.BlockSpec((rb, BR_W), lambda i: (i, P_MQ // BR_W + j))
    return pl.pallas_call(
        _mlstm_step_kernel,
        grid=(db // rb,),
        in_specs=[col(0), col(1), col(2), col(3),
                  pl.BlockSpec((rb, SMALL_W), lambda i: (i, P_SMALL // SMALL_W)),
                  pl.BlockSpec((None, rb, ML_HEADS, ML_HEAD, ML_HEAD), lambda i: (layer, i, 0, 0, 0)),
                  pl.BlockSpec((None, rb * ML_HEADS, ML_HEAD), lambda i: (layer, i, 0)),
                  pl.BlockSpec((None, rb, ML_HEADS), lambda i: (layer, i, 0)),
                  _full((1, SMALL_W)), _full((1, ML_HEAD))],
        out_specs=[pl.BlockSpec((rb, BR_W), lambda i: (i, 0)),
                   pl.BlockSpec((rb, ML_HEADS, ML_HEAD, ML_HEAD), lambda i: (i, 0, 0, 0)),
                   pl.BlockSpec((ML_HEADS, rb, ML_HEAD), lambda i: (0, i, 0)),
                   pl.BlockSpec((rb, ML_HEADS), lambda i: (i, 0))],
        out_shape=[jax.ShapeDtypeStruct((db, BR_W), F32),
                   jax.ShapeDtypeStruct((db, ML_HEADS, ML_HEAD, ML_HEAD), F32),
                   jax.ShapeDtypeStruct((ML_HEADS, db, ML_HEAD), F32),
                   jax.ShapeDtypeStruct((db, ML_HEADS), F32)],
        compiler_params=_cparams("parallel"),
        name="mlstm_step")(proj, proj, proj, proj, proj, c0, n0, m0, bias_row, ng)


def _merge_kernel(x_ref, a_ref, b_ref, c_ref, d_ref, g0, g1, g2, g3, wb_ref, wo_ref, o_ref):
    acc = None
    for n, (br, gate) in enumerate(((a_ref, g0), (b_ref, g1), (c_ref, g2), (d_ref, g3))):
        t = _sigmoid(gate[...]) * jnp.dot(br[...].astype(BF16), wb_ref[n], preferred_element_type=F32)
        acc = t if acc is None else acc + t
    o_ref[...] = x_ref[...] + jnp.dot(acc.astype(BF16), wo_ref[...], preferred_element_type=F32)


def _merge(x, branches, proj, wb, wo, tm):
    m = x.shape[0]
    br = pl.BlockSpec((tm, BR_W), lambda i: (i, 0))
    gate = lambda n: pl.BlockSpec((tm, D_MODEL), lambda i: (i, P_GATE // D_MODEL + n))
    return pl.pallas_call(
        _merge_kernel,
        grid=(m // tm,),
        in_specs=[pl.BlockSpec((tm, D_MODEL), lambda i: (i, 0)), br, br, br, br,
                  gate(0), gate(1), gate(2), gate(3),
                  _full((N_BRANCH, BR_W, D_MODEL)), _full((D_MODEL, D_MODEL))],
        out_specs=pl.BlockSpec((tm, D_MODEL), lambda i: (i, 0)),
        out_shape=jax.ShapeDtypeStruct((m, D_MODEL), F32),
        compiler_params=_cparams("parallel"),
        name="merge")(x, *branches, proj, proj, proj, proj, wb, wo)


def _cross_prompt_kernel(x_ref, g_ref, mk_ref, mv_ref, wq_ref, wo_ref, o_ref):
    x = x_ref[...]
    h = _rms(x, g_ref[...]).astype(BF16)
    q = jnp.dot(h, wq_ref[...], preferred_element_type=F32) * (X_HEAD ** -0.5)
    outs = []
    for hd in range(X_HEADS):
        sl = slice(hd * X_HEAD, (hd + 1) * X_HEAD)
        s = _dot_nt(q[:, sl], mk_ref[:, sl])
        p = jnp.exp(s - jnp.max(s, axis=1, keepdims=True))
        p = p / jnp.sum(p, axis=1, keepdims=True)
        outs.append(_dot(p, mv_ref[:, sl]))
    o = jnp.concatenate(outs, axis=1).astype(BF16)
    o_ref[...] = x + jnp.dot(o, wo_ref[...], preferred_element_type=F32)


def _cross_prompt(x, g, mkv, wq, wo, bsz, seq, tq=512):
    nq = seq // tq
    d = D_MODEL
    return pl.pallas_call(
        _cross_prompt_kernel,
        grid=(bsz, nq),
        in_specs=[pl.BlockSpec((tq, d), lambda b, i: (b * nq + i, 0)), _full((1, d)),
                  pl.BlockSpec((N_MEM, d), lambda b, i: (b, 0)),
                  pl.BlockSpec((N_MEM, d), lambda b, i: (b, 1)),
                  _full((d, d)), _full((d, d))],
        out_specs=pl.BlockSpec((tq, d), lambda b, i: (b * nq + i, 0)),
        out_shape=jax.ShapeDtypeStruct((bsz * seq, d), F32),
        compiler_params=_cparams("parallel", "arbitrary"),
        name="cross_prompt")(x, g.reshape(1, d), mkv, mkv, wq, wo)


CROSS_ROWS = 4


def _cross_decode_kernel(q_ref, mk_ref, mv_ref, o_ref):
    halves = X_HEAD // LANE
    rows_per_tok = halves * X_HEADS

    def head_slab(ref, r, h):
        return jnp.concatenate([ref[r, pl.ds(t * X_HEADS + h, N_MEM, stride=rows_per_tok), :].astype(BF16)
                                for t in range(halves)], axis=1)

    pairs = [(r, h) for r in range(CROSS_ROWS) for h in range(X_HEADS)]
    qs = [q_ref[r] * (X_HEAD ** -0.5) for r in range(CROSS_ROWS)]
    scores = [lax.dot_general(jnp.broadcast_to(qs[r][:, h * X_HEAD:(h + 1) * X_HEAD], (8, X_HEAD)).astype(BF16),
                              head_slab(mk_ref, r, h), (((1,), (1,)), ((), ())), preferred_element_type=F32)
              for r, h in pairs]
    probs = []
    for s in scores:
        e = jnp.exp(s - jnp.max(s, axis=1, keepdims=True))
        probs.append((e / jnp.sum(e, axis=1, keepdims=True)).astype(BF16))
    outs = [jnp.dot(p, head_slab(mv_ref, r, h), preferred_element_type=F32)[0:1] for p, (r, h) in zip(probs, pairs)]
    for r in range(CROSS_ROWS):
        o_ref[r] = jnp.concatenate(outs[r * X_HEADS:(r + 1) * X_HEADS], axis=1)


def _mem_rows(mem):
    depth, db = mem.shape[:2]
    halves = X_HEAD // LANE
    m = mem.reshape(depth, db, N_MEM, X_HEADS, halves, LANE).transpose(0, 1, 2, 4, 3, 5)
    return m.reshape(depth, db, N_MEM * halves * X_HEADS, LANE)


def _cross_decode(q3, mem_k, mem_v, layer):
    db = q3.shape[0]
    rb = CROSS_ROWS
    d = D_MODEL
    mem = pl.BlockSpec((None, rb, mem_k.shape[2], LANE), lambda i: (layer, i, 0, 0))
    return pl.pallas_call(
        _cross_decode_kernel,
        grid=(db // rb,),
        in_specs=[pl.BlockSpec((rb, 1, d), lambda i: (i, 0, 0)), mem, mem],
        out_specs=pl.BlockSpec((rb, 1, d), lambda i: (i, 0, 0)),
        out_shape=jax.ShapeDtypeStruct((db, 1, d), F32),
        compiler_params=_cparams("parallel"),
        name="cross_decode")(q3, mem_k, mem_v)


def _swiglu_kernel(x_ref, g_ref, wg_ref, wu_ref, wd_ref, o_ref, h_ref, acc_ref):
    j = pl.program_id(1)

    @pl.when(j == 0)
    def _():
        h_ref[...] = _rms(x_ref[...], g_ref[...]).astype(BF16)
        acc_ref[...] = jnp.zeros(acc_ref.shape, F32)

    h = h_ref[...]
    gate = jnp.dot(h, wg_ref[...], preferred_element_type=F32)
    up = jnp.dot(h, wu_ref[...], preferred_element_type=F32)
    acc_ref[...] += jnp.dot((_silu(gate) * up).astype(BF16), wd_ref[...], preferred_element_type=F32)

    @pl.when(j == pl.num_programs(1) - 1)
    def _():
        o_ref[...] = x_ref[...] + acc_ref[...]


def _swiglu(x, g, wgu, wd, tm, tf=D_FF // 2):
    m, d = x.shape
    nf = D_FF // tf
    return pl.pallas_call(
        _swiglu_kernel,
        grid=(m // tm, nf),
        in_specs=[pl.BlockSpec((tm, d), lambda i, j: (i, 0)), _full((1, d)),
                  pl.BlockSpec((d, tf), lambda i, j: (0, j)),
                  pl.BlockSpec((d, tf), lambda i, j: (0, nf + j)),
                  pl.BlockSpec((tf, d), lambda i, j: (j, 0))],
        out_specs=pl.BlockSpec((tm, d), lambda i, j: (i, 0)),
        out_shape=jax.ShapeDtypeStruct((m, d), F32),
        scratch_shapes=[pltpu.VMEM((tm, d), BF16), pltpu.VMEM((tm, d), F32)],
        compiler_params=_cparams("parallel", "arbitrary"),
        name="swiglu")(x, g.reshape(1, d), wgu, wgu, wd)


def _pack_w_in(w):
    offs = np.cumsum((0,) + IN_SPLITS)
    seg = lambda i: w[:, offs[i]:offs[i + 1]]
    small = jnp.concatenate([seg(5), seg(8), seg(9), seg(14), seg(15),
                             jnp.zeros((w.shape[0], SMALL_W - 24), w.dtype)], axis=1)
    order = (0, 1, 2, 3, 4, 6, 7, 10, 11, 12, 13, 16)
    return jnp.concatenate([seg(i) for i in order] + [small], axis=1).astype(BF16)


def _small_row(parts):
    row = jnp.zeros((SMALL_W,), F32)
    for off, val in parts:
        row = lax.dynamic_update_slice(row, val.astype(F32), (off,))
    return row.reshape(1, SMALL_W)


def _layer_params(l, p):
    lp = dict(
        w_in=_pack_w_in(p["w_in"][l]),
        g_mix=p["g_mix"][l],
        lam=tuple(p[n][l].reshape(1, DA_HEAD) for n in ("da_lq1", "da_lk1", "da_lq2", "da_lk2")),
        lam_init=0.8 - 0.6 * math.exp(-0.3 * l),
        sub_g=p["da_sub_g"][l],
        bias_row=_small_row(((S_DT, p["ssm_dt_bias"][l]), (S_GA, p["gdn_dt_bias"][l]),
                             (S_MI, p["ml_i_bias"][l]), (S_MF, p["ml_f_bias"][l]))),
        alog_row=_small_row(((S_DT, p["ssm_a_log"][l]), (S_GA, p["gdn_a_log"][l]))),
        ssm_cw=p["ssm_conv_w"][l], ssm_cb=p["ssm_conv_b"][l].reshape(1, SSM_CONV_CH),
        ssm_dsk=jnp.repeat(p["ssm_d"][l], SSM_HEAD).reshape(1, BR_W),
        ssm_ng=p["ssm_norm_g"][l].reshape(1, BR_W),
        gdn_cw=p["gdn_conv_w"][l], gdn_ng=p["gdn_norm_g"][l].reshape(1, GDN_HEAD),
        ml_ng=p["ml_norm_g"][l].reshape(1, ML_HEAD),
        w_branch=p["w_branch"][l].astype(BF16), w_out=p["w_out"][l].astype(BF16),
        g_cross=p["g_cross"][l], w_cq=p["w_cq"][l].astype(BF16), w_co=p["w_co"][l].astype(BF16),
        g_ffn=p["g_ffn"][l], w_gu=p["w_gu"][l].astype(BF16), w_down=p["w_down"][l].astype(BF16),
    )
    return lp


def _prompt_layer(x, lp, mkv, bsz, seq):
    rows = bsz * seq
    proj = _norm_matmul(x, lp["g_mix"], lp["w_in"], tm=2048 if rows % 2048 == 0 else rows, tn=1152, name="in_proj")
    o_da = _da_prompt(proj, lp["lam"], lp["sub_g"], bsz, seq, lp["lam_init"])
    o_ssm, ssm = _ssd_prompt(proj, lp["ssm_cw"], lp["ssm_cb"], lp["bias_row"], lp["alog_row"], lp["ssm_dsk"],
                             lp["ssm_ng"], bsz, seq)
    o_gdn, gdn = _gdn_prompt(proj, lp["gdn_cw"], lp["bias_row"], lp["alog_row"], lp["gdn_ng"], bsz, seq)
    o_ml, ml_c, ml_n, ml_m = _mlstm_prompt(proj, lp["bias_row"], lp["ml_ng"], bsz, seq)
    x = _merge(x, (o_da, o_ssm, o_gdn, o_ml), proj, lp["w_branch"], lp["w_out"], tm=256)
    x = _cross_prompt(x, lp["g_cross"], mkv, lp["w_cq"], lp["w_co"], bsz, seq, tq=min(512, seq))
    x = _swiglu(x, lp["g_ffn"], lp["w_gu"], lp["w_down"], tm=1024 if rows % 1024 == 0 else 256)
    p3 = proj.reshape(bsz, seq, PACK_W)
    new = dict(
        k=p3[:, :, P_K:P_K + BR_W].reshape(bsz, seq, DA_HEADS, 2, DA_HEAD),
        v=p3[:, :, P_V:P_V + BR_W].reshape(bsz, seq, DA_HEADS, 2 * DA_HEAD),
        ssm_conv=p3[:, seq - (CONV_W - 1):, P_XBC:P_XBC + SSM_CONV_CH], ssm=ssm,
        gdn_conv=p3[:, seq - (CONV_W - 1):, P_GQKV:P_GQKV + GDN_CONV_CH], gdn=gdn,
        ml_c=ml_c, ml_n=ml_n, ml_m=ml_m[:, :ML_HEADS, 0])
    return x, new


def _sample_layer(x, lp, l, caches, states):
    db = x.shape[0]
    cache_k, cache_v, page_table, mem_k, mem_v = caches
    proj = _norm_matmul(x, lp["g_mix"], lp["w_in"], tm=db, tn=1152, name="in_proj_s")
    proj3 = proj.reshape(db, 1, PACK_W)
    o_da = _da_decode(proj3, lp["lam"], lp["sub_g"], cache_k, cache_v, page_table, l, lp["lam_init"])
    o_ssm, ssm_conv, ssm = _ssd_step(proj, states["ssm_conv"], states["ssm"], l, lp["ssm_cw"], lp["ssm_cb"],
                                     lp["bias_row"], lp["alog_row"], lp["ssm_dsk"], lp["ssm_ng"])
    o_gdn, gdn_conv, gdn = _gdn_step(proj, states["gdn_conv"], states["gdn"], l, lp["gdn_cw"],
                                     lp["bias_row"], lp["alog_row"], lp["gdn_ng"])
    o_ml, ml_c, ml_n, ml_m = _mlstm_step(proj, states["ml_c"], states["ml_n"], states["ml_m"], l,
                                         lp["bias_row"], lp["ml_ng"])
    x = _merge(x, (o_da, o_ssm, o_gdn, o_ml), proj, lp["w_branch"], lp["w_out"], tm=db)
    q = _norm_matmul(x, lp["g_cross"], lp["w_cq"], tm=db, tn=D_MODEL, name="cross_q_s")
    att = _cross_decode(q.reshape(db, 1, D_MODEL), mem_k, mem_v, l)
    x = _matmul_residual(x, att.reshape(db, D_MODEL), lp["w_co"], tm=db, name="cross_o_s")
    x = _swiglu(x, lp["g_ffn"], lp["w_gu"], lp["w_down"], tm=db)
    new = dict(
        k=proj[:, P_K:P_K + BR_W].reshape(db, 1, DA_HEADS, 2, DA_HEAD),
        v=proj[:, P_V:P_V + BR_W].reshape(db, 1, DA_HEADS, 2 * DA_HEAD),
        ssm_conv=ssm_conv.transpose(1, 0, 2), ssm=ssm.reshape(db, SSM_HEADS, SSM_HEAD, SSM_STATE),
        gdn_conv=gdn_conv.transpose(1, 0, 2), gdn=gdn,
        ml_c=ml_c, ml_n=ml_n.transpose(1, 0, 2), ml_m=ml_m)
    return x, new


_STATE_ORDER = ("ssm_conv", "ssm", "gdn_conv", "gdn", "ml_c", "ml_n", "ml_m")


def kernel(x_prompt, x_sample, cache_k, cache_v, cache_mem_k, cache_mem_v, state_ssm_conv, state_ssm, state_gdn_conv, state_gdn, state_mlstm_c, state_mlstm_n, state_mlstm_m, page_table, mem_prompt, g_mix, w_in, da_lq1, da_lk1, da_lq2, da_lk2, da_sub_g, ssm_conv_w, ssm_conv_b, ssm_dt_bias, ssm_a_log, ssm_d, ssm_norm_g, gdn_conv_w, gdn_dt_bias, gdn_a_log, gdn_norm_g, ml_i_bias, ml_f_bias, ml_norm_g, w_branch, w_out, g_cross, g_mem, w_cq, w_ckv, w_co, g_ffn, w_gu, w_down, g_final):
    p = dict(g_mix=g_mix, w_in=w_in, da_lq1=da_lq1, da_lk1=da_lk1, da_lq2=da_lq2, da_lk2=da_lk2,
             da_sub_g=da_sub_g, ssm_conv_w=ssm_conv_w, ssm_conv_b=ssm_conv_b, ssm_dt_bias=ssm_dt_bias,
             ssm_a_log=ssm_a_log, ssm_d=ssm_d, ssm_norm_g=ssm_norm_g, gdn_conv_w=gdn_conv_w,
             gdn_dt_bias=gdn_dt_bias, gdn_a_log=gdn_a_log, gdn_norm_g=gdn_norm_g,
             ml_i_bias=ml_i_bias, ml_f_bias=ml_f_bias, ml_norm_g=ml_norm_g,
             w_branch=w_branch, w_out=w_out, g_cross=g_cross, w_cq=w_cq, w_co=w_co,
             g_ffn=g_ffn, w_gu=w_gu, w_down=w_down)
    depth = w_in.shape[0]
    bsz, seq, d = x_prompt.shape
    db = x_sample.shape[0]
    n_mem = mem_prompt.shape[1]
    lps = [_layer_params(l, p) for l in range(depth)]

    mem2 = mem_prompt.reshape(bsz * n_mem, d)
    xp = x_prompt.reshape(bsz * seq, d)
    p_new = {n: [] for n in ("k", "v", "mem_k", "mem_v") + _STATE_ORDER}
    for l in range(depth):
        mkv = _norm_matmul(mem2, g_mem[l], w_ckv[l].astype(BF16), tm=min(1024, bsz * n_mem), tn=1024, name="mem_kv")
        xp, new = _prompt_layer(xp, lps[l], mkv, bsz, seq)
        mkv5 = mkv.reshape(bsz, n_mem, 2, X_HEADS, X_HEAD)
        new["mem_k"] = mkv5[:, :, 0]
        new["mem_v"] = mkv5[:, :, 1]
        for n in p_new:
            p_new[n].append(new[n])
    y_prompt = _final_norm(xp, g_final, tm=512 if (bsz * seq) % 512 == 0 else bsz * seq, name="final_norm").reshape(bsz, seq, d)

    n_pool = cache_k.shape[1]
    caches = (cache_k.transpose(0, 1, 3, 4, 5, 2).reshape(depth, n_pool, BR_W, PAGE_SIZE),
              cache_v.reshape(depth, n_pool, PAGE_SIZE * DA_HEADS, 2 * DA_HEAD),
              page_table, _mem_rows(cache_mem_k), _mem_rows(cache_mem_v))
    states = dict(ssm_conv=state_ssm_conv.transpose(0, 2, 1, 3),
                  ssm=state_ssm.reshape(depth, db, SSM_HEADS // 2, 2 * SSM_HEAD, SSM_STATE),
                  gdn_conv=state_gdn_conv.transpose(0, 2, 1, 3), gdn=state_gdn,
                  ml_c=state_mlstm_c, ml_n=state_mlstm_n.reshape(depth, db * ML_HEADS, ML_HEAD), ml_m=state_mlstm_m)
    xs = x_sample.reshape(db, d)
    s_new = {n: [] for n in ("k", "v") + _STATE_ORDER}
    for l in range(depth):
        xs, new = _sample_layer(xs, lps[l], l, caches, states)
        for n in s_new:
            s_new[n].append(new[n])
    y_sample = _final_norm(xs, g_final, tm=db, name="final_norm_s").reshape(db, 1, d)

    stk = lambda dct, n: jnp.stack(dct[n])
    return (y_prompt, y_sample,
            stk(p_new, "k"), stk(p_new, "v"), stk(p_new, "mem_k"), stk(p_new, "mem_v"),
            *(stk(p_new, n) for n in _STATE_ORDER),
            stk(s_new, "k"), stk(s_new, "v"), *(stk(s_new, n) for n in _STATE_ORDER))
```

```python
import functools
import math

import numpy as np
import jax
import jax.numpy as jnp
from jax import lax
from jax.experimental import pallas as pl
from jax.experimental.pallas import tpu as pltpu

F32 = jnp.float32
BF16 = jnp.bfloat16

D_MODEL = 1024
DEPTH = 4
PAGE_SIZE = 128
EPS = 1e-6
N_MEM = 256
CONV_W = 4
N_BRANCH = 4
BR_W = D_MODEL // 2
DA_HEADS = 4
DA_HEAD = BR_W // (2 * DA_HEADS)
SSM_HEAD = 64
SSM_HEADS = BR_W // SSM_HEAD
SSM_GROUPS = 2
SSM_STATE = 128
SSM_CONV_CH = BR_W + 2 * SSM_GROUPS * SSM_STATE
SSM_CHUNK = 128
GDN_HEADS = 4
GDN_HEAD = BR_W // GDN_HEADS
GDN_CONV_CH = 3 * BR_W
GDN_CHUNK = 64
ML_HEADS = 4
ML_HEAD = BR_W // ML_HEADS
ML_CHUNK = 128
X_HEADS = 4
X_HEAD = D_MODEL // X_HEADS
D_FF = -(-8 * D_MODEL // (3 * 256)) * 256

IN_SPLITS = (BR_W, BR_W, BR_W, BR_W, SSM_CONV_CH, SSM_HEADS, GDN_CONV_CH, BR_W, GDN_HEADS, GDN_HEADS,
             BR_W, BR_W, BR_W, BR_W, ML_HEADS, ML_HEADS, N_BRANCH * D_MODEL)

P_Q, P_K, P_V = 0, BR_W, 2 * BR_W
P_SZ = 3 * BR_W
P_XBC = P_SZ + BR_W
P_GQKV = P_XBC + SSM_CONV_CH
P_GZ = P_GQKV + GDN_CONV_CH
P_MQ = P_GZ + BR_W
P_GATE = P_MQ + 4 * BR_W
P_SMALL = P_GATE + N_BRANCH * D_MODEL
SMALL_W = 256
PACK_W = P_SMALL + SMALL_W
S_DT, S_GA, S_GB, S_MI, S_MF = 0, 8, 12, 16, 20

LANE = 128
VMEM_LIMIT = 56 * 1024 * 1024


def _cparams(*sem):
    return pltpu.CompilerParams(dimension_semantics=sem, vmem_limit_bytes=VMEM_LIMIT)


def _dot(a, b):
    return jnp.dot(a.astype(BF16), b.astype(BF16), preferred_element_type=F32)


def _dot_nt(a, b):
    return lax.dot_general(a.astype(BF16), b.astype(BF16), (((1,), (1,)), ((), ())), preferred_element_type=F32)


def _dot_f32(a, b):
    return jnp.dot(a, b, precision=lax.Precision.HIGHEST, preferred_element_type=F32)


def _dot_split(a, b_bf16):
    hi = a.astype(BF16)
    lo = (a - hi.astype(F32)).astype(BF16)
    return (jnp.dot(hi, b_bf16, preferred_element_type=F32) + jnp.dot(lo, b_bf16, preferred_element_type=F32))


def _sigmoid(x):
    return 1.0 / (1.0 + jnp.exp(-x))


def _silu(x):
    return x * _sigmoid(x)


def _softplus(x):
    return jnp.maximum(x, 0.0) + jnp.log(1.0 + jnp.exp(-jnp.abs(x)))


def _rms(x, g):
    return x * lax.rsqrt(jnp.mean(x * x, axis=-1, keepdims=True) + EPS) * g


def _iota2(shape, dim):
    return lax.broadcasted_iota(jnp.int32, shape, dim)


def _row_to_col(x):
    n = x.shape[1]
    eye = _iota2((n, n), 0) == _iota2((n, n), 1)
    return jnp.sum(jnp.where(eye, jnp.broadcast_to(x, (n, n)), 0.0), axis=1, keepdims=True)


def _col_to_row(x):
    n = x.shape[0]
    eye = _iota2((n, n), 0) == _iota2((n, n), 1)
    return jnp.sum(jnp.where(eye, jnp.broadcast_to(x, (n, n)), 0.0), axis=0, keepdims=True)


def _tril_f32(c):
    return (_iota2((c, c), 0) >= _iota2((c, c), 1)).astype(F32)


def _head_expander(n_heads, width):
    rows = _iota2((LANE, n_heads * width), 0)
    cols = _iota2((LANE, n_heads * width), 1)
    return (rows * width <= cols) & (cols < (rows + 1) * width)


def _lam(lq1, lk1, lq2, lk2, lam_init):
    return (jnp.exp(jnp.sum(lq1[...] * lk1[...], axis=1, keepdims=True))
            - jnp.exp(jnp.sum(lq2[...] * lk2[...], axis=1, keepdims=True)) + lam_init)


def _norm_matmul_kernel(x_ref, g_ref, w_ref, o_ref, h_ref):
    @pl.when(pl.program_id(1) == 0)
    def _():
        h_ref[...] = _rms(x_ref[...], g_ref[...]).astype(BF16)

    o_ref[...] = jnp.dot(h_ref[...], w_ref[...], preferred_element_type=F32)


def _norm_matmul(x, g, w, tm, tn, name):
    m, k = x.shape
    n = w.shape[1]
    return pl.pallas_call(
        _norm_matmul_kernel,
        grid=(m // tm, n // tn),
        in_specs=[pl.BlockSpec((tm, k), lambda i, j: (i, 0)),
                  pl.BlockSpec((1, k), lambda i, j: (0, 0)),
                  pl.BlockSpec((k, tn), lambda i, j: (0, j))],
        out_specs=pl.BlockSpec((tm, tn), lambda i, j: (i, j)),
        out_shape=jax.ShapeDtypeStruct((m, n), F32),
        scratch_shapes=[pltpu.VMEM((tm, k), BF16)],
        compiler_params=_cparams("parallel", "arbitrary"),
        name=name)(x, g.reshape(1, k), w)


def _matmul_res_kernel(x_ref, a_ref, w_ref, o_ref):
    o_ref[...] = x_ref[...] + jnp.dot(a_ref[...].astype(BF16), w_ref[...], preferred_element_type=F32)


def _matmul_residual(x, a, w, tm, name):
    m, n = x.shape
    k = a.shape[1]
    return pl.pallas_call(
        _matmul_res_kernel,
        grid=(m // tm,),
        in_specs=[pl.BlockSpec((tm, n), lambda i: (i, 0)),
                  pl.BlockSpec((tm, k), lambda i: (i, 0)),
                  pl.BlockSpec((k, n), lambda i: (0, 0))],
        out_specs=pl.BlockSpec((tm, n), lambda i: (i, 0)),
        out_shape=jax.ShapeDtypeStruct((m, n), F32),
        compiler_params=_cparams("parallel"),
        name=name)(x, a, w)


def _final_norm_kernel(x_ref, g_ref, o_ref):
    o_ref[...] = _rms(x_ref[...], g_ref[...])


def _final_norm(x, g, tm, name):
    m, n = x.shape
    return pl.pallas_call(
        _final_norm_kernel,
        grid=(m // tm,),
        in_specs=[pl.BlockSpec((tm, n), lambda i: (i, 0)), pl.BlockSpec((1, n), lambda i: (0, 0))],
        out_specs=pl.BlockSpec((tm, n), lambda i: (i, 0)),
        out_shape=jax.ShapeDtypeStruct((m, n), F32),
        compiler_params=_cparams("parallel"),
        name=name)(x, g.reshape(1, n))


def _da_prompt_kernel(lq1, lk1, lq2, lk2, subg_ref, q_ref, k_ref, v_ref, o_ref, kb_ref, vt_ref, acc_ref,
                      *, lam_init, tq, cw):
    hw = 2 * DA_HEAD
    nq = vt_ref.shape[0]
    kb_ref[...] = k_ref[...].astype(BF16)
    for t in range(nq):
        vt_ref[t] = v_ref[t * tq:(t + 1) * tq, :].T.astype(BF16)
    lam = _lam(lq1, lk1, lq2, lk2, lam_init)
    subg = subg_ref[...]
    sub = _iota2((hw, tq), 0)
    nch = 2 * tq // cw
    chunks = [slice(c * cw, (c + 1) * cw) for c in range(nch)]

    def scores(j, q2t):
        kj = kb_ref[pl.ds(pl.multiple_of(j * tq, tq), tq), :]
        return tuple(jnp.dot(kj, q2t[:, cols], preferred_element_type=F32) for cols in chunks)

    def update(j, sts, stats, masked):
        new_stats, scaled = [], []
        for c, st in enumerate(sts):
            m, l = stats[c]
            if masked:
                qpos = _iota2((tq, cw), 1) + (c * cw) % tq
                st = jnp.where(_iota2((tq, cw), 0) <= qpos, st, -jnp.inf)
            m_new = jnp.maximum(m, jnp.max(st, axis=0, keepdims=True))
            alpha = jnp.exp(m - m_new)
            p = jnp.exp(st - m_new)
            new_stats.append((m_new, alpha * l + jnp.sum(p, axis=0, keepdims=True)))
            scaled.append((alpha, p.astype(BF16)))
        vtj = vt_ref[j]
        for cols, (alpha, p) in zip(chunks, scaled):
            acc_ref[:, cols] = alpha * acc_ref[:, cols] + jnp.dot(vtj, p, preferred_element_type=F32)
        return tuple(new_stats)

    def q_tile(qi, carry):
        rows = pl.ds(pl.multiple_of(qi * tq, tq), tq)
        qt = (q_ref[rows, :] * (DA_HEAD ** -0.5)).T
        q2t = jnp.concatenate([jnp.where(sub < DA_HEAD, qt, 0.0), jnp.where(sub >= DA_HEAD, qt, 0.0)],
                              axis=1).astype(BF16)
        acc_ref[...] = jnp.zeros(acc_ref.shape, F32)

        def body(j, state):
            sts, stats = state
            nxt = scores(j + 1, q2t)
            return nxt, update(j, sts, stats, False)

        init = tuple((jnp.full((1, cw), -jnp.inf, F32), jnp.zeros((1, cw), F32)) for _ in range(nch))
        sts, stats = lax.fori_loop(0, qi, body, (scores(0, q2t), init))
        stats = update(qi, sts, stats, True)
        l = jnp.concatenate([ml[1] for ml in stats], axis=1)
        ot = acc_ref[...] / l
        odt = ot[:, :tq] - lam * ot[:, tq:]
        yt = odt * lax.rsqrt(jnp.mean(odt * odt, axis=0, keepdims=True) + EPS) * subg * (1.0 - lam_init)
        o_ref[rows, :] = yt.T
        return carry

    lax.fori_loop(0, nq, q_tile, 0)


def _da_prompt(proj, lam_params, sub_g, bsz, seq, lam_init, tq=256, cw=128):
    nq = seq // tq
    hw = 2 * DA_HEAD
    small = pl.BlockSpec((1, DA_HEAD), lambda b, h: (0, 0))
    col = lambda off: pl.BlockSpec((seq, hw), lambda b, h: (b, off // hw + h))
    return pl.pallas_call(
        functools.partial(_da_prompt_kernel, lam_init=lam_init, tq=tq, cw=cw),
        grid=(bsz, DA_HEADS),
        in_specs=[small, small, small, small, pl.BlockSpec((hw, 1), lambda b, h: (0, 0)),
                  col(P_Q), col(P_K), col(P_V)],
        out_specs=pl.BlockSpec((seq, hw), lambda b, h: (b, h)),
        out_shape=jax.ShapeDtypeStruct((bsz * seq, BR_W), F32),
        scratch_shapes=[pltpu.VMEM((seq, hw), BF16), pltpu.VMEM((nq, hw, tq), BF16), pltpu.VMEM((hw, 2 * tq), F32)],
        compiler_params=_cparams("parallel", "parallel"),
        name="da_prompt")(*lam_params, sub_g.reshape(hw, 1), proj, proj, proj)


def _da_decode_kernel(pt_ref, lq1, lk1, lq2, lk2, subg_ref, q_ref, kn_ref, vn_ref, *rest, lam_init, n_pages):
    del pt_ref
    kt_refs = rest[:n_pages]
    v_refs = rest[n_pages:2 * n_pages]
    o_ref = rest[2 * n_pages]
    nh = DA_HEADS
    hw = 2 * DA_HEAD
    lam = _lam(lq1, lk1, lq2, lk2, lam_init)
    q = q_ref[...] * (DA_HEAD ** -0.5)
    r = _iota2((2 * nh, BR_W), 0)
    seg = _iota2((2 * nh, BR_W), 1) >> 6
    q_bd = jnp.where(((seg & 1) == (r >> 2)) & ((seg >> 1) == (r & 3)), jnp.broadcast_to(q, (2 * nh, BR_W)), 0.0)
    s_new = jnp.sum(q_bd * kn_ref[...], axis=1, keepdims=True)
    qb = q_bd.astype(BF16)
    s = jnp.concatenate([jnp.dot(qb, kt_refs[j][...].astype(BF16), preferred_element_type=F32)
                         for j in range(n_pages)], axis=1)
    m = jnp.maximum(jnp.max(s, axis=1, keepdims=True), s_new)
    e = jnp.exp(s - m)
    e_new = jnp.exp(s_new - m)
    l = jnp.sum(e, axis=1, keepdims=True) + e_new
    coef = jnp.where(_iota2((2 * nh, 1), 0) < nh, 1.0, -lam) / l
    w = e * coef
    w_new = e_new * coef
    p = (w + pltpu.roll(w, nh, axis=0)).astype(BF16)
    p_new = w_new + pltpu.roll(w_new, nh, axis=0)
    g = subg_ref[...]
    vn = vn_ref[...]
    outs = []
    for h in range(nh):
        v_h = jnp.concatenate([v_refs[j][pl.ds(h, PAGE_SIZE, stride=nh), :].astype(BF16) for j in range(n_pages)],
                              axis=0)
        o_h = (jnp.dot(p, v_h, preferred_element_type=F32)[h:h + 1, :]
               + p_new[h:h + 1, :] * vn[:, h * hw:(h + 1) * hw])
        outs.append(_rms(o_h, g) * (1.0 - lam_init))
    o_ref[...] = jnp.concatenate(outs, axis=1)


def _da_decode(proj3, lam_params, sub_g, cache_kt, cache_v, page_table, layer, lam_init):
    db = proj3.shape[0]
    n_pages = page_table.shape[1]
    hw = 2 * DA_HEAD
    small = pl.BlockSpec((1, DA_HEAD), lambda b, pt: (0, 0))
    row = lambda off: pl.BlockSpec((None, 1, BR_W), lambda b, pt: (b, 0, off // BR_W))

    def page(j):
        return pl.BlockSpec((None, None, BR_W, PAGE_SIZE), lambda b, pt: (layer, pt[b * n_pages + j], 0, 0))

    grid_spec = pltpu.PrefetchScalarGridSpec(
        num_scalar_prefetch=1,
        grid=(db,),
        in_specs=[small, small, small, small, pl.BlockSpec((1, hw), lambda b, pt: (0, 0)),
                  row(P_Q), row(P_K), row(P_V)]
                 + [page(j) for j in range(n_pages)] + [page(j) for j in range(n_pages)],
        out_specs=pl.BlockSpec((None, 1, BR_W), lambda b, pt: (b, 0, 0)))
    out = pl.pallas_call(
        functools.partial(_da_decode_kernel, lam_init=lam_init, n_pages=n_pages),
        grid_spec=grid_spec,
        out_shape=jax.ShapeDtypeStruct((db, 1, BR_W), F32),
        compiler_params=_cparams("parallel"),
        name="da_decode")(page_table.reshape(-1), *lam_params, sub_g.reshape(1, hw), proj3, proj3, proj3,
                          *([cache_kt] * n_pages), *([cache_v] * n_pages))
    return out.reshape(db, BR_W)


def _conv_window(win_ref, x_ref, cw, c, zi):
    @pl.when(zi == 0)
    def _():
        win_ref[0:8, :] = jnp.zeros((8, win_ref.shape[1]), F32)

    @pl.when(zi > 0)
    def _():
        win_ref[0:8, :] = win_ref[c:c + 8, :]

    win_ref[8:8 + c, :] = x_ref[...]
    y = win_ref[5:5 + c, :] * cw[0:1, :]
    for j in range(1, CONV_W):
        y = y + win_ref[5 + j:5 + j + c, :] * cw[j:j + 1, :]
    return y


def _ssd_prompt_kernel(z_ref, xbc_ref, sm_ref, cw_ref, cb_ref, bias_ref, alog_ref, dsk_ref, ng_ref,
                       o_ref, st_ref, win_ref, yd_ref, yo_ref, *, c):
    zi = pl.program_id(1)

    @pl.when(zi == 0)
    def _():
        st_ref[...] = jnp.zeros(st_ref.shape, F32)

    xbc = _silu(_conv_window(win_ref, xbc_ref, cw_ref[...], c, zi) + cb_ref[...])
    xs = xbc[:, :BR_W]
    gs = SSM_GROUPS * SSM_STATE
    bm = xbc[:, BR_W:BR_W + gs]
    cm = xbc[:, BR_W + gs:]
    lane = _iota2((1, LANE), 1)
    head_lane = lane < SSM_HEADS
    dt = _softplus(sm_ref[:, :LANE] + bias_ref[:, :LANE])
    a = jnp.where(head_lane, -jnp.exp(alog_ref[:, :LANE]), 0.0)
    dt = jnp.where(head_lane, dt, 0.0)
    acs = _dot_f32(_tril_f32(c), dt * a)
    acs_t = acs.T
    acs_last = acs[c - 1:c, :]
    expander = _head_expander(SSM_HEADS, SSM_HEAD).astype(F32)
    xdt = xs * _dot_f32(dt, expander)
    w_t = (xdt * _dot_f32(jnp.exp(acs_last - acs), expander)).T
    causal = _iota2((c, c), 0) >= _iota2((c, c), 1)
    rep = SSM_HEADS // SSM_GROUPS
    heads = range(SSM_HEADS)
    sl = [slice(h * SSM_HEAD, (h + 1) * SSM_HEAD) for h in heads]
    bm_g = [bm[:, g * SSM_STATE:(g + 1) * SSM_STATE].astype(BF16) for g in range(SSM_GROUPS)]
    cm_g = [cm[:, g * SSM_STATE:(g + 1) * SSM_STATE].astype(BF16) for g in range(SSM_GROUPS)]
    cb = [_dot_nt(cm_g[g], bm_g[g]) for g in range(SSM_GROUPS)]
    h_prev = [st_ref[h] for h in heads]
    y_off = [_dot_nt(cm_g[h // rep], h_prev[h]) for h in heads]
    st_add = [_dot(w_t[sl[h], :], bm_g[h // rep]) for h in heads]
    decay = [jnp.exp(jnp.where(causal, acs[:, h:h + 1] - acs_t[h:h + 1, :], -jnp.inf)) for h in heads]
    y_diag = [_dot(cb[h // rep] * decay[h], xdt[:, sl[h]]) for h in heads]
    for h in heads:
        yd_ref[:, sl[h]] = y_diag[h]
        yo_ref[:, sl[h]] = y_off[h]
        st_ref[h] = h_prev[h] * jnp.exp(acs_last[:, h:h + 1]) + st_add[h]
    y = yd_ref[...] + yo_ref[...] * _dot_f32(jnp.exp(acs), expander) + dsk_ref[...] * xs
    y = y * _silu(z_ref[...])
    gw = BR_W // SSM_GROUPS
    ng = ng_ref[...]
    o_ref[...] = jnp.concatenate([_rms(y[:, g * gw:(g + 1) * gw], ng[:, g * gw:(g + 1) * gw])
                                  for g in range(SSM_GROUPS)], axis=1)


def _full(shape):
    return pl.BlockSpec(shape, lambda *a: (0,) * len(shape))


def _ssd_prompt(proj, cw, cb, bias_row, alog_row, dsk, ng, bsz, seq):
    c = SSM_CHUNK
    nc = seq // c
    return pl.pallas_call(
        functools.partial(_ssd_prompt_kernel, c=c),
        grid=(bsz, nc),
        in_specs=[pl.BlockSpec((c, BR_W), lambda b, z: (b * nc + z, P_SZ // BR_W)),
                  pl.BlockSpec((c, SSM_CONV_CH), lambda b, z: (b * nc + z, P_XBC // SSM_CONV_CH)),
                  pl.BlockSpec((c, SMALL_W), lambda b, z: (b * nc + z, P_SMALL // SMALL_W)),
                  _full((CONV_W, SSM_CONV_CH)), _full((1, SSM_CONV_CH)), _full((1, SMALL_W)), _full((1, SMALL_W)),
                  _full((1, BR_W)), _full((1, BR_W))],
        out_specs=[pl.BlockSpec((c, BR_W), lambda b, z: (b * nc + z, 0)),
                   pl.BlockSpec((None, SSM_HEADS, SSM_HEAD, SSM_STATE), lambda b, z: (b, 0, 0, 0))],
        out_shape=[jax.ShapeDtypeStruct((bsz * seq, BR_W), F32),
                   jax.ShapeDtypeStruct((bsz, SSM_HEADS, SSM_HEAD, SSM_STATE), F32)],
        scratch_shapes=[pltpu.VMEM((c + 8, SSM_CONV_CH), F32), pltpu.VMEM((c, BR_W), F32), pltpu.VMEM((c, BR_W), F32)],
        compiler_params=_cparams("parallel", "arbitrary"),
        name="ssd_prompt")(proj, proj, proj, cw, cb, bias_row, alog_row, dsk, ng)


def _inv_unit_lower_minus_eye(mats, n, c):
    row = _iota2((n, n), 0)
    col = _iota2((n, n), 1)
    sh = 4
    ps = [jnp.where((row >> sh) == (col >> sh), a, 0.0) for a in mats]
    ys = [-p for p in ps]
    for _ in range(sh - 1):
        ps = [_dot(p, p) for p in ps]
        yp = [_dot(y, p) for y, p in zip(ys, ps)]
        ys = [y + p + t for y, p, t in zip(ys, ps, yp)]
    while (1 << sh) < c:
        mask = ((row >> (sh + 1)) == (col >> (sh + 1))) & ((row >> sh) != (col >> sh))
        offs = [jnp.where(mask, a, 0.0) for a in mats]
        ts = [off + _dot(y, off) for y, off in zip(ys, offs)]
        ty = [_dot(t, y) for t, y in zip(ts, ys)]
        ys = [y - (t + u) for y, t, u in zip(ys, ts, ty)]
        sh += 1
    return ys


def _gdn_prompt_kernel(qkv_ref, z_ref, sm_ref, cw_ref, bias_ref, alog_ref, ng_ref, o_ref, st_ref, win_ref, *, c, nsub):
    zi = pl.program_id(1)
    hd = GDN_HEAD
    heads = range(GDN_HEADS)
    rows = c * nsub
    units = [(s, h) for s in range(nsub) for h in heads]
    rs = lambda s: slice(s * c, (s + 1) * c)

    @pl.when(zi == 0)
    def _():
        st_ref[...] = jnp.zeros(st_ref.shape, F32)

    qkv = _silu(_conv_window(win_ref, qkv_ref, cw_ref[...], rows, zi))
    pre = sm_ref[:, :LANE] + bias_ref[:, :LANE]
    g_all = -jnp.exp(alog_ref[:, :LANE]) * _softplus(pre)
    beta_all = _sigmoid(sm_ref[:, :LANE])
    tril = _tril_f32(c)
    gc = [_dot_f32(tril, g_all[rs(s), :]) for s in range(nsub)]
    gc_t = [g.T for g in gc]
    row = _iota2((c, c), 0)
    col = _iota2((c, c), 1)
    l2 = lambda t: t * lax.rsqrt(jnp.sum(t * t, axis=-1, keepdims=True) + EPS)

    q = {(s, h): l2(qkv[rs(s), h * hd:(h + 1) * hd]) * (hd ** -0.5) for s, h in units}
    k = {(s, h): l2(qkv[rs(s), BR_W + h * hd:BR_W + (h + 1) * hd]) for s, h in units}
    v = {(s, h): qkv[rs(s), 2 * BR_W + h * hd:2 * BR_W + (h + 1) * hd] for s, h in units}
    g_col = {(s, h): gc[s][:, S_GA + h:S_GA + h + 1] for s, h in units}
    g_last = {(s, h): gc[s][c - 1:c, S_GA + h:S_GA + h + 1] for s, h in units}
    beta = {(s, h): beta_all[rs(s), S_GB + h:S_GB + h + 1] for s, h in units}
    decay = {(s, h): jnp.exp(jnp.where(row >= col, g_col[s, h] - gc_t[s][S_GA + h:S_GA + h + 1, :], -jnp.inf))
             for s, h in units}
    kb = {u: k[u] * beta[u] for u in units}
    eg = {u: jnp.exp(g_col[u]) for u in units}
    a_low = [jnp.where(row > col, _dot_nt(kb[u], k[u]) * decay[u], 0.0) for u in units]
    attn = {u: _dot_nt(q[u], k[u]) * decay[u] for u in units}
    t_dev = dict(zip(units, _inv_unit_lower_minus_eye(a_low, c, c)))
    vb = {u: v[u] * beta[u] for u in units}
    kbe = {u: kb[u] * eg[u] for u in units}
    u_mat = {u: vb[u] + _dot(t_dev[u], vb[u]) for u in units}
    w_mat = {u: kbe[u] + _dot(t_dev[u], kbe[u]) for u in units}
    kg_t = {u: (k[u] * jnp.exp(g_last[u] - g_col[u])).T for u in units}
    ng = ng_ref[...]
    state = [st_ref[h] for h in heads]
    for s in range(nsub):
        v_new = [u_mat[s, h] - _dot(w_mat[s, h], state[h]) for h in heads]
        o = [_dot(q[s, h] * eg[s, h], state[h]) + _dot(attn[s, h], v_new[h]) for h in heads]
        state = [state[h] * jnp.exp(g_last[s, h]) + _dot(kg_t[s, h], v_new[h]) for h in heads]
        for h in heads:
            o_ref[rs(s), h * hd:(h + 1) * hd] = _rms(o[h], ng) * _silu(z_ref[rs(s), h * hd:(h + 1) * hd])
    for h in heads:
        st_ref[h] = state[h]


GDN_KERNEL_CHUNK = 128
GDN_KERNEL_NSUB = 2


def _gdn_prompt(proj, cw, bias_row, alog_row, ng, bsz, seq):
    rows = GDN_KERNEL_CHUNK * GDN_KERNEL_NSUB
    nc = seq // rows
    return pl.pallas_call(
        functools.partial(_gdn_prompt_kernel, c=GDN_KERNEL_CHUNK, nsub=GDN_KERNEL_NSUB),
        grid=(bsz, nc),
        in_specs=[pl.BlockSpec((rows, GDN_CONV_CH), lambda b, z: (b * nc + z, P_GQKV // GDN_CONV_CH)),
                  pl.BlockSpec((rows, BR_W), lambda b, z: (b * nc + z, P_GZ // BR_W)),
                  pl.BlockSpec((rows, SMALL_W), lambda b, z: (b * nc + z, P_SMALL // SMALL_W)),
                  _full((CONV_W, GDN_CONV_CH)), _full((1, SMALL_W)), _full((1, SMALL_W)), _full((1, GDN_HEAD))],
        out_specs=[pl.BlockSpec((rows, BR_W), lambda b, z: (b * nc + z, 0)),
                   pl.BlockSpec((None, GDN_HEADS, GDN_HEAD, GDN_HEAD), lambda b, z: (b, 0, 0, 0))],
        out_shape=[jax.ShapeDtypeStruct((bsz * seq, BR_W), F32),
                   jax.ShapeDtypeStruct((bsz, GDN_HEADS, GDN_HEAD, GDN_HEAD), F32)],
        scratch_shapes=[pltpu.VMEM((rows + 8, GDN_CONV_CH), F32)],
        compiler_params=_cparams("parallel", "arbitrary"),
        name="gdn_prompt")(proj, proj, proj, cw, bias_row, alog_row, ng)


def _mlstm_prompt_kernel(q_ref, k_ref, v_ref, og_ref, sm_ref, bias_ref, ng_ref, o_ref, c_ref, n_ref, m_ref,
                         *, c, nsub):
    zi = pl.program_id(1)

    @pl.when(zi == 0)
    def _():
        c_ref[...] = jnp.zeros(c_ref.shape, F32)
        n_ref[...] = jnp.zeros(n_ref.shape, F32)
        m_ref[...] = jnp.zeros(m_ref.shape, F32)

    pre = sm_ref[:, :LANE] + bias_ref[:, :LANE]
    logf = -_softplus(-pre)
    rs = lambda s: slice(s * c, (s + 1) * c)
    tril = _tril_f32(c)
    bcum = [_dot_f32(tril, logf[rs(s), :]) for s in range(nsub)]
    bcum_t = [b.T for b in bcum]
    pre_t = [pre[rs(s), :].T for s in range(nsub)]
    causal = _iota2((c, c), 0) >= _iota2((c, c), 1)
    ng = ng_ref[...]
    hd = ML_HEAD
    heads = range(ML_HEADS)
    units = [(s, h) for s in range(nsub) for h in heads]
    sl = [slice(h * hd, (h + 1) * hd) for h in heads]
    q = {(s, h): q_ref[rs(s), sl[h]] for s, h in units}
    k = {(s, h): k_ref[rs(s), sl[h]] * (hd ** -0.5) for s, h in units}
    v = {(s, h): v_ref[rs(s), sl[h]] for s, h in units}
    qk = {u: _dot_nt(q[u], k[u]) for u in units}
    b_col = {(s, h): bcum[s][:, S_MF + h:S_MF + h + 1] for s, h in units}
    b_last = {u: b_col[u][c - 1:c, :] for u in units}
    i_col = {(s, h): pre[rs(s), S_MI + h:S_MI + h + 1] for s, h in units}
    dmat = {(s, h): jnp.where(causal, b_col[s, h] - bcum_t[s][S_MF + h:S_MF + h + 1, :]
                              + pre_t[s][S_MI + h:S_MI + h + 1, :], -jnp.inf) for s, h in units}
    dmax = {u: jnp.max(dmat[u], axis=1, keepdims=True) for u in units}
    c_prev = [c_ref[h] for h in heads]
    n_prev = [n_ref[h:h + 1, :] for h in heads]
    m_prev = [m_ref[h:h + 1, 0:1] for h in heads]
    for s in range(nsub):
        qc = [_dot(q[s, h], c_prev[h]) for h in heads]
        m_t = [jnp.maximum(b_col[s, h] + m_prev[h], dmax[s, h]) for h in heads]
        m_new = [m_t[h][c - 1:c, :] for h in heads]
        smat = [qk[s, h] * jnp.exp(dmat[s, h] - m_t[h]) for h in heads]
        kw = [k[s, h] * jnp.exp(b_last[s, h] - b_col[s, h] + i_col[s, h] - m_new[h]) for h in heads]
        sv = [_dot(smat[h], v[s, h]) for h in heads]
        kv = [_dot(kw[h].T, v[s, h]) for h in heads]
        for h in heads:
            w_prev = jnp.exp(b_col[s, h] + m_prev[h] - m_t[h])
            num = w_prev * qc[h] + sv[h]
            den = (w_prev * jnp.sum(q[s, h] * n_prev[h], axis=1, keepdims=True)
                   + jnp.sum(smat[h], axis=1, keepdims=True))
            hid = num / jnp.maximum(jnp.abs(den), jnp.exp(-m_t[h]))
            o_ref[rs(s), sl[h]] = _rms(hid, ng) * _sigmoid(og_ref[rs(s), sl[h]])
        w_c = [jnp.exp(b_last[s, h] + m_prev[h] - m_new[h]) for h in heads]
        c_prev = [c_prev[h] * w_c[h] + kv[h] for h in heads]
        n_prev = [n_prev[h] * w_c[h] + jnp.sum(kw[h], axis=0, keepdims=True) for h in heads]
        m_prev = m_new
    for h in heads:
        c_ref[h] = c_prev[h]
        n_ref[h:h + 1, :] = n_prev[h]
        m_ref[h:h + 1, :] = jnp.broadcast_to(m_prev[h], (1, LANE))


ML_KERNEL_NSUB = 2


def _mlstm_prompt(proj, bias_row, ng, bsz, seq):
    nsub = ML_KERNEL_NSUB
    c = ML_CHUNK * nsub
    nc = seq // c
    col = lambda i: pl.BlockSpec((c, BR_W), lambda b, z: (b * nc + z, P_MQ // BR_W + i))
    return pl.pallas_call(
        functools.partial(_mlstm_prompt_kernel, c=ML_CHUNK, nsub=nsub),
        grid=(bsz, nc),
        in_specs=[col(0), col(1), col(2), col(3),
                  pl.BlockSpec((c, SMALL_W), lambda b, z: (b * nc + z, P_SMALL // SMALL_W)),
                  _full((1, SMALL_W)), _full((1, ML_HEAD))],
        out_specs=[pl.BlockSpec((c, BR_W), lambda b, z: (b * nc + z, 0)),
                   pl.BlockSpec((None, ML_HEADS, ML_HEAD, ML_HEAD), lambda b, z: (b, 0, 0, 0)),
                   pl.BlockSpec((None, ML_HEADS, ML_HEAD), lambda b, z: (b, 0, 0)),
                   pl.BlockSpec((None, 8, LANE), lambda b, z: (b, 0, 0))],
        out_shape=[jax.ShapeDtypeStruct((bsz * seq, BR_W), F32),
                   jax.ShapeDtypeStruct((bsz, ML_HEADS, ML_HEAD, ML_HEAD), F32),
                   jax.ShapeDtypeStruct((bsz, ML_HEADS, ML_HEAD), F32),
                   jax.ShapeDtypeStruct((bsz, 8, LANE), F32)],
        compiler_params=_cparams("parallel", "arbitrary"),
        name="mlstm_prompt")(proj, proj, proj, proj, proj, bias_row, ng)


STEP_ROWS = 8


def _split_f32(x):
    hi = x.astype(BF16).astype(F32)
    return hi, (x - hi).astype(BF16).astype(F32)


def _outer_lhs(rows):
    hi, lo = _split_f32(rows)
    return jnp.concatenate([hi, hi, lo, jnp.zeros_like(hi)], axis=0).T.astype(BF16)


def _outer_rhs(rows, r):
    keep = _iota2(rows.shape, 0) == r
    hi, lo = _split_f32(jnp.where(keep, rows, 0.0))
    return jnp.concatenate([hi, lo, hi, jnp.zeros_like(hi)], axis=0).astype(BF16)


def _pick_rows(results):
    rowi = _iota2(results[0].shape, 0)
    out = results[0]
    for r in range(1, len(results)):
        out = jnp.where(rowi == r, results[r], out)
    return out


def _step_conv(x_ref, buf_ref, cw_ref, nb_ref):
    cw = cw_ref[...]
    x, b0, b1, b2 = x_ref[...], buf_ref[0], buf_ref[1], buf_ref[2]
    nb_ref[0] = b1
    nb_ref[1] = b2
    nb_ref[2] = x
    return b0 * cw[0:1] + b1 * cw[1:2] + b2 * cw[2:3] + x * cw[3:4]


def _ssd_step_kernel(z_ref, xbc_ref, sm_ref, buf_ref, st_ref, cw_ref, cb_ref, bias_ref, alog_ref, dsk_ref, ng_ref,
                     o_ref, nb_ref, nst_ref):
    rb = STEP_ROWS
    xbc = _silu(_step_conv(xbc_ref, buf_ref, cw_ref, nb_ref) + cb_ref[...])
    xs = xbc[:, :BR_W]
    gs = SSM_GROUPS * SSM_STATE
    bm = xbc[:, BR_W:BR_W + gs]
    cm = xbc[:, BR_W + gs:]
    dt = _softplus(sm_ref[:, :LANE] + bias_ref[:, :LANE])
    d_a = jnp.exp(-jnp.exp(alog_ref[:, :LANE]) * dt)
    lane = _iota2((rb, 2 * SSM_HEAD), 1)
    sub = _iota2((2 * SSM_HEAD, 1), 0)
    pairs_per_group = SSM_HEADS // SSM_GROUPS // 2
    ys = []
    for j in range(SSM_HEADS // 2):
        g = j // pairs_per_group
        h0, h1 = 2 * j, 2 * j + 1
        dt_pair = jnp.where(lane < SSM_HEAD, dt[:, h0:h0 + 1], dt[:, h1:h1 + 1])
        xt = _outer_lhs(xs[:, j * 2 * SSM_HEAD:(j + 1) * 2 * SSM_HEAD] * dt_pair)
        bm_g = bm[:, g * SSM_STATE:(g + 1) * SSM_STATE]
        cm_t = cm[:, g * SSM_STATE:(g + 1) * SSM_STATE].T.astype(BF16)
        coli = _iota2((2 * SSM_HEAD, rb), 1)
        y_t = jnp.zeros((2 * SSM_HEAD, rb), F32)
        for r in range(rb):
            decay = jnp.where(sub < SSM_HEAD, d_a[r:r + 1, h0:h0 + 1], d_a[r:r + 1, h1:h1 + 1])
            h_new = st_ref[r, j] * decay + jnp.dot(xt, _outer_rhs(bm_g, r), preferred_element_type=F32)
            nst_ref[r, j] = h_new
            y_t = jnp.where(coli == r, jnp.dot(h_new.astype(BF16), cm_t, preferred_element_type=F32), y_t)
        ys.append(y_t.T)
    y = (jnp.concatenate(ys, axis=1) + dsk_ref[...] * xs) * _silu(z_ref[...])
    gw = BR_W // SSM_GROUPS
    ng = ng_ref[...]
    o_ref[...] = jnp.concatenate([_rms(y[:, g * gw:(g + 1) * gw], ng[:, g * gw:(g + 1) * gw])
                                  for g in range(SSM_GROUPS)], axis=1)


def _ssd_step(proj, buf, st, layer, cw, cb, bias_row, alog_row, dsk, ng):
    db = proj.shape[0]
    rb = STEP_ROWS
    npair = SSM_HEADS // 2
    sdim = 2 * SSM_HEAD
    return pl.pallas_call(
        _ssd_step_kernel,
        grid=(db // rb,),
        in_specs=[pl.BlockSpec((rb, BR_W), lambda i: (i, P_SZ // BR_W)),
                  pl.BlockSpec((rb, SSM_CONV_CH), lambda i: (i, P_XBC // SSM_CONV_CH)),
                  pl.BlockSpec((rb, SMALL_W), lambda i: (i, P_SMALL // SMALL_W)),
                  pl.BlockSpec((None, CONV_W - 1, rb, SSM_CONV_CH), lambda i: (layer, 0, i, 0)),
                  pl.BlockSpec((None, rb, npair, sdim, SSM_STATE), lambda i: (layer, i, 0, 0, 0)),
                  _full((CONV_W, SSM_CONV_CH)), _full((1, SSM_CONV_CH)), _full((1, SMALL_W)), _full((1, SMALL_W)),
                  _full((1, BR_W)), _full((1, BR_W))],
        out_specs=[pl.BlockSpec((rb, BR_W), lambda i: (i, 0)),
                   pl.BlockSpec((CONV_W - 1, rb, SSM_CONV_CH), lambda i: (0, i, 0)),
                   pl.BlockSpec((rb, npair, sdim, SSM_STATE), lambda i: (i, 0, 0, 0))],
        out_shape=[jax.ShapeDtypeStruct((db, BR_W), F32),
                   jax.ShapeDtypeStruct((CONV_W - 1, db, SSM_CONV_CH), F32),
                   jax.ShapeDtypeStruct((db, npair, sdim, SSM_STATE), F32)],
        compiler_params=_cparams("parallel"),
        name="ssd_step")(proj, proj, proj, buf, st, cw, cb, bias_row, alog_row, dsk, ng)


def _gdn_step_kernel(qkv_ref, z_ref, sm_ref, buf_ref, st_ref, cw_ref, bias_ref, alog_ref, ng_ref,
                     o_ref, nb_ref, nst_ref):
    rb = STEP_ROWS
    hd = GDN_HEAD
    ng = ng_ref[...]
    qkv = _silu(_step_conv(qkv_ref, buf_ref, cw_ref, nb_ref))
    sm = sm_ref[:, :LANE]
    eg_all = jnp.exp(-jnp.exp(alog_ref[:, :LANE]) * _softplus(sm + bias_ref[:, :LANE]))
    beta_all = _sigmoid(sm)
    l2 = lambda t: t * lax.rsqrt(jnp.sum(t * t, axis=-1, keepdims=True) + EPS)
    for h in range(GDN_HEADS):
        q = l2(qkv[:, h * hd:(h + 1) * hd]) * (hd ** -0.5)
        k = l2(qkv[:, BR_W + h * hd:BR_W + (h + 1) * hd])
        v = qkv[:, 2 * BR_W + h * hd:2 * BR_W + (h + 1) * hd]
        eg = eg_all[:, S_GA + h:S_GA + h + 1]
        beta = beta_all[:, S_GB + h:S_GB + h + 1]
        lhs = jnp.concatenate([k * (beta * eg), q * eg], axis=0).astype(BF16)
        res = [jnp.dot(lhs, st_ref[r, h].astype(BF16), preferred_element_type=F32) for r in range(rb)]
        v_new = v * beta - _pick_rows([t[:rb] for t in res])
        o = _pick_rows([t[rb:] for t in res]) + jnp.sum(q * k, axis=1, keepdims=True) * v_new
        kt = _outer_lhs(k)
        for r in range(rb):
            nst_ref[r, h] = (st_ref[r, h] * eg[r:r + 1, :]
                             + jnp.dot(kt, _outer_rhs(v_new, r), preferred_element_type=F32))
        o_ref[:, h * hd:(h + 1) * hd] = _rms(o, ng) * _silu(z_ref[:, h * hd:(h + 1) * hd])


def _gdn_step(proj, buf, st, layer, cw, bias_row, alog_row, ng):
    db = proj.shape[0]
    rb = STEP_ROWS
    return pl.pallas_call(
        _gdn_step_kernel,
        grid=(db // rb,),
        in_specs=[pl.BlockSpec((rb, GDN_CONV_CH), lambda i: (i, P_GQKV // GDN_CONV_CH)),
                  pl.BlockSpec((rb, BR_W), lambda i: (i, P_GZ // BR_W)),
                  pl.BlockSpec((rb, SMALL_W), lambda i: (i, P_SMALL // SMALL_W)),
                  pl.BlockSpec((None, CONV_W - 1, rb, GDN_CONV_CH), lambda i: (layer, 0, i, 0)),
                  pl.BlockSpec((None, rb, GDN_HEADS, GDN_HEAD, GDN_HEAD), lambda i: (layer, i, 0, 0, 0)),
                  _full((CONV_W, GDN_CONV_CH)), _full((1, SMALL_W)), _full((1, SMALL_W)), _full((1, GDN_HEAD))],
        out_specs=[pl.BlockSpec((rb, BR_W), lambda i: (i, 0)),
                   pl.BlockSpec((CONV_W - 1, rb, GDN_CONV_CH), lambda i: (0, i, 0)),
                   pl.BlockSpec((rb, GDN_HEADS, GDN_HEAD, GDN_HEAD), lambda i: (i, 0, 0, 0))],
        out_shape=[jax.ShapeDtypeStruct((db, BR_W), F32),
                   jax.ShapeDtypeStruct((CONV_W - 1, db, GDN_CONV_CH), F32),
                   jax.ShapeDtypeStruct((db, GDN_HEADS, GDN_HEAD, GDN_HEAD), F32)],
        compiler_params=_cparams("parallel"),
        name="gdn_step")(proj, proj, proj, buf, st, cw, bias_row, alog_row, ng)


def _mlstm_step_kernel(q_ref, k_ref, v_ref, og_ref, sm_ref, c_ref, n_ref, m_ref, bias_ref, ng_ref,
                       o_ref, nc_ref, nn_ref, nm_ref):
    rb = STEP_ROWS
    hd = ML_HEAD
    ng = ng_ref[...]
    pre = sm_ref[:, :LANE] + bias_ref[:, :LANE]
    logf_all = -_softplus(-pre)
    m_all = m_ref[...]
    lane4 = _iota2((rb, ML_HEADS), 1)
    m_out = jnp.zeros((rb, ML_HEADS), F32)
    for h in range(ML_HEADS):
        sl = slice(h * hd, (h + 1) * hd)
        q = q_ref[:, sl]
        k = k_ref[:, sl] * (hd ** -0.5)
        v = v_ref[:, sl]
        i_pre = pre[:, S_MI + h:S_MI + h + 1]
        logf = logf_all[:, S_MF + h:S_MF + h + 1]
        m_prev = m_all[:, h:h + 1]
        m_t = jnp.maximum(logf + m_prev, i_pre)
        w_prev = jnp.exp(logf + m_prev - m_t)
        w_j = jnp.exp(i_pre - m_t)
        s = jnp.sum(q * k, axis=1, keepdims=True) * w_j
        n_prev = n_ref[pl.ds(h, rb, stride=ML_HEADS), :]
        qb = q.astype(BF16)
        qc = _pick_rows([jnp.dot(qb, c_ref[r, h].astype(BF16), preferred_element_type=F32) for r in range(rb)])
        num = w_prev * qc + s * v
        den = w_prev * jnp.sum(q * n_prev, axis=1, keepdims=True) + s
        hid = num / jnp.maximum(jnp.abs(den), jnp.exp(-m_t))
        kw = k * w_j
        kt = _outer_lhs(kw)
        for r in range(rb):
            nc_ref[r, h] = (c_ref[r, h] * w_prev[r:r + 1, :]
                            + jnp.dot(kt, _outer_rhs(v, r), preferred_element_type=F32))
        nn_ref[h] = n_prev * w_prev + kw
        m_out = jnp.where(lane4 == h, m_t, m_out)
        o_ref[:, sl] = _rms(hid, ng) * _sigmoid(og_ref[:, sl])
    nm_ref[...] = m_out


def _mlstm_step(proj, c0, n0, m0, layer, bias_row, ng):
    db = proj.shape[0]
    rb = STEP_ROWS
    col = lambda j: pl.BlockSpec((rb, BR_W), lambda i: (i, P_MQ // BR_W + j))
    return pl.pallas_call(
        _mlstm_step_kernel,
        grid=(db // rb,),
        in_specs=[col(0), col(1), col(2), col(3),
                  pl.BlockSpec((rb, SMALL_W), lambda i: (i, P_SMALL // SMALL_W)),
                  pl.BlockSpec((None, rb, ML_HEADS, ML_HEAD, ML_HEAD), lambda i: (layer, i, 0, 0, 0)),
                  pl.BlockSpec((None, rb * ML_HEADS, ML_HEAD), lambda i: (layer, i, 0)),
                  pl.BlockSpec((None, rb, ML_HEADS), lambda i: (layer, i, 0)),
                  _full((1, SMALL_W)), _full((1, ML_HEAD))],
        out_specs=[pl.BlockSpec((rb, BR_W), lambda i: (i, 0)),
                   pl.BlockSpec((rb, ML_HEADS, ML_HEAD, ML_HEAD), lambda i: (i, 0, 0, 0)),
                   pl.BlockSpec((ML_HEADS, rb, ML_HEAD), lambda i: (0, i, 0)),
                   pl.BlockSpec((rb, ML_HEADS), lambda i: (i, 0))],
        out_shape=[jax.ShapeDtypeStruct((db, BR_W), F32),
                   jax.ShapeDtypeStruct((db, ML_HEADS, ML_HEAD, ML_HEAD), F32),
                   jax.ShapeDtypeStruct((ML_HEADS, db, ML_HEAD), F32),
                   jax.ShapeDtypeStruct((db, ML_HEADS), F32)],
        compiler_params=_cparams("parallel"),
        name="mlstm_step")(proj, proj, proj, proj, proj, c0, n0, m0, bias_row, ng)


def _merge_kernel(x_ref, a_ref, b_ref, c_ref, d_ref, g0, g1, g2, g3, wb_ref, wo_ref, o_ref):
    acc = None
    for n, (br, gate) in enumerate(((a_ref, g0), (b_ref, g1), (c_ref, g2), (d_ref, g3))):
        t = _sigmoid(gate[...]) * jnp.dot(br[...].astype(BF16), wb_ref[n], preferred_element_type=F32)
        acc = t if acc is None else acc + t
    o_ref[...] = x_ref[...] + jnp.dot(acc.astype(BF16), wo_ref[...], preferred_element_type=F32)


def _merge(x, branches, proj, wb, wo, tm):
    m = x.shape[0]
    br = pl.BlockSpec((tm, BR_W), lambda i: (i, 0))
    gate = lambda n: pl.BlockSpec((tm, D_MODEL), lambda i: (i, P_GATE // D_MODEL + n))
    return pl.pallas_call(
        _merge_kernel,
        grid=(m // tm,),
        in_specs=[pl.BlockSpec((tm, D_MODEL), lambda i: (i, 0)), br, br, br, br,
                  gate(0), gate(1), gate(2), gate(3),
                  _full((N_BRANCH, BR_W, D_MODEL)), _full((D_MODEL, D_MODEL))],
        out_specs=pl.BlockSpec((tm, D_MODEL), lambda i: (i, 0)),
        out_shape=jax.ShapeDtypeStruct((m, D_MODEL), F32),
        compiler_params=_cparams("parallel"),
        name="merge")(x, *branches, proj, proj, proj, proj, wb, wo)


def _cross_prompt_kernel(x_ref, g_ref, mk_ref, mv_ref, wq_ref, wo_ref, o_ref):
    x = x_ref[...]
    h = _rms(x, g_ref[...]).astype(BF16)
    q = jnp.dot(h, wq_ref[...], preferred_element_type=F32) * (X_HEAD ** -0.5)
    outs = []
    for hd in range(X_HEADS):
        sl = slice(hd * X_HEAD, (hd + 1) * X_HEAD)
        s = _dot_nt(q[:, sl], mk_ref[:, sl])
        p = jnp.exp(s - jnp.max(s, axis=1, keepdims=True))
        p = p / jnp.sum(p, axis=1, keepdims=True)
        outs.append(_dot(p, mv_ref[:, sl]))
    o = jnp.concatenate(outs, axis=1).astype(BF16)
    o_ref[...] = x + jnp.dot(o, wo_ref[...], preferred_element_type=F32)


def _cross_prompt(x, g, mkv, wq, wo, bsz, seq, tq=512):
    nq = seq // tq
    d = D_MODEL
    return pl.pallas_call(
        _cross_prompt_kernel,
        grid=(bsz, nq),
        in_specs=[pl.BlockSpec((tq, d), lambda b, i: (b * nq + i, 0)), _full((1, d)),
                  pl.BlockSpec((N_MEM, d), lambda b, i: (b, 0)),
                  pl.BlockSpec((N_MEM, d), lambda b, i: (b, 1)),
                  _full((d, d)), _full((d, d))],
        out_specs=pl.BlockSpec((tq, d), lambda b, i: (b * nq + i, 0)),
        out_shape=jax.ShapeDtypeStruct((bsz * seq, d), F32),
        compiler_params=_cparams("parallel", "arbitrary"),
        name="cross_prompt")(x, g.reshape(1, d), mkv, mkv, wq, wo)


CROSS_ROWS = 4


def _cross_decode_kernel(q_ref, mk_ref, mv_ref, o_ref):
    halves = X_HEAD // LANE
    rows_per_tok = halves * X_HEADS

    def head_slab(ref, r, h):
        return jnp.concatenate([ref[r, pl.ds(t * X_HEADS + h, N_MEM, stride=rows_per_tok), :].astype(BF16)
                                for t in range(halves)], axis=1)

    pairs = [(r, h) for r in range(CROSS_ROWS) for h in range(X_HEADS)]
    qs = [q_ref[r] * (X_HEAD ** -0.5) for r in range(CROSS_ROWS)]
    scores = [lax.dot_general(jnp.broadcast_to(qs[r][:, h * X_HEAD:(h + 1) * X_HEAD], (8, X_HEAD)).astype(BF16),
                              head_slab(mk_ref, r, h), (((1,), (1,)), ((), ())), preferred_element_type=F32)
              for r, h in pairs]
    probs = []
    for s in scores:
        e = jnp.exp(s - jnp.max(s, axis=1, keepdims=True))
        probs.append((e / jnp.sum(e, axis=1, keepdims=True)).astype(BF16))
    outs = [jnp.dot(p, head_slab(mv_ref, r, h), preferred_element_type=F32)[0:1] for p, (r, h) in zip(probs, pairs)]
    for r in range(CROSS_ROWS):
        o_ref[r] = jnp.concatenate(outs[r * X_HEADS:(r + 1) * X_HEADS], axis=1)


def _mem_rows(mem):
    depth, db = mem.shape[:2]
    halves = X_HEAD // LANE
    m = mem.reshape(depth, db, N_MEM, X_HEADS, halves, LANE).transpose(0, 1, 2, 4, 3, 5)
    return m.reshape(depth, db, N_MEM * halves * X_HEADS, LANE)


def _cross_decode(q3, mem_k, mem_v, layer):
    db = q3.shape[0]
    rb = CROSS_ROWS
    d = D_MODEL
    mem = pl.BlockSpec((None, rb, mem_k.shape[2], LANE), lambda i: (layer, i, 0, 0))
    return pl.pallas_call(
        _cross_decode_kernel,
        grid=(db // rb,),
        in_specs=[pl.BlockSpec((rb, 1, d), lambda i: (i, 0, 0)), mem, mem],
        out_specs=pl.BlockSpec((rb, 1, d), lambda i: (i, 0, 0)),
        out_shape=jax.ShapeDtypeStruct((db, 1, d), F32),
        compiler_params=_cparams("parallel"),
        name="cross_decode")(q3, mem_k, mem_v)


def _swiglu_kernel(x_ref, g_ref, wg_ref, wu_ref, wd_ref, o_ref, h_ref, acc_ref):
    j = pl.program_id(1)

    @pl.when(j == 0)
    def _():
        h_ref[...] = _rms(x_ref[...], g_ref[...]).astype(BF16)
        acc_ref[...] = jnp.zeros(acc_ref.shape, F32)

    h = h_ref[...]
    gate = jnp.dot(h, wg_ref[...], preferred_element_type=F32)
    up = jnp.dot(h, wu_ref[...], preferred_element_type=F32)
    acc_ref[...] += jnp.dot((_silu(gate) * up).astype(BF16), wd_ref[...], preferred_element_type=F32)

    @pl.when(j == pl.num_programs(1) - 1)
    def _():
        o_ref[...] = x_ref[...] + acc_ref[...]


def _swiglu(x, g, wgu, wd, tm, tf=D_FF // 2):
    m, d = x.shape
    nf = D_FF // tf
    return pl.pallas_call(
        _swiglu_kernel,
        grid=(m // tm, nf),
        in_specs=[pl.BlockSpec((tm, d), lambda i, j: (i, 0)), _full((1, d)),
                  pl.BlockSpec((d, tf), lambda i, j: (0, j)),
                  pl.BlockSpec((d, tf), lambda i, j: (0, nf + j)),
                  pl.BlockSpec((tf, d), lambda i, j: (j, 0))],
        out_specs=pl.BlockSpec((tm, d), lambda i, j: (i, 0)),
        out_shape=jax.ShapeDtypeStruct((m, d), F32),
        scratch_shapes=[pltpu.VMEM((tm, d), BF16), pltpu.VMEM((tm, d), F32)],
        compiler_params=_cparams("parallel", "arbitrary"),
        name="swiglu")(x, g.reshape(1, d), wgu, wgu, wd)


def _pack_w_in(w):
    offs = np.cumsum((0,) + IN_SPLITS)
    seg = lambda i: w[:, offs[i]:offs[i + 1]]
    small = jnp.concatenate([seg(5), seg(8), seg(9), seg(14), seg(15),
                             jnp.zeros((w.shape[0], SMALL_W - 24), w.dtype)], axis=1)
    order = (0, 1, 2, 3, 4, 6, 7, 10, 11, 12, 13, 16)
    return jnp.concatenate([seg(i) for i in order] + [small], axis=1).astype(BF16)


def _small_row(parts):
    row = jnp.zeros((SMALL_W,), F32)
    for off, val in parts:
        row = lax.dynamic_update_slice(row, val.astype(F32), (off,))
    return row.reshape(1, SMALL_W)


def _layer_params(l, p):
    lp = dict(
        w_in=_pack_w_in(p["w_in"][l]),
        g_mix=p["g_mix"][l],
        lam=tuple(p[n][l].reshape(1, DA_HEAD) for n in ("da_lq1", "da_lk1", "da_lq2", "da_lk2")),
        lam_init=0.8 - 0.6 * math.exp(-0.3 * l),
        sub_g=p["da_sub_g"][l],
        bias_row=_small_row(((S_DT, p["ssm_dt_bias"][l]), (S_GA, p["gdn_dt_bias"][l]),
                             (S_MI, p["ml_i_bias"][l]), (S_MF, p["ml_f_bias"][l]))),
        alog_row=_small_row(((S_DT, p["ssm_a_log"][l]), (S_GA, p["gdn_a_log"][l]))),
        ssm_cw=p["ssm_conv_w"][l], ssm_cb=p["ssm_conv_b"][l].reshape(1, SSM_CONV_CH),
        ssm_dsk=jnp.repeat(p["ssm_d"][l], SSM_HEAD).reshape(1, BR_W),
        ssm_ng=p["ssm_norm_g"][l].reshape(1, BR_W),
        gdn_cw=p["gdn_conv_w"][l], gdn_ng=p["gdn_norm_g"][l].reshape(1, GDN_HEAD),
        ml_ng=p["ml_norm_g"][l].reshape(1, ML_HEAD),
        w_branch=p["w_branch"][l].astype(BF16), w_out=p["w_out"][l].astype(BF16),
        g_cross=p["g_cross"][l], w_cq=p["w_cq"][l].astype(BF16), w_co=p["w_co"][l].astype(BF16),
        g_ffn=p["g_ffn"][l], w_gu=p["w_gu"][l].astype(BF16), w_down=p["w_down"][l].astype(BF16),
    )
    return lp


def _prompt_layer(x, lp, mkv, bsz, seq):
    rows = bsz * seq
    proj = _norm_matmul(x, lp["g_mix"], lp["w_in"], tm=2048 if rows % 2048 == 0 else rows, tn=1152, name="in_proj")
    o_da = _da_prompt(proj, lp["lam"], lp["sub_g"], bsz, seq, lp["lam_init"])
    o_ssm, ssm = _ssd_prompt(proj, lp["ssm_cw"], lp["ssm_cb"], lp["bias_row"], lp["alog_row"], lp["ssm_dsk"],
                             lp["ssm_ng"], bsz, seq)
    o_gdn, gdn = _gdn_prompt(proj, lp["gdn_cw"], lp["bias_row"], lp["alog_row"], lp["gdn_ng"], bsz, seq)
    o_ml, ml_c, ml_n, ml_m = _mlstm_prompt(proj, lp["bias_row"], lp["ml_ng"], bsz, seq)
    x = _merge(x, (o_da, o_ssm, o_gdn, o_ml), proj, lp["w_branch"], lp["w_out"], tm=256)
    x = _cross_prompt(x, lp["g_cross"], mkv, lp["w_cq"], lp["w_co"], bsz, seq, tq=min(512, seq))
    x = _swiglu(x, lp["g_ffn"], lp["w_gu"], lp["w_down"], tm=1024 if rows % 1024 == 0 else 256)
    p3 = proj.reshape(bsz, seq, PACK_W)
    new = dict(
        k=p3[:, :, P_K:P_K + BR_W].reshape(bsz, seq, DA_HEADS, 2, DA_HEAD),
        v=p3[:, :, P_V:P_V + BR_W].reshape(bsz, seq, DA_HEADS, 2 * DA_HEAD),
        ssm_conv=p3[:, seq - (CONV_W - 1):, P_XBC:P_XBC + SSM_CONV_CH], ssm=ssm,
        gdn_conv=p3[:, seq - (CONV_W - 1):, P_GQKV:P_GQKV + GDN_CONV_CH], gdn=gdn,
        ml_c=ml_c, ml_n=ml_n, ml_m=ml_m[:, :ML_HEADS, 0])
    return x, new


def _sample_layer(x, lp, l, caches, states):
    db = x.shape[0]
    cache_k, cache_v, page_table, mem_k, mem_v = caches
    proj = _norm_matmul(x, lp["g_mix"], lp["w_in"], tm=db, tn=1152, name="in_proj_s")
    proj3 = proj.reshape(db, 1, PACK_W)
    o_da = _da_decode(proj3, lp["lam"], lp["sub_g"], cache_k, cache_v, page_table, l, lp["lam_init"])
    o_ssm, ssm_conv, ssm = _ssd_step(proj, states["ssm_conv"], states["ssm"], l, lp["ssm_cw"], lp["ssm_cb"],
                                     lp["bias_row"], lp["alog_row"], lp["ssm_dsk"], lp["ssm_ng"])
    o_gdn, gdn_conv, gdn = _gdn_step(proj, states["gdn_conv"], states["gdn"], l, lp["gdn_cw"],
                                     lp["bias_row"], lp["alog_row"], lp["gdn_ng"])
    o_ml, ml_c, ml_n, ml_m = _mlstm_step(proj, states["ml_c"], states["ml_n"], states["ml_m"], l,
                                         lp["bias_row"], lp["ml_ng"])
    x = _merge(x, (o_da, o_ssm, o_gdn, o_ml), proj, lp["w_branch"], lp["w_out"], tm=db)
    q = _norm_matmul(x, lp["g_cross"], lp["w_cq"], tm=db, tn=D_MODEL, name="cross_q_s")
    att = _cross_decode(q.reshape(db, 1, D_MODEL), mem_k, mem_v, l)
    x = _matmul_residual(x, att.reshape(db, D_MODEL), lp["w_co"], tm=db, name="cross_o_s")
    x = _swiglu(x, lp["g_ffn"], lp["w_gu"], lp["w_down"], tm=db)
    new = dict(
        k=proj[:, P_K:P_K + BR_W].reshape(db, 1, DA_HEADS, 2, DA_HEAD),
        v=proj[:, P_V:P_V + BR_W].reshape(db, 1, DA_HEADS, 2 * DA_HEAD),
        ssm_conv=ssm_conv.transpose(1, 0, 2), ssm=ssm.reshape(db, SSM_HEADS, SSM_HEAD, SSM_STATE),
        gdn_conv=gdn_conv.transpose(1, 0, 2), gdn=gdn,
        ml_c=ml_c, ml_n=ml_n.transpose(1, 0, 2), ml_m=ml_m)
    return x, new


_STATE_ORDER = ("ssm_conv", "ssm", "gdn_conv", "gdn", "ml_c", "ml_n", "ml_m")


def kernel(x_prompt, x_sample, cache_k, cache_v, cache_mem_k, cache_mem_v, state_ssm_conv, state_ssm, state_gdn_conv, state_gdn, state_mlstm_c, state_mlstm_n, state_mlstm_m, page_table, mem_prompt, g_mix, w_in, da_lq1, da_lk1, da_lq2, da_lk2, da_sub_g, ssm_conv_w, ssm_conv_b, ssm_dt_bias, ssm_a_log, ssm_d, ssm_norm_g, gdn_conv_w, gdn_dt_bias, gdn_a_log, gdn_norm_g, ml_i_bias, ml_f_bias, ml_norm_g, w_branch, w_out, g_cross, g_mem, w_cq, w_ckv, w_co, g_ffn, w_gu, w_down, g_final):
    p = dict(g_mix=g_mix, w_in=w_in, da_lq1=da_lq1, da_lk1=da_lk1, da_lq2=da_lq2, da_lk2=da_lk2,
             da_sub_g=da_sub_g, ssm_conv_w=ssm_conv_w, ssm_conv_b=ssm_conv_b, ssm_dt_bias=ssm_dt_bias,
             ssm_a_log=ssm_a_log, ssm_d=ssm_d, ssm_norm_g=ssm_norm_g, gdn_conv_w=gdn_conv_w,
             gdn_dt_bias=gdn_dt_bias, gdn_a_log=gdn_a_log, gdn_norm_g=gdn_norm_g,
             ml_i_bias=ml_i_bias, ml_f_bias=ml_f_bias, ml_norm_g=ml_norm_g,
             w_branch=w_branch, w_out=w_out, g_cross=g_cross, w_cq=w_cq, w_co=w_co,
             g_ffn=g_ffn, w_gu=w_gu, w_down=w_down)
    depth = w_in.shape[0]
    bsz, seq, d = x_prompt.shape
    db = x_sample.shape[0]
    n_mem = mem_prompt.shape[1]
    lps = [_layer_params(l, p) for l in range(depth)]

    mem2 = mem_prompt.reshape(bsz * n_mem, d)
    xp = x_prompt.reshape(bsz * seq, d)
    p_new = {n: [] for n in ("k", "v", "mem_k", "mem_v") + _STATE_ORDER}
    for l in range(depth):
        mkv = _norm_matmul(mem2, g_mem[l], w_ckv[l].astype(BF16), tm=min(1024, bsz * n_mem), tn=1024, name="mem_kv")
        xp, new = _prompt_layer(xp, lps[l], mkv, bsz, seq)
        mkv5 = mkv.reshape(bsz, n_mem, 2, X_HEADS, X_HEAD)
        new["mem_k"] = mkv5[:, :, 0]
        new["mem_v"] = mkv5[:, :, 1]
        for n in p_new:
            p_new[n].append(new[n])
    y_prompt = _final_norm(xp, g_final, tm=512 if (bsz * seq) % 512 == 0 else bsz * seq, name="final_norm").reshape(bsz, seq, d)

    n_pool = cache_k.shape[1]
    caches = (cache_k.transpose(0, 1, 3, 4, 5, 2).reshape(depth, n_pool, BR_W, PAGE_SIZE),
              cache_v.reshape(depth, n_pool, PAGE_SIZE * DA_HEADS, 2 * DA_HEAD),
              page_table, _mem_rows(cache_mem_k), _mem_rows(cache_mem_v))
    states = dict(ssm_conv=state_ssm_conv.transpose(0, 2, 1, 3),
                  ssm=state_ssm.reshape(depth, db, SSM_HEADS // 2, 2 * SSM_HEAD, SSM_STATE),
                  gdn_conv=state_gdn_conv.transpose(0, 2, 1, 3), gdn=state_gdn,
                  ml_c=state_mlstm_c, ml_n=state_mlstm_n.reshape(depth, db * ML_HEADS, ML_HEAD), ml_m=state_mlstm_m)
    xs = x_sample.reshape(db, d)
    s_new = {n: [] for n in ("k", "v") + _STATE_ORDER}
    for l in range(depth):
        xs, new = _sample_layer(xs, lps[l], l, caches, states)
        for n in s_new:
            s_new[n].append(new[n])
    y_sample = _final_norm(xs, g_final, tm=db, name="final_norm_s").reshape(db, 1, d)

    stk = lambda dct, n: jnp.stack(dct[n])
    return (y_prompt, y_sample,
            stk(p_new, "k"), stk(p_new, "v"), stk(p_new, "mem_k"), stk(p_new, "mem_v"),
            *(stk(p_new, n) for n in _STATE_ORDER),
            stk(s_new, "k"), stk(s_new, "v"), *(stk(s_new, n) for n in _STATE_ORDER))
```

```python
import functools
import math

import numpy as np
import jax
import jax.numpy as jnp
from jax import lax
from jax.experimental import pallas as pl
from jax.experimental.pallas import tpu as pltpu

F32 = jnp.float32
BF16 = jnp.bfloat16

D_MODEL = 1024
DEPTH = 4
PAGE_SIZE = 128
EPS = 1e-6
N_MEM = 256
CONV_W = 4
N_BRANCH = 4
BR_W = D_MODEL // 2
DA_HEADS = 4
DA_HEAD = BR_W // (2 * DA_HEADS)
SSM_HEAD = 64
SSM_HEADS = BR_W // SSM_HEAD
SSM_GROUPS = 2
SSM_STATE = 128
SSM_CONV_CH = BR_W + 2 * SSM_GROUPS * SSM_STATE
SSM_CHUNK = 128
GDN_HEADS = 4
GDN_HEAD = BR_W // GDN_HEADS
GDN_CONV_CH = 3 * BR_W
GDN_CHUNK = 64
ML_HEADS = 4
ML_HEAD = BR_W // ML_HEADS
ML_CHUNK = 128
X_HEADS = 4
X_HEAD = D_MODEL // X_HEADS
D_FF = -(-8 * D_MODEL // (3 * 256)) * 256

IN_SPLITS = (BR_W, BR_W, BR_W, BR_W, SSM_CONV_CH, SSM_HEADS, GDN_CONV_CH, BR_W, GDN_HEADS, GDN_HEADS,
             BR_W, BR_W, BR_W, BR_W, ML_HEADS, ML_HEADS, N_BRANCH * D_MODEL)

P_Q, P_K, P_V = 0, BR_W, 2 * BR_W
P_SZ = 3 * BR_W
P_XBC = P_SZ + BR_W
P_GQKV = P_XBC + SSM_CONV_CH
P_GZ = P_GQKV + GDN_CONV_CH
P_MQ = P_GZ + BR_W
P_GATE = P_MQ + 4 * BR_W
P_SMALL = P_GATE + N_BRANCH * D_MODEL
SMALL_W = 256
PACK_W = P_SMALL + SMALL_W
S_DT, S_GA, S_GB, S_MI, S_MF = 0, 8, 12, 16, 20

LANE = 128
VMEM_LIMIT = 56 * 1024 * 1024


def _cparams(*sem):
    return pltpu.CompilerParams(dimension_semantics=sem, vmem_limit_bytes=VMEM_LIMIT)


def _dot(a, b):
    return jnp.dot(a.astype(BF16), b.astype(BF16), preferred_element_type=F32)


def _dot_nt(a, b):
    return lax.dot_general(a.astype(BF16), b.astype(BF16), (((1,), (1,)), ((), ())), preferred_element_type=F32)


def _dot_f32(a, b):
    return jnp.dot(a, b, precision=lax.Precision.HIGHEST, preferred_element_type=F32)


def _sigmoid(x):
    return 1.0 / (1.0 + jnp.exp(-x))


def _silu(x):
    return x * _sigmoid(x)


def _softplus(x):
    return jnp.maximum(x, 0.0) + jnp.log(1.0 + jnp.exp(-jnp.abs(x)))


def _rms(x, g):
    return x * lax.rsqrt(jnp.mean(x * x, axis=-1, keepdims=True) + EPS) * g


def _iota2(shape, dim):
    return lax.broadcasted_iota(jnp.int32, shape, dim)


def _tril_f32(c):
    return (_iota2((c, c), 0) >= _iota2((c, c), 1)).astype(F32)


def _head_expander(n_heads, width):
    rows = _iota2((LANE, n_heads * width), 0)
    cols = _iota2((LANE, n_heads * width), 1)
    return (rows * width <= cols) & (cols < (rows + 1) * width)


def _lam(lq1, lk1, lq2, lk2, lam_init):
    return (jnp.exp(jnp.sum(lq1[...] * lk1[...], axis=1, keepdims=True))
            - jnp.exp(jnp.sum(lq2[...] * lk2[...], axis=1, keepdims=True)) + lam_init)


def _norm_matmul_kernel(x_ref, g_ref, w_ref, o_ref, h_ref):
    @pl.when(pl.program_id(1) == 0)
    def _():
        h_ref[...] = _rms(x_ref[...], g_ref[...]).astype(BF16)

    o_ref[...] = jnp.dot(h_ref[...], w_ref[...], preferred_element_type=F32)


def _norm_matmul(x, g, w, tm, tn, name):
    m, k = x.shape
    n = w.shape[1]
    return pl.pallas_call(
        _norm_matmul_kernel,
        grid=(m // tm, n // tn),
        in_specs=[pl.BlockSpec((tm, k), lambda i, j: (i, 0)),
                  pl.BlockSpec((1, k), lambda i, j: (0, 0)),
                  pl.BlockSpec((k, tn), lambda i, j: (0, j))],
        out_specs=pl.BlockSpec((tm, tn), lambda i, j: (i, j)),
        out_shape=jax.ShapeDtypeStruct((m, n), F32),
        scratch_shapes=[pltpu.VMEM((tm, k), BF16)],
        compiler_params=_cparams("parallel", "arbitrary"),
        name=name)(x, g.reshape(1, k), w)


def _matmul_res_kernel(x_ref, a_ref, w_ref, o_ref):
    o_ref[...] = x_ref[...] + jnp.dot(a_ref[...].astype(BF16), w_ref[...], preferred_element_type=F32)


def _matmul_residual(x, a, w, tm, name):
    m, n = x.shape
    k = a.shape[1]
    return pl.pallas_call(
        _matmul_res_kernel,
        grid=(m // tm,),
        in_specs=[pl.BlockSpec((tm, n), lambda i: (i, 0)),
                  pl.BlockSpec((tm, k), lambda i: (i, 0)),
                  pl.BlockSpec((k, n), lambda i: (0, 0))],
        out_specs=pl.BlockSpec((tm, n), lambda i: (i, 0)),
        out_shape=jax.ShapeDtypeStruct((m, n), F32),
        compiler_params=_cparams("parallel"),
        name=name)(x, a, w)


def _final_norm_kernel(x_ref, g_ref, o_ref):
    o_ref[...] = _rms(x_ref[...], g_ref[...])


def _final_norm(x, g, tm, name):
    m, n = x.shape
    return pl.pallas_call(
        _final_norm_kernel,
        grid=(m // tm,),
        in_specs=[pl.BlockSpec((tm, n), lambda i: (i, 0)), pl.BlockSpec((1, n), lambda i: (0, 0))],
        out_specs=pl.BlockSpec((tm, n), lambda i: (i, 0)),
        out_shape=jax.ShapeDtypeStruct((m, n), F32),
        compiler_params=_cparams("parallel"),
        name=name)(x, g.reshape(1, n))


def _da_prompt_kernel(lq1, lk1, lq2, lk2, subg_ref, q_ref, k_ref, v_ref, o_ref, kb_ref, vt_ref, acc_ref,
                      *, lam_init, tq, cw):
    hw = 2 * DA_HEAD
    nq = vt_ref.shape[0]
    kb_ref[...] = k_ref[...].astype(BF16)
    for t in range(nq):
        vt_ref[t] = v_ref[t * tq:(t + 1) * tq, :].T.astype(BF16)
    lam = _lam(lq1, lk1, lq2, lk2, lam_init)
    subg = subg_ref[...]
    sub = _iota2((hw, tq), 0)
    nch = 2 * tq // cw
    chunks = [slice(c * cw, (c + 1) * cw) for c in range(nch)]

    def scores(j, q2t):
        kj = kb_ref[pl.ds(pl.multiple_of(j * tq, tq), tq), :]
        return tuple(jnp.dot(kj, q2t[:, cols], preferred_element_type=F32) for cols in chunks)

    def update(j, sts, stats, masked):
        new_stats, scaled = [], []
        for c, st in enumerate(sts):
            m, l = stats[c]
            if masked:
                qpos = _iota2((tq, cw), 1) + (c * cw) % tq
                st = jnp.where(_iota2((tq, cw), 0) <= qpos, st, -jnp.inf)
            m_new = jnp.maximum(m, jnp.max(st, axis=0, keepdims=True))
            alpha = jnp.exp(m - m_new)
            p = jnp.exp(st - m_new)
            new_stats.append((m_new, alpha * l + jnp.sum(p, axis=0, keepdims=True)))
            scaled.append((alpha, p.astype(BF16)))
        vtj = vt_ref[j]
        for cols, (alpha, p) in zip(chunks, scaled):
            acc_ref[:, cols] = alpha * acc_ref[:, cols] + jnp.dot(vtj, p, preferred_element_type=F32)
        return tuple(new_stats)

    def q_tile(qi, carry):
        rows = pl.ds(pl.multiple_of(qi * tq, tq), tq)
        qt = (q_ref[rows, :] * (DA_HEAD ** -0.5)).T
        q2t = jnp.concatenate([jnp.where(sub < DA_HEAD, qt, 0.0), jnp.where(sub >= DA_HEAD, qt, 0.0)],
                              axis=1).astype(BF16)
        acc_ref[...] = jnp.zeros(acc_ref.shape, F32)

        def body(j, state):
            sts, stats = state
            nxt = scores(j + 1, q2t)
            return nxt, update(j, sts, stats, False)

        init = tuple((jnp.full((1, cw), -jnp.inf, F32), jnp.zeros((1, cw), F32)) for _ in range(nch))
        sts, stats = lax.fori_loop(0, qi, body, (scores(0, q2t), init))
        stats = update(qi, sts, stats, True)
        l = jnp.concatenate([ml[1] for ml in stats], axis=1)
        ot = acc_ref[...] / l
        odt = ot[:, :tq] - lam * ot[:, tq:]
        yt = odt * lax.rsqrt(jnp.mean(odt * odt, axis=0, keepdims=True) + EPS) * subg * (1.0 - lam_init)
        o_ref[rows, :] = yt.T
        return carry

    lax.fori_loop(0, nq, q_tile, 0)


def _da_prompt(proj, lam_params, sub_g, bsz, seq, lam_init, tq=256, cw=128):
    nq = seq // tq
    hw = 2 * DA_HEAD
    small = pl.BlockSpec((1, DA_HEAD), lambda b, h: (0, 0))
    col = lambda off: pl.BlockSpec((seq, hw), lambda b, h: (b, off // hw + h))
    return pl.pallas_call(
        functools.partial(_da_prompt_kernel, lam_init=lam_init, tq=tq, cw=cw),
        grid=(bsz, DA_HEADS),
        in_specs=[small, small, small, small, pl.BlockSpec((hw, 1), lambda b, h: (0, 0)),
                  col(P_Q), col(P_K), col(P_V)],
        out_specs=pl.BlockSpec((seq, hw), lambda b, h: (b, h)),
        out_shape=jax.ShapeDtypeStruct((bsz * seq, BR_W), F32),
        scratch_shapes=[pltpu.VMEM((seq, hw), BF16), pltpu.VMEM((nq, hw, tq), BF16), pltpu.VMEM((hw, 2 * tq), F32)],
        compiler_params=_cparams("parallel", "parallel"),
        name="da_prompt")(*lam_params, sub_g.reshape(hw, 1), proj, proj, proj)


def _da_decode_kernel(pt_ref, lq1, lk1, lq2, lk2, subg_ref, q_ref, kn_ref, vn_ref, *rest, lam_init, n_pages):
    del pt_ref
    kt_refs = rest[:n_pages]
    v_refs = rest[n_pages:2 * n_pages]
    o_ref = rest[2 * n_pages]
    nh = DA_HEADS
    hw = 2 * DA_HEAD
    lam = _lam(lq1, lk1, lq2, lk2, lam_init)
    q = q_ref[...] * (DA_HEAD ** -0.5)
    r = _iota2((2 * nh, BR_W), 0)
    seg = _iota2((2 * nh, BR_W), 1) >> 6
    q_bd = jnp.where(((seg & 1) == (r >> 2)) & ((seg >> 1) == (r & 3)), jnp.broadcast_to(q, (2 * nh, BR_W)), 0.0)
    s_new = jnp.sum(q_bd * kn_ref[...], axis=1, keepdims=True)
    qb = q_bd.astype(BF16)
    s = jnp.concatenate([jnp.dot(qb, kt_refs[j][...].astype(BF16), preferred_element_type=F32)
                         for j in range(n_pages)], axis=1)
    m = jnp.maximum(jnp.max(s, axis=1, keepdims=True), s_new)
    e = jnp.exp(s - m)
    e_new = jnp.exp(s_new - m)
    l = jnp.sum(e, axis=1, keepdims=True) + e_new
    coef = jnp.where(_iota2((2 * nh, 1), 0) < nh, 1.0, -lam) / l
    w = e * coef
    w_new = e_new * coef
    p = (w + pltpu.roll(w, nh, axis=0)).astype(BF16)
    p_new = w_new + pltpu.roll(w_new, nh, axis=0)
    g = subg_ref[...]
    vn = vn_ref[...]
    outs = []
    for h in range(nh):
        v_h = jnp.concatenate([v_refs[j][pl.ds(h, PAGE_SIZE, stride=nh), :].astype(BF16) for j in range(n_pages)],
                              axis=0)
        o_h = (jnp.dot(p, v_h, preferred_element_type=F32)[h:h + 1, :]
               + p_new[h:h + 1, :] * vn[:, h * hw:(h + 1) * hw])
        outs.append(_rms(o_h, g) * (1.0 - lam_init))
    o_ref[...] = jnp.concatenate(outs, axis=1)


def _da_decode(proj3, lam_params, sub_g, cache_kt, cache_v, page_table, layer, lam_init):
    db = proj3.shape[0]
    n_pages = page_table.shape[1]
    hw = 2 * DA_HEAD
    small = pl.BlockSpec((1, DA_HEAD), lambda b, pt: (0, 0))
    row = lambda off: pl.BlockSpec((None, 1, BR_W), lambda b, pt: (b, 0, off // BR_W))

    def page(j):
        return pl.BlockSpec((None, None, BR_W, PAGE_SIZE), lambda b, pt: (layer, pt[b * n_pages + j], 0, 0))

    grid_spec = pltpu.PrefetchScalarGridSpec(
        num_scalar_prefetch=1,
        grid=(db,),
        in_specs=[small, small, small, small, pl.BlockSpec((1, hw), lambda b, pt: (0, 0)),
                  row(P_Q), row(P_K), row(P_V)]
                 + [page(j) for j in range(n_pages)] + [page(j) for j in range(n_pages)],
        out_specs=pl.BlockSpec((None, 1, BR_W), lambda b, pt: (b, 0, 0)))
    out = pl.pallas_call(
        functools.partial(_da_decode_kernel, lam_init=lam_init, n_pages=n_pages),
        grid_spec=grid_spec,
        out_shape=jax.ShapeDtypeStruct((db, 1, BR_W), F32),
        compiler_params=_cparams("parallel"),
        name="da_decode")(page_table.reshape(-1), *lam_params, sub_g.reshape(1, hw), proj3, proj3, proj3,
                          *([cache_kt] * n_pages), *([cache_v] * n_pages))
    return out.reshape(db, BR_W)


def _conv_window(win_ref, x_ref, cw, c, zi):
    @pl.when(zi == 0)
    def _():
        win_ref[0:8, :] = jnp.zeros((8, win_ref.shape[1]), F32)

    @pl.when(zi > 0)
    def _():
        win_ref[0:8, :] = win_ref[c:c + 8, :]

    win_ref[8:8 + c, :] = x_ref[...]
    y = win_ref[5:5 + c, :] * cw[0:1, :]
    for j in range(1, CONV_W):
        y = y + win_ref[5 + j:5 + j + c, :] * cw[j:j + 1, :]
    return y


def _ssd_prompt_kernel(z_ref, xbc_ref, sm_ref, cw_ref, cb_ref, bias_ref, alog_ref, dsk_ref, ng_ref,
                       o_ref, st_ref, win_ref, yd_ref, yo_ref, *, c, nsub):
    zi = pl.program_id(1)
    rows = c * nsub
    rs = lambda s: slice(s * c, (s + 1) * c)

    @pl.when(zi == 0)
    def _():
        st_ref[...] = jnp.zeros(st_ref.shape, F32)

    xbc = _silu(_conv_window(win_ref, xbc_ref, cw_ref[...], rows, zi) + cb_ref[...])
    xs = xbc[:, :BR_W]
    gs = SSM_GROUPS * SSM_STATE
    bm = xbc[:, BR_W:BR_W + gs]
    cm = xbc[:, BR_W + gs:]
    lane = _iota2((1, LANE), 1)
    head_lane = lane < SSM_HEADS
    dt = _softplus(sm_ref[:, :LANE] + bias_ref[:, :LANE])
    a = jnp.where(head_lane, -jnp.exp(alog_ref[:, :LANE]), 0.0)
    dt = jnp.where(head_lane, dt, 0.0)
    tril = _tril_f32(c)
    acs = [_dot_f32(tril, (dt * a)[rs(s), :]) for s in range(nsub)]
    acs_t = [t.T for t in acs]
    acs_last = [t[c - 1:c, :] for t in acs]
    expander = _head_expander(SSM_HEADS, SSM_HEAD).astype(F32)
    xdt = xs * _dot_f32(dt, expander)
    w_t = [(xdt[rs(s), :] * _dot_f32(jnp.exp(acs_last[s] - acs[s]), expander)).T for s in range(nsub)]
    causal = _iota2((c, c), 0) >= _iota2((c, c), 1)
    rep = SSM_HEADS // SSM_GROUPS
    heads = range(SSM_HEADS)
    units = [(s, h) for s in range(nsub) for h in heads]
    sl = [slice(h * SSM_HEAD, (h + 1) * SSM_HEAD) for h in heads]
    bm_g = {(s, g): bm[rs(s), g * SSM_STATE:(g + 1) * SSM_STATE].astype(BF16)
            for s in range(nsub) for g in range(SSM_GROUPS)}
    cm_g = {(s, g): cm[rs(s), g * SSM_STATE:(g + 1) * SSM_STATE].astype(BF16)
            for s in range(nsub) for g in range(SSM_GROUPS)}
    cb = {sg: _dot_nt(cm_g[sg], bm_g[sg]) for sg in bm_g}
    st_add = {(s, h): _dot(w_t[s][sl[h], :], bm_g[s, h // rep]) for s, h in units}
    decay = {(s, h): jnp.exp(jnp.where(causal, acs[s][:, h:h + 1] - acs_t[s][h:h + 1, :], -jnp.inf)) for s, h in units}
    for s, h in units:
        yd_ref[rs(s), sl[h]] = _dot(cb[s, h // rep] * decay[s, h], xdt[rs(s), sl[h]])
    h_prev = [st_ref[h] for h in heads]
    for s in range(nsub):
        for h in heads:
            yo_ref[rs(s), sl[h]] = _dot_nt(cm_g[s, h // rep], h_prev[h])
        h_prev = [h_prev[h] * jnp.exp(acs_last[s][:, h:h + 1]) + st_add[s, h] for h in heads]
    for h in heads:
        st_ref[h] = h_prev[h]
    decay_in = jnp.concatenate([_dot_f32(jnp.exp(acs[s]), expander) for s in range(nsub)], axis=0)
    y = yd_ref[...] + yo_ref[...] * decay_in + dsk_ref[...] * xs
    y = y * _silu(z_ref[...])
    gw = BR_W // SSM_GROUPS
    ng = ng_ref[...]
    o_ref[...] = jnp.concatenate([_rms(y[:, g * gw:(g + 1) * gw], ng[:, g * gw:(g + 1) * gw])
                                  for g in range(SSM_GROUPS)], axis=1)


def _full(shape):
    return pl.BlockSpec(shape, lambda *a: (0,) * len(shape))


SSD_KERNEL_NSUB = 2


def _ssd_prompt(proj, cw, cb, bias_row, alog_row, dsk, ng, bsz, seq):
    c = SSM_CHUNK * SSD_KERNEL_NSUB
    nc = seq // c
    return pl.pallas_call(
        functools.partial(_ssd_prompt_kernel, c=SSM_CHUNK, nsub=SSD_KERNEL_NSUB),
        grid=(bsz, nc),
        in_specs=[pl.BlockSpec((c, BR_W), lambda b, z: (b * nc + z, P_SZ // BR_W)),
                  pl.BlockSpec((c, SSM_CONV_CH), lambda b, z: (b * nc + z, P_XBC // SSM_CONV_CH)),
                  pl.BlockSpec((c, SMALL_W), lambda b, z: (b * nc + z, P_SMALL // SMALL_W)),
                  _full((CONV_W, SSM_CONV_CH)), _full((1, SSM_CONV_CH)), _full((1, SMALL_W)), _full((1, SMALL_W)),
                  _full((1, BR_W)), _full((1, BR_W))],
        out_specs=[pl.BlockSpec((c, BR_W), lambda b, z: (b * nc + z, 0)),
                   pl.BlockSpec((None, SSM_HEADS, SSM_HEAD, SSM_STATE), lambda b, z: (b, 0, 0, 0))],
        out_shape=[jax.ShapeDtypeStruct((bsz * seq, BR_W), F32),
                   jax.ShapeDtypeStruct((bsz, SSM_HEADS, SSM_HEAD, SSM_STATE), F32)],
        scratch_shapes=[pltpu.VMEM((c + 8, SSM_CONV_CH), F32), pltpu.VMEM((c, BR_W), F32), pltpu.VMEM((c, BR_W), F32)],
        compiler_params=_cparams("parallel", "arbitrary"),
        name="ssd_prompt")(proj, proj, proj, cw, cb, bias_row, alog_row, dsk, ng)


def _inv_unit_lower_minus_eye(mats, n, c):
    row = _iota2((n, n), 0)
    col = _iota2((n, n), 1)
    sh = 4
    ps = [jnp.where((row >> sh) == (col >> sh), a, 0.0) for a in mats]
    ys = [-p for p in ps]
    for _ in range(sh - 1):
        ps = [_dot(p, p) for p in ps]
        yp = [_dot(y, p) for y, p in zip(ys, ps)]
        ys = [y + p + t for y, p, t in zip(ys, ps, yp)]
    while (1 << sh) < c:
        mask = ((row >> (sh + 1)) == (col >> (sh + 1))) & ((row >> sh) != (col >> sh))
        offs = [jnp.where(mask, a, 0.0) for a in mats]
        ts = [off + _dot(y, off) for y, off in zip(ys, offs)]
        ty = [_dot(t, y) for t, y in zip(ts, ys)]
        ys = [y - (t + u) for y, t, u in zip(ys, ts, ty)]
        sh += 1
    return ys


def _gdn_prompt_kernel(qkv_ref, z_ref, sm_ref, cw_ref, bias_ref, alog_ref, ng_ref, o_ref, st_ref, win_ref, *, c, nsub):
    zi = pl.program_id(1)
    hd = GDN_HEAD
    heads = range(GDN_HEADS)
    rows = c * nsub
    units = [(s, h) for s in range(nsub) for h in heads]
    rs = lambda s: slice(s * c, (s + 1) * c)

    @pl.when(zi == 0)
    def _():
        st_ref[...] = jnp.zeros(st_ref.shape, F32)

    qkv = _silu(_conv_window(win_ref, qkv_ref, cw_ref[...], rows, zi))
    pre = sm_ref[:, :LANE] + bias_ref[:, :LANE]
    g_all = -jnp.exp(alog_ref[:, :LANE]) * _softplus(pre)
    beta_all = _sigmoid(sm_ref[:, :LANE])
    tril = _tril_f32(c)
    gc = [_dot_f32(tril, g_all[rs(s), :]) for s in range(nsub)]
    gc_t = [g.T for g in gc]
    row = _iota2((c, c), 0)
    col = _iota2((c, c), 1)
    l2 = lambda t: t * lax.rsqrt(jnp.sum(t * t, axis=-1, keepdims=True) + EPS)

    q = {(s, h): l2(qkv[rs(s), h * hd:(h + 1) * hd]) * (hd ** -0.5) for s, h in units}
    k = {(s, h): l2(qkv[rs(s), BR_W + h * hd:BR_W + (h + 1) * hd]) for s, h in units}
    v = {(s, h): qkv[rs(s), 2 * BR_W + h * hd:2 * BR_W + (h + 1) * hd] for s, h in units}
    g_col = {(s, h): gc[s][:, S_GA + h:S_GA + h + 1] for s, h in units}
    g_last = {(s, h): gc[s][c - 1:c, S_GA + h:S_GA + h + 1] for s, h in units}
    beta = {(s, h): beta_all[rs(s), S_GB + h:S_GB + h + 1] for s, h in units}
    decay = {(s, h): jnp.exp(jnp.where(row >= col, g_col[s, h] - gc_t[s][S_GA + h:S_GA + h + 1, :], -jnp.inf))
             for s, h in units}
    kb = {u: k[u] * beta[u] for u in units}
    eg = {u: jnp.exp(g_col[u]) for u in units}
    a_low = [jnp.where(row > col, _dot_nt(kb[u], k[u]) * decay[u], 0.0) for u in units]
    attn = {u: _dot_nt(q[u], k[u]) * decay[u] for u in units}
    t_dev = dict(zip(units, _inv_unit_lower_minus_eye(a_low, c, c)))
    vb = {u: v[u] * beta[u] for u in units}
    kbe = {u: kb[u] * eg[u] for u in units}
    u_mat = {u: vb[u] + _dot(t_dev[u], vb[u]) for u in units}
    w_mat = {u: kbe[u] + _dot(t_dev[u], kbe[u]) for u in units}
    kg_t = {u: (k[u] * jnp.exp(g_last[u] - g_col[u])).T for u in units}
    ng = ng_ref[...]
    state = [st_ref[h] for h in heads]
    for s in range(nsub):
        v_new = [u_mat[s, h] - _dot(w_mat[s, h], state[h]) for h in heads]
        o = [_dot(q[s, h] * eg[s, h], state[h]) + _dot(attn[s, h], v_new[h]) for h in heads]
        state = [state[h] * jnp.exp(g_last[s, h]) + _dot(kg_t[s, h], v_new[h]) for h in heads]
        for h in heads:
            o_ref[rs(s), h * hd:(h + 1) * hd] = _rms(o[h], ng) * _silu(z_ref[rs(s), h * hd:(h + 1) * hd])
    for h in heads:
        st_ref[h] = state[h]


GDN_KERNEL_CHUNK = 128
GDN_KERNEL_NSUB = 4


def _gdn_prompt(proj, cw, bias_row, alog_row, ng, bsz, seq):
    rows = GDN_KERNEL_CHUNK * GDN_KERNEL_NSUB
    nc = seq // rows
    return pl.pallas_call(
        functools.partial(_gdn_prompt_kernel, c=GDN_KERNEL_CHUNK, nsub=GDN_KERNEL_NSUB),
        grid=(bsz, nc),
        in_specs=[pl.BlockSpec((rows, GDN_CONV_CH), lambda b, z: (b * nc + z, P_GQKV // GDN_CONV_CH)),
                  pl.BlockSpec((rows, BR_W), lambda b, z: (b * nc + z, P_GZ // BR_W)),
                  pl.BlockSpec((rows, SMALL_W), lambda b, z: (b * nc + z, P_SMALL // SMALL_W)),
                  _full((CONV_W, GDN_CONV_CH)), _full((1, SMALL_W)), _full((1, SMALL_W)), _full((1, GDN_HEAD))],
        out_specs=[pl.BlockSpec((rows, BR_W), lambda b, z: (b * nc + z, 0)),
                   pl.BlockSpec((None, GDN_HEADS, GDN_HEAD, GDN_HEAD), lambda b, z: (b, 0, 0, 0))],
        out_shape=[jax.ShapeDtypeStruct((bsz * seq, BR_W), F32),
                   jax.ShapeDtypeStruct((bsz, GDN_HEADS, GDN_HEAD, GDN_HEAD), F32)],
        scratch_shapes=[pltpu.VMEM((rows + 8, GDN_CONV_CH), F32)],
        compiler_params=_cparams("parallel", "arbitrary"),
        name="gdn_prompt")(proj, proj, proj, cw, bias_row, alog_row, ng)


def _mlstm_prompt_kernel(q_ref, k_ref, v_ref, og_ref, sm_ref, bias_ref, ng_ref, o_ref, c_ref, n_ref, m_ref,
                         *, c, nsub):
    zi = pl.program_id(1)

    @pl.when(zi == 0)
    def _():
        c_ref[...] = jnp.zeros(c_ref.shape, F32)
        n_ref[...] = jnp.zeros(n_ref.shape, F32)
        m_ref[...] = jnp.zeros(m_ref.shape, F32)

    pre = sm_ref[:, :LANE] + bias_ref[:, :LANE]
    logf = -_softplus(-pre)
    rs = lambda s: slice(s * c, (s + 1) * c)
    tril = _tril_f32(c)
    bcum = [_dot_f32(tril, logf[rs(s), :]) for s in range(nsub)]
    bcum_t = [b.T for b in bcum]
    pre_t = [pre[rs(s), :].T for s in range(nsub)]
    causal = _iota2((c, c), 0) >= _iota2((c, c), 1)
    ng = ng_ref[...]
    hd = ML_HEAD
    heads = range(ML_HEADS)
    units = [(s, h) for s in range(nsub) for h in heads]
    sl = [slice(h * hd, (h + 1) * hd) for h in heads]
    q = {(s, h): q_ref[rs(s), sl[h]] for s, h in units}
    k = {(s, h): k_ref[rs(s), sl[h]] * (hd ** -0.5) for s, h in units}
    v = {(s, h): v_ref[rs(s), sl[h]] for s, h in units}
    qk = {u: _dot_nt(q[u], k[u]) for u in units}
    b_col = {(s, h): bcum[s][:, S_MF + h:S_MF + h + 1] for s, h in units}
    b_last = {u: b_col[u][c - 1:c, :] for u in units}
    i_col = {(s, h): pre[rs(s), S_MI + h:S_MI + h + 1] for s, h in units}
    dmat = {(s, h): jnp.where(causal, b_col[s, h] - bcum_t[s][S_MF + h:S_MF + h + 1, :]
                              + pre_t[s][S_MI + h:S_MI + h + 1, :], -jnp.inf) for s, h in units}
    dmax = {u: jnp.max(dmat[u], axis=1, keepdims=True) for u in units}
    c_prev = [c_ref[h] for h in heads]
    n_prev = [n_ref[h:h + 1, :] for h in heads]
    m_prev = [m_ref[h:h + 1, 0:1] for h in heads]
    for s in range(nsub):
        qc = [_dot(q[s, h], c_prev[h]) for h in heads]
        m_t = [jnp.maximum(b_col[s, h] + m_prev[h], dmax[s, h]) for h in heads]
        m_new = [m_t[h][c - 1:c, :] for h in heads]
        smat = [qk[s, h] * jnp.exp(dmat[s, h] - m_t[h]) for h in heads]
        kw = [k[s, h] * jnp.exp(b_last[s, h] - b_col[s, h] + i_col[s, h] - m_new[h]) for h in heads]
        sv = [_dot(smat[h], v[s, h]) for h in heads]
        kv = [_dot(kw[h].T, v[s, h]) for h in heads]
        for h in heads:
            w_prev = jnp.exp(b_col[s, h] + m_prev[h] - m_t[h])
            num = w_prev * qc[h] + sv[h]
            den = (w_prev * jnp.sum(q[s, h] * n_prev[h], axis=1, keepdims=True)
                   + jnp.sum(smat[h], axis=1, keepdims=True))
            hid = num / jnp.maximum(jnp.abs(den), jnp.exp(-m_t[h]))
            o_ref[rs(s), sl[h]] = _rms(hid, ng) * _sigmoid(og_ref[rs(s), sl[h]])
        w_c = [jnp.exp(b_last[s, h] + m_prev[h] - m_new[h]) for h in heads]
        c_prev = [c_prev[h] * w_c[h] + kv[h] for h in heads]
        n_prev = [n_prev[h] * w_c[h] + jnp.sum(kw[h], axis=0, keepdims=True) for h in heads]
        m_prev = m_new
    for h in heads:
        c_ref[h] = c_prev[h]
        n_ref[h:h + 1, :] = n_prev[h]
        m_ref[h:h + 1, :] = jnp.broadcast_to(m_prev[h], (1, LANE))


ML_KERNEL_NSUB = 2


def _mlstm_prompt(proj, bias_row, ng, bsz, seq):
    nsub = ML_KERNEL_NSUB
    c = ML_CHUNK * nsub
    nc = seq // c
    col = lambda i: pl.BlockSpec((c, BR_W), lambda b, z: (b * nc + z, P_MQ // BR_W + i))
    return pl.pallas_call(
        functools.partial(_mlstm_prompt_kernel, c=ML_CHUNK, nsub=nsub),
        grid=(bsz, nc),
        in_specs=[col(0), col(1), col(2), col(3),
                  pl.BlockSpec((c, SMALL_W), lambda b, z: (b * nc + z, P_SMALL // SMALL_W)),
                  _full((1, SMALL_W)), _full((1, ML_HEAD))],
        out_specs=[pl.BlockSpec((c, BR_W), lambda b, z: (b * nc + z, 0)),
                   pl.BlockSpec((None, ML_HEADS, ML_HEAD, ML_HEAD), lambda b, z: (b, 0, 0, 0)),
                   pl.BlockSpec((None, ML_HEADS, ML_HEAD), lambda b, z: (b, 0, 0)),
                   pl.BlockSpec((None, 8, LANE), lambda b, z: (b, 0, 0))],
        out_shape=[jax.ShapeDtypeStruct((bsz * seq, BR_W), F32),
                   jax.ShapeDtypeStruct((bsz, ML_HEADS, ML_HEAD, ML_HEAD), F32),
                   jax.ShapeDtypeStruct((bsz, ML_HEADS, ML_HEAD), F32),
                   jax.ShapeDtypeStruct((bsz, 8, LANE), F32)],
        compiler_params=_cparams("parallel", "arbitrary"),
        name="mlstm_prompt")(proj, proj, proj, proj, proj, bias_row, ng)


STEP_ROWS = 8


def _split_f32(x):
    hi = x.astype(BF16).astype(F32)
    return hi, (x - hi).astype(BF16).astype(F32)


def _outer_lhs(rows):
    hi, lo = _split_f32(rows)
    return jnp.concatenate([hi, hi, lo, jnp.zeros_like(hi)], axis=0).T.astype(BF16)


def _outer_rhs(rows, r):
    keep = _iota2(rows.shape, 0) == r
    hi, lo = _split_f32(jnp.where(keep, rows, 0.0))
    return jnp.concatenate([hi, lo, hi, jnp.zeros_like(hi)], axis=0).astype(BF16)


def _pick_rows(results):
    rowi = _iota2(results[0].shape, 0)
    out = results[0]
    for r in range(1, len(results)):
        out = jnp.where(rowi == r, results[r], out)
    return out


def _step_conv(x_ref, buf_ref, cw_ref, nb_ref):
    cw = cw_ref[...]
    x, b0, b1, b2 = x_ref[...], buf_ref[0], buf_ref[1], buf_ref[2]
    nb_ref[0] = b1
    nb_ref[1] = b2
    nb_ref[2] = x
    return b0 * cw[0:1] + b1 * cw[1:2] + b2 * cw[2:3] + x * cw[3:4]


def _ssd_step_kernel(z_ref, xbc_ref, sm_ref, buf_ref, st_ref, cw_ref, cb_ref, bias_ref, alog_ref, dsk_ref, ng_ref,
                     o_ref, nb_ref, nst_ref):
    rb = STEP_ROWS
    xbc = _silu(_step_conv(xbc_ref, buf_ref, cw_ref, nb_ref) + cb_ref[...])
    xs = xbc[:, :BR_W]
    gs = SSM_GROUPS * SSM_STATE
    bm = xbc[:, BR_W:BR_W + gs]
    cm = xbc[:, BR_W + gs:]
    dt = _softplus(sm_ref[:, :LANE] + bias_ref[:, :LANE])
    d_a = jnp.exp(-jnp.exp(alog_ref[:, :LANE]) * dt)
    lane = _iota2((rb, 2 * SSM_HEAD), 1)
    sub = _iota2((2 * SSM_HEAD, 1), 0)
    pairs_per_group = SSM_HEADS // SSM_GROUPS // 2
    ys = []
    for j in range(SSM_HEADS // 2):
        g = j // pairs_per_group
        h0, h1 = 2 * j, 2 * j + 1
        dt_pair = jnp.where(lane < SSM_HEAD, dt[:, h0:h0 + 1], dt[:, h1:h1 + 1])
        xt = _outer_lhs(xs[:, j * 2 * SSM_HEAD:(j + 1) * 2 * SSM_HEAD] * dt_pair)
        bm_g = bm[:, g * SSM_STATE:(g + 1) * SSM_STATE]
        cm_t = cm[:, g * SSM_STATE:(g + 1) * SSM_STATE].T.astype(BF16)
        coli = _iota2((2 * SSM_HEAD, rb), 1)
        y_t = jnp.zeros((2 * SSM_HEAD, rb), F32)
        for r in range(rb):
            decay = jnp.where(sub < SSM_HEAD, d_a[r:r + 1, h0:h0 + 1], d_a[r:r + 1, h1:h1 + 1])
            h_new = st_ref[r, j] * decay + jnp.dot(xt, _outer_rhs(bm_g, r), preferred_element_type=F32)
            nst_ref[r, j] = h_new
            y_t = jnp.where(coli == r, jnp.dot(h_new.astype(BF16), cm_t, preferred_element_type=F32), y_t)
        ys.append(y_t.T)
    y = (jnp.concatenate(ys, axis=1) + dsk_ref[...] * xs) * _silu(z_ref[...])
    gw = BR_W // SSM_GROUPS
    ng = ng_ref[...]
    o_ref[...] = jnp.concatenate([_rms(y[:, g * gw:(g + 1) * gw], ng[:, g * gw:(g + 1) * gw])
                                  for g in range(SSM_GROUPS)], axis=1)


def _ssd_step(proj, buf, st, layer, cw, cb, bias_row, alog_row, dsk, ng):
    db = proj.shape[0]
    rb = STEP_ROWS
    npair = SSM_HEADS // 2
    sdim = 2 * SSM_HEAD
    return pl.pallas_call(
        _ssd_step_kernel,
        grid=(db // rb,),
        in_specs=[pl.BlockSpec((rb, BR_W), lambda i: (i, P_SZ // BR_W)),
                  pl.BlockSpec((rb, SSM_CONV_CH), lambda i: (i, P_XBC // SSM_CONV_CH)),
                  pl.BlockSpec((rb, SMALL_W), lambda i: (i, P_SMALL // SMALL_W)),
                  pl.BlockSpec((None, CONV_W - 1, rb, SSM_CONV_CH), lambda i: (layer, 0, i, 0)),
                  pl.BlockSpec((None, rb, npair, sdim, SSM_STATE), lambda i: (layer, i, 0, 0, 0)),
                  _full((CONV_W, SSM_CONV_CH)), _full((1, SSM_CONV_CH)), _full((1, SMALL_W)), _full((1, SMALL_W)),
                  _full((1, BR_W)), _full((1, BR_W))],
        out_specs=[pl.BlockSpec((rb, BR_W), lambda i: (i, 0)),
                   pl.BlockSpec((CONV_W - 1, rb, SSM_CONV_CH), lambda i: (0, i, 0)),
                   pl.BlockSpec((rb, npair, sdim, SSM_STATE), lambda i: (i, 0, 0, 0))],
        out_shape=[jax.ShapeDtypeStruct((db, BR_W), F32),
                   jax.ShapeDtypeStruct((CONV_W - 1, db, SSM_CONV_CH), F32),
                   jax.ShapeDtypeStruct((db, npair, sdim, SSM_STATE), F32)],
        compiler_params=_cparams("parallel"),
        name="ssd_step")(proj, proj, proj, buf, st, cw, cb, bias_row, alog_row, dsk, ng)


def _gdn_step_kernel(qkv_ref, z_ref, sm_ref, buf_ref, st_ref, cw_ref, bias_ref, alog_ref, ng_ref,
                     o_ref, nb_ref, nst_ref):
    rb = STEP_ROWS
    hd = GDN_HEAD
    ng = ng_ref[...]
    qkv = _silu(_step_conv(qkv_ref, buf_ref, cw_ref, nb_ref))
    sm = sm_ref[:, :LANE]
    eg_all = jnp.exp(-jnp.exp(alog_ref[:, :LANE]) * _softplus(sm + bias_ref[:, :LANE]))
    beta_all = _sigmoid(sm)
    l2 = lambda t: t * lax.rsqrt(jnp.sum(t * t, axis=-1, keepdims=True) + EPS)
    for h in range(GDN_HEADS):
        q = l2(qkv[:, h * hd:(h + 1) * hd]) * (hd ** -0.5)
        k = l2(qkv[:, BR_W + h * hd:BR_W + (h + 1) * hd])
        v = qkv[:, 2 * BR_W + h * hd:2 * BR_W + (h + 1) * hd]
        eg = eg_all[:, S_GA + h:S_GA + h + 1]
        beta = beta_all[:, S_GB + h:S_GB + h + 1]
        lhs = jnp.concatenate([k * (beta * eg), q * eg], axis=0).astype(BF16)
        res = [jnp.dot(lhs, st_ref[r, h].astype(BF16), preferred_element_type=F32) for r in range(rb)]
        v_new = v * beta - _pick_rows([t[:rb] for t in res])
        o = _pick_rows([t[rb:] for t in res]) + jnp.sum(q * k, axis=1, keepdims=True) * v_new
        kt = _outer_lhs(k)
        for r in range(rb):
            nst_ref[r, h] = (st_ref[r, h] * eg[r:r + 1, :]
                             + jnp.dot(kt, _outer_rhs(v_new, r), preferred_element_type=F32))
        o_ref[:, h * hd:(h + 1) * hd] = _rms(o, ng) * _silu(z_ref[:, h * hd:(h + 1) * hd])


def _gdn_step(proj, buf, st, layer, cw, bias_row, alog_row, ng):
    db = proj.shape[0]
    rb = STEP_ROWS
    return pl.pallas_call(
        _gdn_step_kernel,
        grid=(db // rb,),
        in_specs=[pl.BlockSpec((rb, GDN_CONV_CH), lambda i: (i, P_GQKV // GDN_CONV_CH)),
                  pl.BlockSpec((rb, BR_W), lambda i: (i, P_GZ // BR_W)),
                  pl.BlockSpec((rb, SMALL_W), lambda i: (i, P_SMALL // SMALL_W)),
                  pl.BlockSpec((None, CONV_W - 1, rb, GDN_CONV_CH), lambda i: (layer, 0, i, 0)),
                  pl.BlockSpec((None, rb, GDN_HEADS, GDN_HEAD, GDN_HEAD), lambda i: (layer, i, 0, 0, 0)),
                  _full((CONV_W, GDN_CONV_CH)), _full((1, SMALL_W)), _full((1, SMALL_W)), _full((1, GDN_HEAD))],
        out_specs=[pl.BlockSpec((rb, BR_W), lambda i: (i, 0)),
                   pl.BlockSpec((CONV_W - 1, rb, GDN_CONV_CH), lambda i: (0, i, 0)),
                   pl.BlockSpec((rb, GDN_HEADS, GDN_HEAD, GDN_HEAD), lambda i: (i, 0, 0, 0))],
        out_shape=[jax.ShapeDtypeStruct((db, BR_W), F32),
                   jax.ShapeDtypeStruct((CONV_W - 1, db, GDN_CONV_CH), F32),
                   jax.ShapeDtypeStruct((db, GDN_HEADS, GDN_HEAD, GDN_HEAD), F32)],
        compiler_params=_cparams("parallel"),
        name="gdn_step")(proj, proj, proj, buf, st, cw, bias_row, alog_row, ng)


def _mlstm_step_kernel(q_ref, k_ref, v_ref, og_ref, sm_ref, c_ref, n_ref, m_ref, bias_ref, ng_ref,
                       o_ref, nc_ref, nn_ref, nm_ref):
    rb = STEP_ROWS
    hd = ML_HEAD
    ng = ng_ref[...]
    pre = sm_ref[:, :LANE] + bias_ref[:, :LANE]
    logf_all = -_softplus(-pre)
    m_all = m_ref[...]
    lane4 = _iota2((rb, ML_HEADS), 1)
    m_out = jnp.zeros((rb, ML_HEADS), F32)
    for h in range(ML_HEADS):
        sl = slice(h * hd, (h + 1) * hd)
        q = q_ref[:, sl]
        k = k_ref[:, sl] * (hd ** -0.5)
        v = v_ref[:, sl]
        i_pre = pre[:, S_MI + h:S_MI + h + 1]
        logf = logf_all[:, S_MF + h:S_MF + h + 1]
        m_prev = m_all[:, h:h + 1]
        m_t = jnp.maximum(logf + m_prev, i_pre)
        w_prev = jnp.exp(logf + m_prev - m_t)
        w_j = jnp.exp(i_pre - m_t)
        s = jnp.sum(q * k, axis=1, keepdims=True) * w_j
        n_prev = n_ref[pl.ds(h, rb, stride=ML_HEADS), :]
        qb = q.astype(BF16)
        qc = _pick_rows([jnp.dot(qb, c_ref[r, h].astype(BF16), preferred_element_type=F32) for r in range(rb)])
        num = w_prev * qc + s * v
        den = w_prev * jnp.sum(q * n_prev, axis=1, keepdims=True) + s
        hid = num / jnp.maximum(jnp.abs(den), jnp.exp(-m_t))
        kw = k * w_j
        kt = _outer_lhs(kw)
        for r in range(rb):
            nc_ref[r, h] = (c_ref[r, h] * w_prev[r:r + 1, :]
                            + jnp.dot(kt, _outer_rhs(v, r), preferred_element_type=F32))
        nn_ref[h] = n_prev * w_prev + kw
        m_out = jnp.where(lane4 == h, m_t, m_out)
        o_ref[:, sl] = _rms(hid, ng) * _sigmoid(og_ref[:, sl])
    nm_ref[...] = m_out


def _mlstm_step(proj, c0, n0, m0, layer, bias_row, ng):
    db = proj.shape[0]
    rb = STEP_ROWS
    col = lambda j: pl.BlockSpec((rb, BR_W), lambda i: (i, P_MQ // BR_W + j))
    return pl.pallas_call(
        _mlstm_step_kernel,
        grid=(db // rb,),
        in_specs=[col(0), col(1), col(2), col(3),
                  pl.BlockSpec((rb, SMALL_W), lambda i: (i, P_SMALL // SMALL_W)),
                  pl.BlockSpec((None, rb, ML_HEADS, ML_HEAD, ML_HEAD), lambda i: (layer, i, 0, 0, 0)),
                  pl.BlockSpec((None, rb * ML_HEADS, ML_HEAD), lambda i: (layer, i, 0)),
                  pl.BlockSpec((None, rb, ML_HEADS), lambda i: (layer, i, 0)),
                  _full((1, SMALL_W)), _full((1, ML_HEAD))],
        out_specs=[pl.BlockSpec((rb, BR_W), lambda i: (i, 0)),
                   pl.BlockSpec((rb, ML_HEADS, ML_HEAD, ML_HEAD), lambda i: (i, 0, 0, 0)),
                   pl.BlockSpec((ML_HEADS, rb, ML_HEAD), lambda i: (0, i, 0)),
                   pl.BlockSpec((rb, ML_HEADS), lambda i: (i, 0))],
        out_shape=[jax.ShapeDtypeStruct((db, BR_W), F32),
                   jax.ShapeDtypeStruct((db, ML_HEADS, ML_HEAD, ML_HEAD), F32),
                   jax.ShapeDtypeStruct((ML_HEADS, db, ML_HEAD), F32),
                   jax.ShapeDtypeStruct((db, ML_HEADS), F32)],
        compiler_params=_cparams("parallel"),
        name="mlstm_step")(proj, proj, proj, proj, proj, c0, n0, m0, bias_row, ng)


def _merge_kernel(x_ref, a_ref, b_ref, c_ref, d_ref, g0, g1, g2, g3, wb_ref, wo_ref, o_ref):
    acc = None
    for n, (br, gate) in enumerate(((a_ref, g0), (b_ref, g1), (c_ref, g2), (d_ref, g3))):
        t = _sigmoid(gate[...]) * jnp.dot(br[...].astype(BF16), wb_ref[n], preferred_element_type=F32)
        acc = t if acc is None else acc + t
    o_ref[...] = x_ref[...] + jnp.dot(acc.astype(BF16), wo_ref[...], preferred_element_type=F32)


def _merge(x, branches, proj, wb, wo, tm):
    m = x.shape[0]
    br = pl.BlockSpec((tm, BR_W), lambda i: (i, 0))
    gate = lambda n: pl.BlockSpec((tm, D_MODEL), lambda i: (i, P_GATE // D_MODEL + n))
    return pl.pallas_call(
        _merge_kernel,
        grid=(m // tm,),
        in_specs=[pl.BlockSpec((tm, D_MODEL), lambda i: (i, 0)), br, br, br, br,
                  gate(0), gate(1), gate(2), gate(3),
                  _full((N_BRANCH, BR_W, D_MODEL)), _full((D_MODEL, D_MODEL))],
        out_specs=pl.BlockSpec((tm, D_MODEL), lambda i: (i, 0)),
        out_shape=jax.ShapeDtypeStruct((m, D_MODEL), F32),
        compiler_params=_cparams("parallel"),
        name="merge")(x, *branches, proj, proj, proj, proj, wb, wo)


def _cross_prompt_kernel(x_ref, g_ref, mk_ref, mv_ref, wq_ref, wo_ref, o_ref):
    x = x_ref[...]
    h = _rms(x, g_ref[...]).astype(BF16)
    q = jnp.dot(h, wq_ref[...], preferred_element_type=F32) * (X_HEAD ** -0.5)
    outs = []
    for hd in range(X_HEADS):
        sl = slice(hd * X_HEAD, (hd + 1) * X_HEAD)
        s = _dot_nt(q[:, sl], mk_ref[:, sl])
        p = jnp.exp(s - jnp.max(s, axis=1, keepdims=True))
        p = p / jnp.sum(p, axis=1, keepdims=True)
        outs.append(_dot(p, mv_ref[:, sl]))
    o = jnp.concatenate(outs, axis=1).astype(BF16)
    o_ref[...] = x + jnp.dot(o, wo_ref[...], preferred_element_type=F32)


def _cross_prompt(x, g, mkv, wq, wo, bsz, seq, tq=512):
    nq = seq // tq
    d = D_MODEL
    return pl.pallas_call(
        _cross_prompt_kernel,
        grid=(bsz, nq),
        in_specs=[pl.BlockSpec((tq, d), lambda b, i: (b * nq + i, 0)), _full((1, d)),
                  pl.BlockSpec((N_MEM, d), lambda b, i: (b, 0)),
                  pl.BlockSpec((N_MEM, d), lambda b, i: (b, 1)),
                  _full((d, d)), _full((d, d))],
        out_specs=pl.BlockSpec((tq, d), lambda b, i: (b * nq + i, 0)),
        out_shape=jax.ShapeDtypeStruct((bsz * seq, d), F32),
        compiler_params=_cparams("parallel", "arbitrary"),
        name="cross_prompt")(x, g.reshape(1, d), mkv, mkv, wq, wo)


CROSS_ROWS = 4


def _cross_decode_kernel(q_ref, mk_ref, mv_ref, o_ref):
    halves = X_HEAD // LANE
    rows_per_tok = halves * X_HEADS

    def head_slab(ref, r, h):
        return jnp.concatenate([ref[r, pl.ds(t * X_HEADS + h, N_MEM, stride=rows_per_tok), :].astype(BF16)
                                for t in range(halves)], axis=1)

    pairs = [(r, h) for r in range(CROSS_ROWS) for h in range(X_HEADS)]
    qs = [q_ref[r] * (X_HEAD ** -0.5) for r in range(CROSS_ROWS)]
    scores = [lax.dot_general(jnp.broadcast_to(qs[r][:, h * X_HEAD:(h + 1) * X_HEAD], (8, X_HEAD)).astype(BF16),
                              head_slab(mk_ref, r, h), (((1,), (1,)), ((), ())), preferred_element_type=F32)
              for r, h in pairs]
    probs = []
    for s in scores:
        e = jnp.exp(s - jnp.max(s, axis=1, keepdims=True))
        probs.append((e / jnp.sum(e, axis=1, keepdims=True)).astype(BF16))
    outs = [jnp.dot(p, head_slab(mv_ref, r, h), preferred_element_type=F32)[0:1] for p, (r, h) in zip(probs, pairs)]
    for r in range(CROSS_ROWS):
        o_ref[r] = jnp.concatenate(outs[r * X_HEADS:(r + 1) * X_HEADS], axis=1)


def _mem_rows(mem):
    depth, db = mem.shape[:2]
    halves = X_HEAD // LANE
    m = mem.reshape(depth, db, N_MEM, X_HEADS, halves, LANE).transpose(0, 1, 2, 4, 3, 5)
    return m.reshape(depth, db, N_MEM * halves * X_HEADS, LANE)


def _cross_decode(q3, mem_k, mem_v, layer):
    db = q3.shape[0]
    rb = CROSS_ROWS
    d = D_MODEL
    mem = pl.BlockSpec((None, rb, mem_k.shape[2], LANE), lambda i: (layer, i, 0, 0))
    return pl.pallas_call(
        _cross_decode_kernel,
        grid=(db // rb,),
        in_specs=[pl.BlockSpec((rb, 1, d), lambda i: (i, 0, 0)), mem, mem],
        out_specs=pl.BlockSpec((rb, 1, d), lambda i: (i, 0, 0)),
        out_shape=jax.ShapeDtypeStruct((db, 1, d), F32),
        compiler_params=_cparams("parallel"),
        name="cross_decode")(q3, mem_k, mem_v)


def _swiglu_kernel(x_ref, g_ref, wg_ref, wu_ref, wd_ref, o_ref, h_ref, acc_ref):
    j = pl.program_id(1)

    @pl.when(j == 0)
    def _():
        h_ref[...] = _rms(x_ref[...], g_ref[...]).astype(BF16)
        acc_ref[...] = jnp.zeros(acc_ref.shape, F32)

    h = h_ref[...]
    gate = jnp.dot(h, wg_ref[...], preferred_element_type=F32)
    up = jnp.dot(h, wu_ref[...], preferred_element_type=F32)
    acc_ref[...] += jnp.dot((_silu(gate) * up).astype(BF16), wd_ref[...], preferred_element_type=F32)

    @pl.when(j == pl.num_programs(1) - 1)
    def _():
        o_ref[...] = x_ref[...] + acc_ref[...]


def _swiglu(x, g, wgu, wd, tm, tf=D_FF // 2):
    m, d = x.shape
    nf = D_FF // tf
    return pl.pallas_call(
        _swiglu_kernel,
        grid=(m // tm, nf),
        in_specs=[pl.BlockSpec((tm, d), lambda i, j: (i, 0)), _full((1, d)),
                  pl.BlockSpec((d, tf), lambda i, j: (0, j)),
                  pl.BlockSpec((d, tf), lambda i, j: (0, nf + j)),
                  pl.BlockSpec((tf, d), lambda i, j: (j, 0))],
        out_specs=pl.BlockSpec((tm, d), lambda i, j: (i, 0)),
        out_shape=jax.ShapeDtypeStruct((m, d), F32),
        scratch_shapes=[pltpu.VMEM((tm, d), BF16), pltpu.VMEM((tm, d), F32)],
        compiler_params=_cparams("parallel", "arbitrary"),
        name="swiglu")(x, g.reshape(1, d), wgu, wgu, wd)


def _pack_w_in(w):
    offs = np.cumsum((0,) + IN_SPLITS)
    seg = lambda i: w[:, offs[i]:offs[i + 1]]
    small = jnp.concatenate([seg(5), seg(8), seg(9), seg(14), seg(15),
                             jnp.zeros((w.shape[0], SMALL_W - 24), w.dtype)], axis=1)
    order = (0, 1, 2, 3, 4, 6, 7, 10, 11, 12, 13, 16)
    return jnp.concatenate([seg(i) for i in order] + [small], axis=1).astype(BF16)


def _small_row(parts):
    row = jnp.zeros((SMALL_W,), F32)
    for off, val in parts:
        row = lax.dynamic_update_slice(row, val.astype(F32), (off,))
    return row.reshape(1, SMALL_W)


def _layer_params(l, p):
    lp = dict(
        w_in=_pack_w_in(p["w_in"][l]),
        g_mix=p["g_mix"][l],
        lam=tuple(p[n][l].reshape(1, DA_HEAD) for n in ("da_lq1", "da_lk1", "da_lq2", "da_lk2")),
        lam_init=0.8 - 0.6 * math.exp(-0.3 * l),
        sub_g=p["da_sub_g"][l],
        bias_row=_small_row(((S_DT, p["ssm_dt_bias"][l]), (S_GA, p["gdn_dt_bias"][l]),
                             (S_MI, p["ml_i_bias"][l]), (S_MF, p["ml_f_bias"][l]))),
        alog_row=_small_row(((S_DT, p["ssm_a_log"][l]), (S_GA, p["gdn_a_log"][l]))),
        ssm_cw=p["ssm_conv_w"][l], ssm_cb=p["ssm_conv_b"][l].reshape(1, SSM_CONV_CH),
        ssm_dsk=jnp.repeat(p["ssm_d"][l], SSM_HEAD).reshape(1, BR_W),
        ssm_ng=p["ssm_norm_g"][l].reshape(1, BR_W),
        gdn_cw=p["gdn_conv_w"][l], gdn_ng=p["gdn_norm_g"][l].reshape(1, GDN_HEAD),
        ml_ng=p["ml_norm_g"][l].reshape(1, ML_HEAD),
        w_branch=p["w_branch"][l].astype(BF16), w_out=p["w_out"][l].astype(BF16),
        g_cross=p["g_cross"][l], w_cq=p["w_cq"][l].astype(BF16), w_co=p["w_co"][l].astype(BF16),
        g_ffn=p["g_ffn"][l], w_gu=p["w_gu"][l].astype(BF16), w_down=p["w_down"][l].astype(BF16),
    )
    return lp


def _prompt_layer(x, lp, mkv, bsz, seq):
    rows = bsz * seq
    proj = _norm_matmul(x, lp["g_mix"], lp["w_in"], tm=2048 if rows % 2048 == 0 else rows, tn=1152, name="in_proj")
    o_da = _da_prompt(proj, lp["lam"], lp["sub_g"], bsz, seq, lp["lam_init"])
    o_ssm, ssm = _ssd_prompt(proj, lp["ssm_cw"], lp["ssm_cb"], lp["bias_row"], lp["alog_row"], lp["ssm_dsk"],
                             lp["ssm_ng"], bsz, seq)
    o_gdn, gdn = _gdn_prompt(proj, lp["gdn_cw"], lp["bias_row"], lp["alog_row"], lp["gdn_ng"], bsz, seq)
    o_ml, ml_c, ml_n, ml_m = _mlstm_prompt(proj, lp["bias_row"], lp["ml_ng"], bsz, seq)
    x = _merge(x, (o_da, o_ssm, o_gdn, o_ml), proj, lp["w_branch"], lp["w_out"], tm=256)
    x = _cross_prompt(x, lp["g_cross"], mkv, lp["w_cq"], lp["w_co"], bsz, seq, tq=min(512, seq))
    x = _swiglu(x, lp["g_ffn"], lp["w_gu"], lp["w_down"], tm=1024 if rows % 1024 == 0 else 256)
    p3 = proj.reshape(bsz, seq, PACK_W)
    new = dict(
        k=p3[:, :, P_K:P_K + BR_W].reshape(bsz, seq, DA_HEADS, 2, DA_HEAD),
        v=p3[:, :, P_V:P_V + BR_W].reshape(bsz, seq, DA_HEADS, 2 * DA_HEAD),
        ssm_conv=p3[:, seq - (CONV_W - 1):, P_XBC:P_XBC + SSM_CONV_CH], ssm=ssm,
        gdn_conv=p3[:, seq - (CONV_W - 1):, P_GQKV:P_GQKV + GDN_CONV_CH], gdn=gdn,
        ml_c=ml_c, ml_n=ml_n, ml_m=ml_m[:, :ML_HEADS, 0])
    return x, new


def _sample_layer(x, lp, l, caches, states):
    db = x.shape[0]
    cache_k, cache_v, page_table, mem_k, mem_v = caches
    proj = _norm_matmul(x, lp["g_mix"], lp["w_in"], tm=db, tn=1152, name="in_proj_s")
    proj3 = proj.reshape(db, 1, PACK_W)
    o_da = _da_decode(proj3, lp["lam"], lp["sub_g"], cache_k, cache_v, page_table, l, lp["lam_init"])
    o_ssm, ssm_conv, ssm = _ssd_step(proj, states["ssm_conv"], states["ssm"], l, lp["ssm_cw"], lp["ssm_cb"],
                                     lp["bias_row"], lp["alog_row"], lp["ssm_dsk"], lp["ssm_ng"])
    o_gdn, gdn_conv, gdn = _gdn_step(proj, states["gdn_conv"], states["gdn"], l, lp["gdn_cw"],
                                     lp["bias_row"], lp["alog_row"], lp["gdn_ng"])
    o_ml, ml_c, ml_n, ml_m = _mlstm_step(proj, states["ml_c"], states["ml_n"], states["ml_m"], l,
                                         lp["bias_row"], lp["ml_ng"])
    x = _merge(x, (o_da, o_ssm, o_gdn, o_ml), proj, lp["w_branch"], lp["w_out"], tm=db)
    q = _norm_matmul(x, lp["g_cross"], lp["w_cq"], tm=db, tn=D_MODEL, name="cross_q_s")
    att = _cross_decode(q.reshape(db, 1, D_MODEL), mem_k, mem_v, l)
    x = _matmul_residual(x, att.reshape(db, D_MODEL), lp["w_co"], tm=db, name="cross_o_s")
    x = _swiglu(x, lp["g_ffn"], lp["w_gu"], lp["w_down"], tm=db)
    new = dict(
        k=proj[:, P_K:P_K + BR_W].reshape(db, 1, DA_HEADS, 2, DA_HEAD),
        v=proj[:, P_V:P_V + BR_W].reshape(db, 1, DA_HEADS, 2 * DA_HEAD),
        ssm_conv=ssm_conv.transpose(1, 0, 2), ssm=ssm.reshape(db, SSM_HEADS, SSM_HEAD, SSM_STATE),
        gdn_conv=gdn_conv.transpose(1, 0, 2), gdn=gdn,
        ml_c=ml_c, ml_n=ml_n.transpose(1, 0, 2), ml_m=ml_m)
    return x, new


_STATE_ORDER = ("ssm_conv", "ssm", "gdn_conv", "gdn", "ml_c", "ml_n", "ml_m")


def kernel(x_prompt, x_sample, cache_k, cache_v, cache_mem_k, cache_mem_v, state_ssm_conv, state_ssm, state_gdn_conv, state_gdn, state_mlstm_c, state_mlstm_n, state_mlstm_m, page_table, mem_prompt, g_mix, w_in, da_lq1, da_lk1, da_lq2, da_lk2, da_sub_g, ssm_conv_w, ssm_conv_b, ssm_dt_bias, ssm_a_log, ssm_d, ssm_norm_g, gdn_conv_w, gdn_dt_bias, gdn_a_log, gdn_norm_g, ml_i_bias, ml_f_bias, ml_norm_g, w_branch, w_out, g_cross, g_mem, w_cq, w_ckv, w_co, g_ffn, w_gu, w_down, g_final):
    p = dict(g_mix=g_mix, w_in=w_in, da_lq1=da_lq1, da_lk1=da_lk1, da_lq2=da_lq2, da_lk2=da_lk2,
             da_sub_g=da_sub_g, ssm_conv_w=ssm_conv_w, ssm_conv_b=ssm_conv_b, ssm_dt_bias=ssm_dt_bias,
             ssm_a_log=ssm_a_log, ssm_d=ssm_d, ssm_norm_g=ssm_norm_g, gdn_conv_w=gdn_conv_w,
             gdn_dt_bias=gdn_dt_bias, gdn_a_log=gdn_a_log, gdn_norm_g=gdn_norm_g,
             ml_i_bias=ml_i_bias, ml_f_bias=ml_f_bias, ml_norm_g=ml_norm_g,
             w_branch=w_branch, w_out=w_out, g_cross=g_cross, w_cq=w_cq, w_co=w_co,
             g_ffn=g_ffn, w_gu=w_gu, w_down=w_down)
    depth = w_in.shape[0]
    bsz, seq, d = x_prompt.shape
    db = x_sample.shape[0]
    n_mem = mem_prompt.shape[1]
    lps = [_layer_params(l, p) for l in range(depth)]

    mem2 = mem_prompt.reshape(bsz * n_mem, d)
    xp = x_prompt.reshape(bsz * seq, d)
    p_new = {n: [] for n in ("k", "v", "mem_k", "mem_v") + _STATE_ORDER}
    for l in range(depth):
        mkv = _norm_matmul(mem2, g_mem[l], w_ckv[l].astype(BF16), tm=min(1024, bsz * n_mem), tn=1024, name="mem_kv")
        xp, new = _prompt_layer(xp, lps[l], mkv, bsz, seq)
        mkv5 = mkv.reshape(bsz, n_mem, 2, X_HEADS, X_HEAD)
        new["mem_k"] = mkv5[:, :, 0]
        new["mem_v"] = mkv5[:, :, 1]
        for n in p_new:
            p_new[n].append(new[n])
    y_prompt = _final_norm(xp, g_final, tm=512 if (bsz * seq) % 512 == 0 else bsz * seq, name="final_norm").reshape(bsz, seq, d)

    n_pool = cache_k.shape[1]
    caches = (cache_k.transpose(0, 1, 3, 4, 5, 2).reshape(depth, n_pool, BR_W, PAGE_SIZE),
              cache_v.reshape(depth, n_pool, PAGE_SIZE * DA_HEADS, 2 * DA_HEAD),
              page_table, _mem_rows(cache_mem_k), _mem_rows(cache_mem_v))
    states = dict(ssm_conv=state_ssm_conv.transpose(0, 2, 1, 3),
                  ssm=state_ssm.reshape(depth, db, SSM_HEADS // 2, 2 * SSM_HEAD, SSM_STATE),
                  gdn_conv=state_gdn_conv.transpose(0, 2, 1, 3), gdn=state_gdn,
                  ml_c=state_mlstm_c, ml_n=state_mlstm_n.reshape(depth, db * ML_HEADS, ML_HEAD), ml_m=state_mlstm_m)
    xs = x_sample.reshape(db, d)
    s_new = {n: [] for n in ("k", "v") + _STATE_ORDER}
    for l in range(depth):
        xs, new = _sample_layer(xs, lps[l], l, caches, states)
        for n in s_new:
            s_new[n].append(new[n])
    y_sample = _final_norm(xs, g_final, tm=db, name="final_norm_s").reshape(db, 1, d)

    stk = lambda dct, n: jnp.stack(dct[n])
    return (y_prompt, y_sample,
            stk(p_new, "k"), stk(p_new, "v"), stk(p_new, "mem_k"), stk(p_new, "mem_v"),
            *(stk(p_new, n) for n in _STATE_ORDER),
            stk(s_new, "k"), stk(s_new, "v"), *(stk(s_new, n) for n in _STATE_ORDER))
```

```python
import functools
import math

import numpy as np
import jax
import jax.numpy as jnp
from jax import lax
from jax.experimental import pallas as pl
from jax.experimental.pallas import tpu as pltpu

F32 = jnp.float32
BF16 = jnp.bfloat16

D_MODEL = 1024
DEPTH = 4
PAGE_SIZE = 128
EPS = 1e-6
N_MEM = 256
CONV_W = 4
N_BRANCH = 4
BR_W = D_MODEL // 2
DA_HEADS = 4
DA_HEAD = BR_W // (2 * DA_HEADS)
SSM_HEAD = 64
SSM_HEADS = BR_W // SSM_HEAD
SSM_GROUPS = 2
SSM_STATE = 128
SSM_CONV_CH = BR_W + 2 * SSM_GROUPS * SSM_STATE
SSM_CHUNK = 128
GDN_HEADS = 4
GDN_HEAD = BR_W // GDN_HEADS
GDN_CONV_CH = 3 * BR_W
GDN_CHUNK = 64
ML_HEADS = 4
ML_HEAD = BR_W // ML_HEADS
ML_CHUNK = 128
X_HEADS = 4
X_HEAD = D_MODEL // X_HEADS
D_FF = -(-8 * D_MODEL // (3 * 256)) * 256

IN_SPLITS = (BR_W, BR_W, BR_W, BR_W, SSM_CONV_CH, SSM_HEADS, GDN_CONV_CH, BR_W, GDN_HEADS, GDN_HEADS,
             BR_W, BR_W, BR_W, BR_W, ML_HEADS, ML_HEADS, N_BRANCH * D_MODEL)

P_Q, P_K, P_V = 0, BR_W, 2 * BR_W
P_SZ = 3 * BR_W
P_XBC = P_SZ + BR_W
P_GQKV = P_XBC + SSM_CONV_CH
P_GZ = P_GQKV + GDN_CONV_CH
P_MQ = P_GZ + BR_W
P_GATE = P_MQ + 4 * BR_W
P_SMALL = P_GATE + N_BRANCH * D_MODEL
SMALL_W = 256
PACK_W = P_SMALL + SMALL_W
S_DT, S_GA, S_GB, S_MI, S_MF = 0, 8, 12, 16, 20

LANE = 128
VMEM_LIMIT = 56 * 1024 * 1024


def _cparams(*sem):
    return pltpu.CompilerParams(dimension_semantics=sem, vmem_limit_bytes=VMEM_LIMIT)


def _dot(a, b):
    return jnp.dot(a.astype(BF16), b.astype(BF16), preferred_element_type=F32)


def _dot_nt(a, b):
    return lax.dot_general(a.astype(BF16), b.astype(BF16), (((1,), (1,)), ((), ())), preferred_element_type=F32)


def _dot_f32(a, b):
    return jnp.dot(a, b, precision=lax.Precision.HIGHEST, preferred_element_type=F32)


def _sigmoid(x):
    return 1.0 / (1.0 + jnp.exp(-x))


def _silu(x):
    return x * _sigmoid(x)


def _softplus(x):
    return jnp.maximum(x, 0.0) + jnp.log(1.0 + jnp.exp(-jnp.abs(x)))


def _rms(x, g):
    return x * lax.rsqrt(jnp.mean(x * x, axis=-1, keepdims=True) + EPS) * g


def _iota2(shape, dim):
    return lax.broadcasted_iota(jnp.int32, shape, dim)


def _tril_f32(c):
    return (_iota2((c, c), 0) >= _iota2((c, c), 1)).astype(F32)


def _head_expander(n_heads, width):
    rows = _iota2((LANE, n_heads * width), 0)
    cols = _iota2((LANE, n_heads * width), 1)
    return (rows * width <= cols) & (cols < (rows + 1) * width)


def _lam(lq1, lk1, lq2, lk2, lam_init):
    return (jnp.exp(jnp.sum(lq1[...] * lk1[...], axis=1, keepdims=True))
            - jnp.exp(jnp.sum(lq2[...] * lk2[...], axis=1, keepdims=True)) + lam_init)


def _norm_matmul_kernel(x_ref, g_ref, w_ref, o_ref, h_ref):
    @pl.when(pl.program_id(1) == 0)
    def _():
        h_ref[...] = _rms(x_ref[...], g_ref[...]).astype(BF16)

    o_ref[...] = jnp.dot(h_ref[...], w_ref[...], preferred_element_type=F32)


def _norm_matmul(x, g, w, tm, tn, name):
    m, k = x.shape
    n = w.shape[1]
    return pl.pallas_call(
        _norm_matmul_kernel,
        grid=(m // tm, n // tn),
        in_specs=[pl.BlockSpec((tm, k), lambda i, j: (i, 0)),
                  pl.BlockSpec((1, k), lambda i, j: (0, 0)),
                  pl.BlockSpec((k, tn), lambda i, j: (0, j))],
        out_specs=pl.BlockSpec((tm, tn), lambda i, j: (i, j)),
        out_shape=jax.ShapeDtypeStruct((m, n), F32),
        scratch_shapes=[pltpu.VMEM((tm, k), BF16)],
        compiler_params=_cparams("parallel", "arbitrary"),
        name=name)(x, g.reshape(1, k), w)


def _matmul_res_kernel(x_ref, a_ref, w_ref, o_ref):
    o_ref[...] = x_ref[...] + jnp.dot(a_ref[...].astype(BF16), w_ref[...], preferred_element_type=F32)


def _matmul_residual(x, a, w, tm, name):
    m, n = x.shape
    k = a.shape[1]
    return pl.pallas_call(
        _matmul_res_kernel,
        grid=(m // tm,),
        in_specs=[pl.BlockSpec((tm, n), lambda i: (i, 0)),
                  pl.BlockSpec((tm, k), lambda i: (i, 0)),
                  pl.BlockSpec((k, n), lambda i: (0, 0))],
        out_specs=pl.BlockSpec((tm, n), lambda i: (i, 0)),
        out_shape=jax.ShapeDtypeStruct((m, n), F32),
        compiler_params=_cparams("parallel"),
        name=name)(x, a, w)


def _final_norm_kernel(x_ref, g_ref, o_ref):
    o_ref[...] = _rms(x_ref[...], g_ref[...])


def _final_norm(x, g, tm, name):
    m, n = x.shape
    return pl.pallas_call(
        _final_norm_kernel,
        grid=(m // tm,),
        in_specs=[pl.BlockSpec((tm, n), lambda i: (i, 0)), pl.BlockSpec((1, n), lambda i: (0, 0))],
        out_specs=pl.BlockSpec((tm, n), lambda i: (i, 0)),
        out_shape=jax.ShapeDtypeStruct((m, n), F32),
        compiler_params=_cparams("parallel"),
        name=name)(x, g.reshape(1, n))


def _da_prompt_kernel(lq1, lk1, lq2, lk2, subg_ref, q_ref, k_ref, v_ref, o_ref, kb_ref, vt_ref, acc_ref,
                      *, lam_init, tq, cw):
    hw = 2 * DA_HEAD
    nq = vt_ref.shape[0]
    kb_ref[...] = k_ref[...].astype(BF16)
    for t in range(nq):
        vt_ref[t] = v_ref[t * tq:(t + 1) * tq, :].T.astype(BF16)
    lam = _lam(lq1, lk1, lq2, lk2, lam_init)
    subg = subg_ref[...]
    sub = _iota2((hw, tq), 0)
    nch = 2 * tq // cw
    chunks = [slice(c * cw, (c + 1) * cw) for c in range(nch)]

    def scores(j, q2t):
        kj = kb_ref[pl.ds(pl.multiple_of(j * tq, tq), tq), :]
        return tuple(jnp.dot(kj, q2t[:, cols], preferred_element_type=F32) for cols in chunks)

    def update(j, sts, stats, masked):
        new_stats, scaled = [], []
        for c, st in enumerate(sts):
            m, l = stats[c]
            if masked:
                qpos = _iota2((tq, cw), 1) + (c * cw) % tq
                st = jnp.where(_iota2((tq, cw), 0) <= qpos, st, -jnp.inf)
            m_new = jnp.maximum(m, jnp.max(st, axis=0, keepdims=True))
            alpha = jnp.exp(m - m_new)
            p = jnp.exp(st - m_new)
            new_stats.append((m_new, alpha * l + jnp.sum(p, axis=0, keepdims=True)))
            scaled.append((alpha, p.astype(BF16)))
        vtj = vt_ref[j]
        for cols, (alpha, p) in zip(chunks, scaled):
            acc_ref[:, cols] = alpha * acc_ref[:, cols] + jnp.dot(vtj, p, preferred_element_type=F32)
        return tuple(new_stats)

    def q_tile(qi, carry):
        rows = pl.ds(pl.multiple_of(qi * tq, tq), tq)
        qt = (q_ref[rows, :] * (DA_HEAD ** -0.5)).T
        q2t = jnp.concatenate([jnp.where(sub < DA_HEAD, qt, 0.0), jnp.where(sub >= DA_HEAD, qt, 0.0)],
                              axis=1).astype(BF16)
        acc_ref[...] = jnp.zeros(acc_ref.shape, F32)

        def body(j, state):
            sts, stats = state
            nxt = scores(j + 1, q2t)
            return nxt, update(j, sts, stats, False)

        init = tuple((jnp.full((1, cw), -jnp.inf, F32), jnp.zeros((1, cw), F32)) for _ in range(nch))
        sts, stats = lax.fori_loop(0, qi, body, (scores(0, q2t), init))
        stats = update(qi, sts, stats, True)
        l = jnp.concatenate([ml[1] for ml in stats], axis=1)
        ot = acc_ref[...] / l
        odt = ot[:, :tq] - lam * ot[:, tq:]
        yt = odt * lax.rsqrt(jnp.mean(odt * odt, axis=0, keepdims=True) + EPS) * subg * (1.0 - lam_init)
        o_ref[rows, :] = yt.T
        return carry

    lax.fori_loop(0, nq, q_tile, 0)


def _da_prompt(proj, lam_params, sub_g, bsz, seq, lam_init, tq=256, cw=128):
    nq = seq // tq
    hw = 2 * DA_HEAD
    small = pl.BlockSpec((1, DA_HEAD), lambda b, h: (0, 0))
    col = lambda off: pl.BlockSpec((seq, hw), lambda b, h: (b, off // hw + h))
    return pl.pallas_call(
        functools.partial(_da_prompt_kernel, lam_init=lam_init, tq=tq, cw=cw),
        grid=(bsz, DA_HEADS),
        in_specs=[small, small, small, small, pl.BlockSpec((hw, 1), lambda b, h: (0, 0)),
                  col(P_Q), col(P_K), col(P_V)],
        out_specs=pl.BlockSpec((seq, hw), lambda b, h: (b, h)),
        out_shape=jax.ShapeDtypeStruct((bsz * seq, BR_W), F32),
        scratch_shapes=[pltpu.VMEM((seq, hw), BF16), pltpu.VMEM((nq, hw, tq), BF16), pltpu.VMEM((hw, 2 * tq), F32)],
        compiler_params=_cparams("parallel", "parallel"),
        name="da_prompt")(*lam_params, sub_g.reshape(hw, 1), proj, proj, proj)


def _da_decode_kernel(pt_ref, lq1, lk1, lq2, lk2, subg_ref, q_ref, kn_ref, vn_ref, *rest, lam_init, n_pages):
    del pt_ref
    kt_refs = rest[:n_pages]
    v_refs = rest[n_pages:2 * n_pages]
    o_ref = rest[2 * n_pages]
    nh = DA_HEADS
    hw = 2 * DA_HEAD
    lam = _lam(lq1, lk1, lq2, lk2, lam_init)
    q = q_ref[...] * (DA_HEAD ** -0.5)
    r = _iota2((2 * nh, BR_W), 0)
    seg = _iota2((2 * nh, BR_W), 1) >> 6
    q_bd = jnp.where(((seg & 1) == (r >> 2)) & ((seg >> 1) == (r & 3)), jnp.broadcast_to(q, (2 * nh, BR_W)), 0.0)
    s_new = jnp.sum(q_bd * kn_ref[...], axis=1, keepdims=True)
    qb = q_bd.astype(BF16)
    s = jnp.concatenate([jnp.dot(qb, kt_refs[j][...].astype(BF16), preferred_element_type=F32)
                         for j in range(n_pages)], axis=1)
    m = jnp.maximum(jnp.max(s, axis=1, keepdims=True), s_new)
    e = jnp.exp(s - m)
    e_new = jnp.exp(s_new - m)
    l = jnp.sum(e, axis=1, keepdims=True) + e_new
    coef = jnp.where(_iota2((2 * nh, 1), 0) < nh, 1.0, -lam) / l
    w = e * coef
    w_new = e_new * coef
    p = (w + pltpu.roll(w, nh, axis=0)).astype(BF16)
    p_new = w_new + pltpu.roll(w_new, nh, axis=0)
    g = subg_ref[...]
    vn = vn_ref[...]
    outs = []
    for h in range(nh):
        v_h = jnp.concatenate([v_refs[j][pl.ds(h, PAGE_SIZE, stride=nh), :].astype(BF16) for j in range(n_pages)],
                              axis=0)
        o_h = (jnp.dot(p, v_h, preferred_element_type=F32)[h:h + 1, :]
               + p_new[h:h + 1, :] * vn[:, h * hw:(h + 1) * hw])
        outs.append(_rms(o_h, g) * (1.0 - lam_init))
    o_ref[...] = jnp.concatenate(outs, axis=1)


def _da_decode(proj3, lam_params, sub_g, cache_kt, cache_v, page_table, layer, lam_init):
    db = proj3.shape[0]
    n_pages = page_table.shape[1]
    hw = 2 * DA_HEAD
    small = pl.BlockSpec((1, DA_HEAD), lambda b, pt: (0, 0))
    row = lambda off: pl.BlockSpec((None, 1, BR_W), lambda b, pt: (b, 0, off // BR_W))

    def page(j):
        return pl.BlockSpec((None, None, BR_W, PAGE_SIZE), lambda b, pt: (layer, pt[b * n_pages + j], 0, 0))

    grid_spec = pltpu.PrefetchScalarGridSpec(
        num_scalar_prefetch=1,
        grid=(db,),
        in_specs=[small, small, small, small, pl.BlockSpec((1, hw), lambda b, pt: (0, 0)),
                  row(P_Q), row(P_K), row(P_V)]
                 + [page(j) for j in range(n_pages)] + [page(j) for j in range(n_pages)],
        out_specs=pl.BlockSpec((None, 1, BR_W), lambda b, pt: (b, 0, 0)))
    out = pl.pallas_call(
        functools.partial(_da_decode_kernel, lam_init=lam_init, n_pages=n_pages),
        grid_spec=grid_spec,
        out_shape=jax.ShapeDtypeStruct((db, 1, BR_W), F32),
        compiler_params=_cparams("parallel"),
        name="da_decode")(page_table.reshape(-1), *lam_params, sub_g.reshape(1, hw), proj3, proj3, proj3,
                          *([cache_kt] * n_pages), *([cache_v] * n_pages))
    return out.reshape(db, BR_W)


def _conv_window(win_ref, x_ref, cw, c, zi):
    @pl.when(zi == 0)
    def _():
        win_ref[0:8, :] = jnp.zeros((8, win_ref.shape[1]), F32)

    @pl.when(zi > 0)
    def _():
        win_ref[0:8, :] = win_ref[c:c + 8, :]

    win_ref[8:8 + c, :] = x_ref[...]
    y = win_ref[5:5 + c, :] * cw[0:1, :]
    for j in range(1, CONV_W):
        y = y + win_ref[5 + j:5 + j + c, :] * cw[j:j + 1, :]
    return y


def _ssd_prompt_kernel(z_ref, xbc_ref, sm_ref, cw_ref, cb_ref, bias_ref, alog_ref, dsk_ref, ng_ref,
                       o_ref, st_ref, win_ref, yd_ref, yo_ref, *, c, nsub):
    zi = pl.program_id(1)
    rows = c * nsub
    rs = lambda s: slice(s * c, (s + 1) * c)

    @pl.when(zi == 0)
    def _():
        st_ref[...] = jnp.zeros(st_ref.shape, F32)

    xbc = _silu(_conv_window(win_ref, xbc_ref, cw_ref[...], rows, zi) + cb_ref[...])
    xs = xbc[:, :BR_W]
    gs = SSM_GROUPS * SSM_STATE
    bm = xbc[:, BR_W:BR_W + gs]
    cm = xbc[:, BR_W + gs:]
    lane = _iota2((1, LANE), 1)
    head_lane = lane < SSM_HEADS
    dt = _softplus(sm_ref[:, :LANE] + bias_ref[:, :LANE])
    a = jnp.where(head_lane, -jnp.exp(alog_ref[:, :LANE]), 0.0)
    dt = jnp.where(head_lane, dt, 0.0)
    tril = _tril_f32(c)
    acs = [_dot_f32(tril, (dt * a)[rs(s), :]) for s in range(nsub)]
    acs_t = [t.T for t in acs]
    acs_last = [t[c - 1:c, :] for t in acs]
    expander = _head_expander(SSM_HEADS, SSM_HEAD).astype(F32)
    xdt = xs * _dot_f32(dt, expander)
    w_t = [(xdt[rs(s), :] * _dot_f32(jnp.exp(acs_last[s] - acs[s]), expander)).T for s in range(nsub)]
    causal = _iota2((c, c), 0) >= _iota2((c, c), 1)
    rep = SSM_HEADS // SSM_GROUPS
    heads = range(SSM_HEADS)
    units = [(s, h) for s in range(nsub) for h in heads]
    sl = [slice(h * SSM_HEAD, (h + 1) * SSM_HEAD) for h in heads]
    bm_g = {(s, g): bm[rs(s), g * SSM_STATE:(g + 1) * SSM_STATE].astype(BF16)
            for s in range(nsub) for g in range(SSM_GROUPS)}
    cm_g = {(s, g): cm[rs(s), g * SSM_STATE:(g + 1) * SSM_STATE].astype(BF16)
            for s in range(nsub) for g in range(SSM_GROUPS)}
    cb = {sg: _dot_nt(cm_g[sg], bm_g[sg]) for sg in bm_g}
    st_add = {(s, h): _dot(w_t[s][sl[h], :], bm_g[s, h // rep]) for s, h in units}
    decay = {(s, h): jnp.exp(jnp.where(causal, acs[s][:, h:h + 1] - acs_t[s][h:h + 1, :], -jnp.inf)) for s, h in units}
    for s, h in units:
        yd_ref[rs(s), sl[h]] = _dot(cb[s, h // rep] * decay[s, h], xdt[rs(s), sl[h]])
    h_prev = [st_ref[h] for h in heads]
    for s in range(nsub):
        for h in heads:
            yo_ref[rs(s), sl[h]] = _dot_nt(cm_g[s, h // rep], h_prev[h])
        h_prev = [h_prev[h] * jnp.exp(acs_last[s][:, h:h + 1]) + st_add[s, h] for h in heads]
    for h in heads:
        st_ref[h] = h_prev[h]
    decay_in = jnp.concatenate([_dot_f32(jnp.exp(acs[s]), expander) for s in range(nsub)], axis=0)
    y = yd_ref[...] + yo_ref[...] * decay_in + dsk_ref[...] * xs
    y = y * _silu(z_ref[...])
    gw = BR_W // SSM_GROUPS
    ng = ng_ref[...]
    o_ref[...] = jnp.concatenate([_rms(y[:, g * gw:(g + 1) * gw], ng[:, g * gw:(g + 1) * gw])
                                  for g in range(SSM_GROUPS)], axis=1)


def _full(shape):
    return pl.BlockSpec(shape, lambda *a: (0,) * len(shape))


SSD_KERNEL_NSUB = 4


def _ssd_prompt(proj, cw, cb, bias_row, alog_row, dsk, ng, bsz, seq):
    c = SSM_CHUNK * SSD_KERNEL_NSUB
    nc = seq // c
    return pl.pallas_call(
        functools.partial(_ssd_prompt_kernel, c=SSM_CHUNK, nsub=SSD_KERNEL_NSUB),
        grid=(bsz, nc),
        in_specs=[pl.BlockSpec((c, BR_W), lambda b, z: (b * nc + z, P_SZ // BR_W)),
                  pl.BlockSpec((c, SSM_CONV_CH), lambda b, z: (b * nc + z, P_XBC // SSM_CONV_CH)),
                  pl.BlockSpec((c, SMALL_W), lambda b, z: (b * nc + z, P_SMALL // SMALL_W)),
                  _full((CONV_W, SSM_CONV_CH)), _full((1, SSM_CONV_CH)), _full((1, SMALL_W)), _full((1, SMALL_W)),
                  _full((1, BR_W)), _full((1, BR_W))],
        out_specs=[pl.BlockSpec((c, BR_W), lambda b, z: (b * nc + z, 0)),
                   pl.BlockSpec((None, SSM_HEADS, SSM_HEAD, SSM_STATE), lambda b, z: (b, 0, 0, 0))],
        out_shape=[jax.ShapeDtypeStruct((bsz * seq, BR_W), F32),
                   jax.ShapeDtypeStruct((bsz, SSM_HEADS, SSM_HEAD, SSM_STATE), F32)],
        scratch_shapes=[pltpu.VMEM((c + 8, SSM_CONV_CH), F32), pltpu.VMEM((c, BR_W), F32), pltpu.VMEM((c, BR_W), F32)],
        compiler_params=_cparams("parallel", "arbitrary"),
        name="ssd_prompt")(proj, proj, proj, cw, cb, bias_row, alog_row, dsk, ng)


def _inv_unit_lower_minus_eye(mats, n, c):
    row = _iota2((n, n), 0)
    col = _iota2((n, n), 1)
    sh = 4
    ps = [jnp.where((row >> sh) == (col >> sh), a, 0.0) for a in mats]
    ys = [-p for p in ps]
    for _ in range(sh - 1):
        ps = [_dot(p, p) for p in ps]
        yp = [_dot(y, p) for y, p in zip(ys, ps)]
        ys = [y + p + t for y, p, t in zip(ys, ps, yp)]
    while (1 << sh) < c:
        mask = ((row >> (sh + 1)) == (col >> (sh + 1))) & ((row >> sh) != (col >> sh))
        offs = [jnp.where(mask, a, 0.0) for a in mats]
        ts = [off + _dot(y, off) for y, off in zip(ys, offs)]
        ty = [_dot(t, y) for t, y in zip(ts, ys)]
        ys = [y - (t + u) for y, t, u in zip(ys, ts, ty)]
        sh += 1
    return ys


def _gdn_prompt_kernel(qkv_ref, z_ref, sm_ref, cw_ref, bias_ref, alog_ref, ng_ref, o_ref, st_ref, win_ref, *, c, nsub):
    zi = pl.program_id(1)
    hd = GDN_HEAD
    heads = range(GDN_HEADS)
    rows = c * nsub
    units = [(s, h) for s in range(nsub) for h in heads]
    rs = lambda s: slice(s * c, (s + 1) * c)

    @pl.when(zi == 0)
    def _():
        st_ref[...] = jnp.zeros(st_ref.shape, F32)

    qkv = _silu(_conv_window(win_ref, qkv_ref, cw_ref[...], rows, zi))
    pre = sm_ref[:, :LANE] + bias_ref[:, :LANE]
    g_all = -jnp.exp(alog_ref[:, :LANE]) * _softplus(pre)
    beta_all = _sigmoid(sm_ref[:, :LANE])
    tril = _tril_f32(c)
    gc = [_dot_f32(tril, g_all[rs(s), :]) for s in range(nsub)]
    gc_t = [g.T for g in gc]
    row = _iota2((c, c), 0)
    col = _iota2((c, c), 1)
    l2 = lambda t: t * lax.rsqrt(jnp.sum(t * t, axis=-1, keepdims=True) + EPS)

    q = {(s, h): l2(qkv[rs(s), h * hd:(h + 1) * hd]) * (hd ** -0.5) for s, h in units}
    k = {(s, h): l2(qkv[rs(s), BR_W + h * hd:BR_W + (h + 1) * hd]) for s, h in units}
    v = {(s, h): qkv[rs(s), 2 * BR_W + h * hd:2 * BR_W + (h + 1) * hd] for s, h in units}
    g_col = {(s, h): gc[s][:, S_GA + h:S_GA + h + 1] for s, h in units}
    g_last = {(s, h): gc[s][c - 1:c, S_GA + h:S_GA + h + 1] for s, h in units}
    beta = {(s, h): beta_all[rs(s), S_GB + h:S_GB + h + 1] for s, h in units}
    decay = {(s, h): jnp.exp(jnp.where(row >= col, g_col[s, h] - gc_t[s][S_GA + h:S_GA + h + 1, :], -jnp.inf))
             for s, h in units}
    kb = {u: k[u] * beta[u] for u in units}
    eg = {u: jnp.exp(g_col[u]) for u in units}
    a_low = [jnp.where(row > col, _dot_nt(kb[u], k[u]) * decay[u], 0.0) for u in units]
    attn = {u: _dot_nt(q[u], k[u]) * decay[u] for u in units}
    t_dev = dict(zip(units, _inv_unit_lower_minus_eye(a_low, c, c)))
    vb = {u: v[u] * beta[u] for u in units}
    kbe = {u: kb[u] * eg[u] for u in units}
    u_mat = {u: vb[u] + _dot(t_dev[u], vb[u]) for u in units}
    w_mat = {u: kbe[u] + _dot(t_dev[u], kbe[u]) for u in units}
    kg_t = {u: (k[u] * jnp.exp(g_last[u] - g_col[u])).T for u in units}
    ng = ng_ref[...]
    state = [st_ref[h] for h in heads]
    for s in range(nsub):
        v_new = [u_mat[s, h] - _dot(w_mat[s, h], state[h]) for h in heads]
        o = [_dot(q[s, h] * eg[s, h], state[h]) + _dot(attn[s, h], v_new[h]) for h in heads]
        state = [state[h] * jnp.exp(g_last[s, h]) + _dot(kg_t[s, h], v_new[h]) for h in heads]
        for h in heads:
            o_ref[rs(s), h * hd:(h + 1) * hd] = _rms(o[h], ng) * _silu(z_ref[rs(s), h * hd:(h + 1) * hd])
    for h in heads:
        st_ref[h] = state[h]


GDN_KERNEL_CHUNK = 128
GDN_KERNEL_NSUB = 4


def _gdn_prompt(proj, cw, bias_row, alog_row, ng, bsz, seq):
    rows = GDN_KERNEL_CHUNK * GDN_KERNEL_NSUB
    nc = seq // rows
    return pl.pallas_call(
        functools.partial(_gdn_prompt_kernel, c=GDN_KERNEL_CHUNK, nsub=GDN_KERNEL_NSUB),
        grid=(bsz, nc),
        in_specs=[pl.BlockSpec((rows, GDN_CONV_CH), lambda b, z: (b * nc + z, P_GQKV // GDN_CONV_CH)),
                  pl.BlockSpec((rows, BR_W), lambda b, z: (b * nc + z, P_GZ // BR_W)),
                  pl.BlockSpec((rows, SMALL_W), lambda b, z: (b * nc + z, P_SMALL // SMALL_W)),
                  _full((CONV_W, GDN_CONV_CH)), _full((1, SMALL_W)), _full((1, SMALL_W)), _full((1, GDN_HEAD))],
        out_specs=[pl.BlockSpec((rows, BR_W), lambda b, z: (b * nc + z, 0)),
                   pl.BlockSpec((None, GDN_HEADS, GDN_HEAD, GDN_HEAD), lambda b, z: (b, 0, 0, 0))],
        out_shape=[jax.ShapeDtypeStruct((bsz * seq, BR_W), F32),
                   jax.ShapeDtypeStruct((bsz, GDN_HEADS, GDN_HEAD, GDN_HEAD), F32)],
        scratch_shapes=[pltpu.VMEM((rows + 8, GDN_CONV_CH), F32)],
        compiler_params=_cparams("parallel", "arbitrary"),
        name="gdn_prompt")(proj, proj, proj, cw, bias_row, alog_row, ng)


def _mlstm_prompt_kernel(q_ref, k_ref, v_ref, og_ref, sm_ref, bias_ref, ng_ref, o_ref, c_ref, n_ref, m_ref,
                         *, c, nsub):
    zi = pl.program_id(1)

    @pl.when(zi == 0)
    def _():
        c_ref[...] = jnp.zeros(c_ref.shape, F32)
        n_ref[...] = jnp.zeros(n_ref.shape, F32)
        m_ref[...] = jnp.zeros(m_ref.shape, F32)

    pre = sm_ref[:, :LANE] + bias_ref[:, :LANE]
    logf = -_softplus(-pre)
    rs = lambda s: slice(s * c, (s + 1) * c)
    tril = _tril_f32(c)
    bcum = [_dot_f32(tril, logf[rs(s), :]) for s in range(nsub)]
    bcum_t = [b.T for b in bcum]
    pre_t = [pre[rs(s), :].T for s in range(nsub)]
    causal = _iota2((c, c), 0) >= _iota2((c, c), 1)
    ng = ng_ref[...]
    hd = ML_HEAD
    heads = range(ML_HEADS)
    units = [(s, h) for s in range(nsub) for h in heads]
    sl = [slice(h * hd, (h + 1) * hd) for h in heads]
    q = {(s, h): q_ref[rs(s), sl[h]] for s, h in units}
    k = {(s, h): k_ref[rs(s), sl[h]] * (hd ** -0.5) for s, h in units}
    v = {(s, h): v_ref[rs(s), sl[h]] for s, h in units}
    qk = {u: _dot_nt(q[u], k[u]) for u in units}
    b_col = {(s, h): bcum[s][:, S_MF + h:S_MF + h + 1] for s, h in units}
    b_last = {u: b_col[u][c - 1:c, :] for u in units}
    i_col = {(s, h): pre[rs(s), S_MI + h:S_MI + h + 1] for s, h in units}
    dmat = {(s, h): jnp.where(causal, b_col[s, h] - bcum_t[s][S_MF + h:S_MF + h + 1, :]
                              + pre_t[s][S_MI + h:S_MI + h + 1, :], -jnp.inf) for s, h in units}
    dmax = {u: jnp.max(dmat[u], axis=1, keepdims=True) for u in units}
    c_prev = [c_ref[h] for h in heads]
    n_prev = [n_ref[h:h + 1, :] for h in heads]
    m_prev = [m_ref[h:h + 1, 0:1] for h in heads]
    for s in range(nsub):
        qc = [_dot(q[s, h], c_prev[h]) for h in heads]
        m_t = [jnp.maximum(b_col[s, h] + m_prev[h], dmax[s, h]) for h in heads]
        m_new = [m_t[h][c - 1:c, :] for h in heads]
        smat = [qk[s, h] * jnp.exp(dmat[s, h] - m_t[h]) for h in heads]
        kw = [k[s, h] * jnp.exp(b_last[s, h] - b_col[s, h] + i_col[s, h] - m_new[h]) for h in heads]
        sv = [_dot(smat[h], v[s, h]) for h in heads]
        kv = [_dot(kw[h].T, v[s, h]) for h in heads]
        for h in heads:
            w_prev = jnp.exp(b_col[s, h] + m_prev[h] - m_t[h])
            num = w_prev * qc[h] + sv[h]
            den = (w_prev * jnp.sum(q[s, h] * n_prev[h], axis=1, keepdims=True)
                   + jnp.sum(smat[h], axis=1, keepdims=True))
            hid = num / jnp.maximum(jnp.abs(den), jnp.exp(-m_t[h]))
            o_ref[rs(s), sl[h]] = _rms(hid, ng) * _sigmoid(og_ref[rs(s), sl[h]])
        w_c = [jnp.exp(b_last[s, h] + m_prev[h] - m_new[h]) for h in heads]
        c_prev = [c_prev[h] * w_c[h] + kv[h] for h in heads]
        n_prev = [n_prev[h] * w_c[h] + jnp.sum(kw[h], axis=0, keepdims=True) for h in heads]
        m_prev = m_new
    for h in heads:
        c_ref[h] = c_prev[h]
        n_ref[h:h + 1, :] = n_prev[h]
        m_ref[h:h + 1, :] = jnp.broadcast_to(m_prev[h], (1, LANE))


ML_KERNEL_NSUB = 4


def _mlstm_prompt(proj, bias_row, ng, bsz, seq):
    nsub = ML_KERNEL_NSUB
    c = ML_CHUNK * nsub
    nc = seq // c
    col = lambda i: pl.BlockSpec((c, BR_W), lambda b, z: (b * nc + z, P_MQ // BR_W + i))
    return pl.pallas_call(
        functools.partial(_mlstm_prompt_kernel, c=ML_CHUNK, nsub=nsub),
        grid=(bsz, nc),
        in_specs=[col(0), col(1), col(2), col(3),
                  pl.BlockSpec((c, SMALL_W), lambda b, z: (b * nc + z, P_SMALL // SMALL_W)),
                  _full((1, SMALL_W)), _full((1, ML_HEAD))],
        out_specs=[pl.BlockSpec((c, BR_W), lambda b, z: (b * nc + z, 0)),
                   pl.BlockSpec((None, ML_HEADS, ML_HEAD, ML_HEAD), lambda b, z: (b, 0, 0, 0)),
                   pl.BlockSpec((None, ML_HEADS, ML_HEAD), lambda b, z: (b, 0, 0)),
                   pl.BlockSpec((None, 8, LANE), lambda b, z: (b, 0, 0))],
        out_shape=[jax.ShapeDtypeStruct((bsz * seq, BR_W), F32),
                   jax.ShapeDtypeStruct((bsz, ML_HEADS, ML_HEAD, ML_HEAD), F32),
                   jax.ShapeDtypeStruct((bsz, ML_HEADS, ML_HEAD), F32),
                   jax.ShapeDtypeStruct((bsz, 8, LANE), F32)],
        compiler_params=_cparams("parallel", "arbitrary"),
        name="mlstm_prompt")(proj, proj, proj, proj, proj, bias_row, ng)


STEP_ROWS = 8


def _split_f32(x):
    hi = x.astype(BF16).astype(F32)
    return hi, (x - hi).astype(BF16).astype(F32)


def _outer_lhs(rows):
    hi, lo = _split_f32(rows)
    return jnp.concatenate([hi, hi, lo, jnp.zeros_like(hi)], axis=0).T.astype(BF16)


def _outer_rhs(rows, r):
    keep = _iota2(rows.shape, 0) == r
    hi, lo = _split_f32(jnp.where(keep, rows, 0.0))
    return jnp.concatenate([hi, lo, hi, jnp.zeros_like(hi)], axis=0).astype(BF16)


def _pick_rows(results):
    rowi = _iota2(results[0].shape, 0)
    out = results[0]
    for r in range(1, len(results)):
        out = jnp.where(rowi == r, results[r], out)
    return out


def _step_conv(x_ref, buf_ref, cw_ref, nb_ref):
    cw = cw_ref[...]
    x, b0, b1, b2 = x_ref[...], buf_ref[0], buf_ref[1], buf_ref[2]
    nb_ref[0] = b1
    nb_ref[1] = b2
    nb_ref[2] = x
    return b0 * cw[0:1] + b1 * cw[1:2] + b2 * cw[2:3] + x * cw[3:4]


def _ssd_step_kernel(z_ref, xbc_ref, sm_ref, buf_ref, st_ref, cw_ref, cb_ref, bias_ref, alog_ref, dsk_ref, ng_ref,
                     o_ref, nb_ref, nst_ref):
    rb = STEP_ROWS
    xbc = _silu(_step_conv(xbc_ref, buf_ref, cw_ref, nb_ref) + cb_ref[...])
    xs = xbc[:, :BR_W]
    gs = SSM_GROUPS * SSM_STATE
    bm = xbc[:, BR_W:BR_W + gs]
    cm = xbc[:, BR_W + gs:]
    dt = _softplus(sm_ref[:, :LANE] + bias_ref[:, :LANE])
    d_a = jnp.exp(-jnp.exp(alog_ref[:, :LANE]) * dt)
    lane = _iota2((rb, 2 * SSM_HEAD), 1)
    sub = _iota2((2 * SSM_HEAD, 1), 0)
    pairs_per_group = SSM_HEADS // SSM_GROUPS // 2
    ys = []
    for j in range(SSM_HEADS // 2):
        g = j // pairs_per_group
        h0, h1 = 2 * j, 2 * j + 1
        dt_pair = jnp.where(lane < SSM_HEAD, dt[:, h0:h0 + 1], dt[:, h1:h1 + 1])
        xt = _outer_lhs(xs[:, j * 2 * SSM_HEAD:(j + 1) * 2 * SSM_HEAD] * dt_pair)
        bm_g = bm[:, g * SSM_STATE:(g + 1) * SSM_STATE]
        cm_t = cm[:, g * SSM_STATE:(g + 1) * SSM_STATE].T.astype(BF16)
        coli = _iota2((2 * SSM_HEAD, rb), 1)
        y_t = jnp.zeros((2 * SSM_HEAD, rb), F32)
        for r in range(rb):
            decay = jnp.where(sub < SSM_HEAD, d_a[r:r + 1, h0:h0 + 1], d_a[r:r + 1, h1:h1 + 1])
            h_new = st_ref[r, j] * decay + jnp.dot(xt, _outer_rhs(bm_g, r), preferred_element_type=F32)
            nst_ref[r, j] = h_new
            y_t = jnp.where(coli == r, jnp.dot(h_new.astype(BF16), cm_t, preferred_element_type=F32), y_t)
        ys.append(y_t.T)
    y = (jnp.concatenate(ys, axis=1) + dsk_ref[...] * xs) * _silu(z_ref[...])
    gw = BR_W // SSM_GROUPS
    ng = ng_ref[...]
    o_ref[...] = jnp.concatenate([_rms(y[:, g * gw:(g + 1) * gw], ng[:, g * gw:(g + 1) * gw])
                                  for g in range(SSM_GROUPS)], axis=1)


def _ssd_step(proj, buf, st, layer, cw, cb, bias_row, alog_row, dsk, ng):
    db = proj.shape[0]
    rb = STEP_ROWS
    npair = SSM_HEADS // 2
    sdim = 2 * SSM_HEAD
    return pl.pallas_call(
        _ssd_step_kernel,
        grid=(db // rb,),
        in_specs=[pl.BlockSpec((rb, BR_W), lambda i: (i, P_SZ // BR_W)),
                  pl.BlockSpec((rb, SSM_CONV_CH), lambda i: (i, P_XBC // SSM_CONV_CH)),
                  pl.BlockSpec((rb, SMALL_W), lambda i: (i, P_SMALL // SMALL_W)),
                  pl.BlockSpec((None, CONV_W - 1, rb, SSM_CONV_CH), lambda i: (layer, 0, i, 0)),
                  pl.BlockSpec((None, rb, npair, sdim, SSM_STATE), lambda i: (layer, i, 0, 0, 0)),
                  _full((CONV_W, SSM_CONV_CH)), _full((1, SSM_CONV_CH)), _full((1, SMALL_W)), _full((1, SMALL_W)),
                  _full((1, BR_W)), _full((1, BR_W))],
        out_specs=[pl.BlockSpec((rb, BR_W), lambda i: (i, 0)),
                   pl.BlockSpec((CONV_W - 1, rb, SSM_CONV_CH), lambda i: (0, i, 0)),
                   pl.BlockSpec((rb, npair, sdim, SSM_STATE), lambda i: (i, 0, 0, 0))],
        out_shape=[jax.ShapeDtypeStruct((db, BR_W), F32),
                   jax.ShapeDtypeStruct((CONV_W - 1, db, SSM_CONV_CH), F32),
                   jax.ShapeDtypeStruct((db, npair, sdim, SSM_STATE), F32)],
        compiler_params=_cparams("parallel"),
        name="ssd_step")(proj, proj, proj, buf, st, cw, cb, bias_row, alog_row, dsk, ng)


def _gdn_step_kernel(qkv_ref, z_ref, sm_ref, buf_ref, st_ref, cw_ref, bias_ref, alog_ref, ng_ref,
                     o_ref, nb_ref, nst_ref):
    rb = STEP_ROWS
    hd = GDN_HEAD
    ng = ng_ref[...]
    qkv = _silu(_step_conv(qkv_ref, buf_ref, cw_ref, nb_ref))
    sm = sm_ref[:, :LANE]
    eg_all = jnp.exp(-jnp.exp(alog_ref[:, :LANE]) * _softplus(sm + bias_ref[:, :LANE]))
    beta_all = _sigmoid(sm)
    l2 = lambda t: t * lax.rsqrt(jnp.sum(t * t, axis=-1, keepdims=True) + EPS)
    for h in range(GDN_HEADS):
        q = l2(qkv[:, h * hd:(h + 1) * hd]) * (hd ** -0.5)
        k = l2(qkv[:, BR_W + h * hd:BR_W + (h + 1) * hd])
        v = qkv[:, 2 * BR_W + h * hd:2 * BR_W + (h + 1) * hd]
        eg = eg_all[:, S_GA + h:S_GA + h + 1]
        beta = beta_all[:, S_GB + h:S_GB + h + 1]
        lhs = jnp.concatenate([k * (beta * eg), q * eg], axis=0).astype(BF16)
        res = [jnp.dot(lhs, st_ref[r, h].astype(BF16), preferred_element_type=F32) for r in range(rb)]
        v_new = v * beta - _pick_rows([t[:rb] for t in res])
        o = _pick_rows([t[rb:] for t in res]) + jnp.sum(q * k, axis=1, keepdims=True) * v_new
        kt = _outer_lhs(k)
        for r in range(rb):
            nst_ref[r, h] = (st_ref[r, h] * eg[r:r + 1, :]
                             + jnp.dot(kt, _outer_rhs(v_new, r), preferred_element_type=F32))
        o_ref[:, h * hd:(h + 1) * hd] = _rms(o, ng) * _silu(z_ref[:, h * hd:(h + 1) * hd])


def _gdn_step(proj, buf, st, layer, cw, bias_row, alog_row, ng):
    db = proj.shape[0]
    rb = STEP_ROWS
    return pl.pallas_call(
        _gdn_step_kernel,
        grid=(db // rb,),
        in_specs=[pl.BlockSpec((rb, GDN_CONV_CH), lambda i: (i, P_GQKV // GDN_CONV_CH)),
                  pl.BlockSpec((rb, BR_W), lambda i: (i, P_GZ // BR_W)),
                  pl.BlockSpec((rb, SMALL_W), lambda i: (i, P_SMALL // SMALL_W)),
                  pl.BlockSpec((None, CONV_W - 1, rb, GDN_CONV_CH), lambda i: (layer, 0, i, 0)),
                  pl.BlockSpec((None, rb, GDN_HEADS, GDN_HEAD, GDN_HEAD), lambda i: (layer, i, 0, 0, 0)),
                  _full((CONV_W, GDN_CONV_CH)), _full((1, SMALL_W)), _full((1, SMALL_W)), _full((1, GDN_HEAD))],
        out_specs=[pl.BlockSpec((rb, BR_W), lambda i: (i, 0)),
                   pl.BlockSpec((CONV_W - 1, rb, GDN_CONV_CH), lambda i: (0, i, 0)),
                   pl.BlockSpec((rb, GDN_HEADS, GDN_HEAD, GDN_HEAD), lambda i: (i, 0, 0, 0))],
        out_shape=[jax.ShapeDtypeStruct((db, BR_W), F32),
                   jax.ShapeDtypeStruct((CONV_W - 1, db, GDN_CONV_CH), F32),
                   jax.ShapeDtypeStruct((db, GDN_HEADS, GDN_HEAD, GDN_HEAD), F32)],
        compiler_params=_cparams("parallel"),
        name="gdn_step")(proj, proj, proj, buf, st, cw, bias_row, alog_row, ng)


def _mlstm_step_kernel(q_ref, k_ref, v_ref, og_ref, sm_ref, c_ref, n_ref, m_ref, bias_ref, ng_ref,
                       o_ref, nc_ref, nn_ref, nm_ref):
    rb = STEP_ROWS
    hd = ML_HEAD
    ng = ng_ref[...]
    pre = sm_ref[:, :LANE] + bias_ref[:, :LANE]
    logf_all = -_softplus(-pre)
    m_all = m_ref[...]
    lane4 = _iota2((rb, ML_HEADS), 1)
    m_out = jnp.zeros((rb, ML_HEADS), F32)
    for h in range(ML_HEADS):
        sl = slice(h * hd, (h + 1) * hd)
        q = q_ref[:, sl]
        k = k_ref[:, sl] * (hd ** -0.5)
        v = v_ref[:, sl]
        i_pre = pre[:, S_MI + h:S_MI + h + 1]
        logf = logf_all[:, S_MF + h:S_MF + h + 1]
        m_prev = m_all[:, h:h + 1]
        m_t = jnp.maximum(logf + m_prev, i_pre)
        w_prev = jnp.exp(logf + m_prev - m_t)
        w_j = jnp.exp(i_pre - m_t)
        s = jnp.sum(q * k, axis=1, keepdims=True) * w_j
        n_prev = n_ref[pl.ds(h, rb, stride=ML_HEADS), :]
        qb = q.astype(BF16)
        qc = _pick_rows([jnp.dot(qb, c_ref[r, h].astype(BF16), preferred_element_type=F32) for r in range(rb)])
        num = w_prev * qc + s * v
        den = w_prev * jnp.sum(q * n_prev, axis=1, keepdims=True) + s
        hid = num / jnp.maximum(jnp.abs(den), jnp.exp(-m_t))
        kw = k * w_j
        kt = _outer_lhs(kw)
        for r in range(rb):
            nc_ref[r, h] = (c_ref[r, h] * w_prev[r:r + 1, :]
                            + jnp.dot(kt, _outer_rhs(v, r), preferred_element_type=F32))
        nn_ref[h] = n_prev * w_prev + kw
        m_out = jnp.where(lane4 == h, m_t, m_out)
        o_ref[:, sl] = _rms(hid, ng) * _sigmoid(og_ref[:, sl])
    nm_ref[...] = m_out


def _mlstm_step(proj, c0, n0, m0, layer, bias_row, ng):
    db = proj.shape[0]
    rb = STEP_ROWS
    col = lambda j: pl.BlockSpec((rb, BR_W), lambda i: (i, P_MQ // BR_W + j))
    return pl.pallas_call(
        _mlstm_step_kernel,
        grid=(db // rb,),
        in_specs=[col(0), col(1), col(2), col(3),
                  pl.BlockSpec((rb, SMALL_W), lambda i: (i, P_SMALL // SMALL_W)),
                  pl.BlockSpec((None, rb, ML_HEADS, ML_HEAD, ML_HEAD), lambda i: (layer, i, 0, 0, 0)),
                  pl.BlockSpec((None, rb * ML_HEADS, ML_HEAD), lambda i: (layer, i, 0)),
                  pl.BlockSpec((None, rb, ML_HEADS), lambda i: (layer, i, 0)),
                  _full((1, SMALL_W)), _full((1, ML_HEAD))],
        out_specs=[pl.BlockSpec((rb, BR_W), lambda i: (i, 0)),
                   pl.BlockSpec((rb, ML_HEADS, ML_HEAD, ML_HEAD), lambda i: (i, 0, 0, 0)),
                   pl.BlockSpec((ML_HEADS, rb, ML_HEAD), lambda i: (0, i, 0)),
                   pl.BlockSpec((rb, ML_HEADS), lambda i: (i, 0))],
        out_shape=[jax.ShapeDtypeStruct((db, BR_W), F32),
                   jax.ShapeDtypeStruct((db, ML_HEADS, ML_HEAD, ML_HEAD), F32),
                   jax.ShapeDtypeStruct((ML_HEADS, db, ML_HEAD), F32),
                   jax.ShapeDtypeStruct((db, ML_HEADS), F32)],
        compiler_params=_cparams("parallel"),
        name="mlstm_step")(proj, proj, proj, proj, proj, c0, n0, m0, bias_row, ng)


def _merge_kernel(x_ref, a_ref, b_ref, c_ref, d_ref, g0, g1, g2, g3, wb_ref, wo_ref, o_ref):
    acc = None
    for n, (br, gate) in enumerate(((a_ref, g0), (b_ref, g1), (c_ref, g2), (d_ref, g3))):
        t = _sigmoid(gate[...]) * jnp.dot(br[...].astype(BF16), wb_ref[n], preferred_element_type=F32)
        acc = t if acc is None else acc + t
    o_ref[...] = x_ref[...] + jnp.dot(acc.astype(BF16), wo_ref[...], preferred_element_type=F32)


def _merge(x, branches, proj, wb, wo, tm):
    m = x.shape[0]
    br = pl.BlockSpec((tm, BR_W), lambda i: (i, 0))
    gate = lambda n: pl.BlockSpec((tm, D_MODEL), lambda i: (i, P_GATE // D_MODEL + n))
    return pl.pallas_call(
        _merge_kernel,
        grid=(m // tm,),
        in_specs=[pl.BlockSpec((tm, D_MODEL), lambda i: (i, 0)), br, br, br, br,
                  gate(0), gate(1), gate(2), gate(3),
                  _full((N_BRANCH, BR_W, D_MODEL)), _full((D_MODEL, D_MODEL))],
        out_specs=pl.BlockSpec((tm, D_MODEL), lambda i: (i, 0)),
        out_shape=jax.ShapeDtypeStruct((m, D_MODEL), F32),
        compiler_params=_cparams("parallel"),
        name="merge")(x, *branches, proj, proj, proj, proj, wb, wo)


def _cross_prompt_kernel(x_ref, g_ref, mk_ref, mv_ref, wq_ref, wo_ref, o_ref):
    x = x_ref[...]
    h = _rms(x, g_ref[...]).astype(BF16)
    q = jnp.dot(h, wq_ref[...], preferred_element_type=F32) * (X_HEAD ** -0.5)
    outs = []
    for hd in range(X_HEADS):
        sl = slice(hd * X_HEAD, (hd + 1) * X_HEAD)
        s = _dot_nt(q[:, sl], mk_ref[:, sl])
        p = jnp.exp(s - jnp.max(s, axis=1, keepdims=True))
        p = p / jnp.sum(p, axis=1, keepdims=True)
        outs.append(_dot(p, mv_ref[:, sl]))
    o = jnp.concatenate(outs, axis=1).astype(BF16)
    o_ref[...] = x + jnp.dot(o, wo_ref[...], preferred_element_type=F32)


def _cross_prompt(x, g, mkv, wq, wo, bsz, seq, tq=512):
    nq = seq // tq
    d = D_MODEL
    return pl.pallas_call(
        _cross_prompt_kernel,
        grid=(bsz, nq),
        in_specs=[pl.BlockSpec((tq, d), lambda b, i: (b * nq + i, 0)), _full((1, d)),
                  pl.BlockSpec((N_MEM, d), lambda b, i: (b, 0)),
                  pl.BlockSpec((N_MEM, d), lambda b, i: (b, 1)),
                  _full((d, d)), _full((d, d))],
        out_specs=pl.BlockSpec((tq, d), lambda b, i: (b * nq + i, 0)),
        out_shape=jax.ShapeDtypeStruct((bsz * seq, d), F32),
        compiler_params=_cparams("parallel", "arbitrary"),
        name="cross_prompt")(x, g.reshape(1, d), mkv, mkv, wq, wo)


CROSS_ROWS = 4


def _cross_decode_kernel(q_ref, mk_ref, mv_ref, o_ref):
    halves = X_HEAD // LANE
    rows_per_tok = halves * X_HEADS

    def head_slab(ref, r, h):
        return jnp.concatenate([ref[r, pl.ds(t * X_HEADS + h, N_MEM, stride=rows_per_tok), :].astype(BF16)
                                for t in range(halves)], axis=1)

    pairs = [(r, h) for r in range(CROSS_ROWS) for h in range(X_HEADS)]
    qs = [q_ref[r] * (X_HEAD ** -0.5) for r in range(CROSS_ROWS)]
    scores = [lax.dot_general(jnp.broadcast_to(qs[r][:, h * X_HEAD:(h + 1) * X_HEAD], (8, X_HEAD)).astype(BF16),
                              head_slab(mk_ref, r, h), (((1,), (1,)), ((), ())), preferred_element_type=F32)
              for r, h in pairs]
    probs = []
    for s in scores:
        e = jnp.exp(s - jnp.max(s, axis=1, keepdims=True))
        probs.append((e / jnp.sum(e, axis=1, keepdims=True)).astype(BF16))
    outs = [jnp.dot(p, head_slab(mv_ref, r, h), preferred_element_type=F32)[0:1] for p, (r, h) in zip(probs, pairs)]
    for r in range(CROSS_ROWS):
        o_ref[r] = jnp.concatenate(outs[r * X_HEADS:(r + 1) * X_HEADS], axis=1)


def _mem_rows(mem):
    depth, db = mem.shape[:2]
    halves = X_HEAD // LANE
    m = mem.reshape(depth, db, N_MEM, X_HEADS, halves, LANE).transpose(0, 1, 2, 4, 3, 5)
    return m.reshape(depth, db, N_MEM * halves * X_HEADS, LANE)


def _cross_decode(q3, mem_k, mem_v, layer):
    db = q3.shape[0]
    rb = CROSS_ROWS
    d = D_MODEL
    mem = pl.BlockSpec((None, rb, mem_k.shape[2], LANE), lambda i: (layer, i, 0, 0))
    return pl.pallas_call(
        _cross_decode_kernel,
        grid=(db // rb,),
        in_specs=[pl.BlockSpec((rb, 1, d), lambda i: (i, 0, 0)), mem, mem],
        out_specs=pl.BlockSpec((rb, 1, d), lambda i: (i, 0, 0)),
        out_shape=jax.ShapeDtypeStruct((db, 1, d), F32),
        compiler_params=_cparams("parallel"),
        name="cross_decode")(q3, mem_k, mem_v)


def _swiglu_kernel(x_ref, g_ref, wg_ref, wu_ref, wd_ref, o_ref, h_ref, acc_ref):
    j = pl.program_id(1)

    @pl.when(j == 0)
    def _():
        h_ref[...] = _rms(x_ref[...], g_ref[...]).astype(BF16)
        acc_ref[...] = jnp.zeros(acc_ref.shape, F32)

    h = h_ref[...]
    gate = jnp.dot(h, wg_ref[...], preferred_element_type=F32)
    up = jnp.dot(h, wu_ref[...], preferred_element_type=F32)
    acc_ref[...] += jnp.dot((_silu(gate) * up).astype(BF16), wd_ref[...], preferred_element_type=F32)

    @pl.when(j == pl.num_programs(1) - 1)
    def _():
        o_ref[...] = x_ref[...] + acc_ref[...]


def _swiglu(x, g, wgu, wd, tm, tf=D_FF // 2):
    m, d = x.shape
    nf = D_FF // tf
    return pl.pallas_call(
        _swiglu_kernel,
        grid=(m // tm, nf),
        in_specs=[pl.BlockSpec((tm, d), lambda i, j: (i, 0)), _full((1, d)),
                  pl.BlockSpec((d, tf), lambda i, j: (0, j)),
                  pl.BlockSpec((d, tf), lambda i, j: (0, nf + j)),
                  pl.BlockSpec((tf, d), lambda i, j: (j, 0))],
        out_specs=pl.BlockSpec((tm, d), lambda i, j: (i, 0)),
        out_shape=jax.ShapeDtypeStruct((m, d), F32),
        scratch_shapes=[pltpu.VMEM((tm, d), BF16), pltpu.VMEM((tm, d), F32)],
        compiler_params=_cparams("parallel", "arbitrary"),
        name="swiglu")(x, g.reshape(1, d), wgu, wgu, wd)


def _pack_w_in(w):
    offs = np.cumsum((0,) + IN_SPLITS)
    seg = lambda i: w[:, offs[i]:offs[i + 1]]
    small = jnp.concatenate([seg(5), seg(8), seg(9), seg(14), seg(15),
                             jnp.zeros((w.shape[0], SMALL_W - 24), w.dtype)], axis=1)
    order = (0, 1, 2, 3, 4, 6, 7, 10, 11, 12, 13, 16)
    return jnp.concatenate([seg(i) for i in order] + [small], axis=1).astype(BF16)


def _small_row(parts):
    row = jnp.zeros((SMALL_W,), F32)
    for off, val in parts:
        row = lax.dynamic_update_slice(row, val.astype(F32), (off,))
    return row.reshape(1, SMALL_W)


def _layer_params(l, p):
    lp = dict(
        w_in=_pack_w_in(p["w_in"][l]),
        g_mix=p["g_mix"][l],
        lam=tuple(p[n][l].reshape(1, DA_HEAD) for n in ("da_lq1", "da_lk1", "da_lq2", "da_lk2")),
        lam_init=0.8 - 0.6 * math.exp(-0.3 * l),
        sub_g=p["da_sub_g"][l],
        bias_row=_small_row(((S_DT, p["ssm_dt_bias"][l]), (S_GA, p["gdn_dt_bias"][l]),
                             (S_MI, p["ml_i_bias"][l]), (S_MF, p["ml_f_bias"][l]))),
        alog_row=_small_row(((S_DT, p["ssm_a_log"][l]), (S_GA, p["gdn_a_log"][l]))),
        ssm_cw=p["ssm_conv_w"][l], ssm_cb=p["ssm_conv_b"][l].reshape(1, SSM_CONV_CH),
        ssm_dsk=jnp.repeat(p["ssm_d"][l], SSM_HEAD).reshape(1, BR_W),
        ssm_ng=p["ssm_norm_g"][l].reshape(1, BR_W),
        gdn_cw=p["gdn_conv_w"][l], gdn_ng=p["gdn_norm_g"][l].reshape(1, GDN_HEAD),
        ml_ng=p["ml_norm_g"][l].reshape(1, ML_HEAD),
        w_branch=p["w_branch"][l].astype(BF16), w_out=p["w_out"][l].astype(BF16),
        g_cross=p["g_cross"][l], w_cq=p["w_cq"][l].astype(BF16), w_co=p["w_co"][l].astype(BF16),
        g_ffn=p["g_ffn"][l], w_gu=p["w_gu"][l].astype(BF16), w_down=p["w_down"][l].astype(BF16),
    )
    return lp


def _prompt_layer(x, lp, mkv, bsz, seq):
    rows = bsz * seq
    proj = _norm_matmul(x, lp["g_mix"], lp["w_in"], tm=2048 if rows % 2048 == 0 else rows, tn=1152, name="in_proj")
    o_da = _da_prompt(proj, lp["lam"], lp["sub_g"], bsz, seq, lp["lam_init"])
    o_ssm, ssm = _ssd_prompt(proj, lp["ssm_cw"], lp["ssm_cb"], lp["bias_row"], lp["alog_row"], lp["ssm_dsk"],
                             lp["ssm_ng"], bsz, seq)
    o_gdn, gdn = _gdn_prompt(proj, lp["gdn_cw"], lp["bias_row"], lp["alog_row"], lp["gdn_ng"], bsz, seq)
    o_ml, ml_c, ml_n, ml_m = _mlstm_prompt(proj, lp["bias_row"], lp["ml_ng"], bsz, seq)
    x = _merge(x, (o_da, o_ssm, o_gdn, o_ml), proj, lp["w_branch"], lp["w_out"], tm=256)
    x = _cross_prompt(x, lp["g_cross"], mkv, lp["w_cq"], lp["w_co"], bsz, seq, tq=min(512, seq))
    x = _swiglu(x, lp["g_ffn"], lp["w_gu"], lp["w_down"], tm=1024 if rows % 1024 == 0 else 256)
    p3 = proj.reshape(bsz, seq, PACK_W)
    new = dict(
        k=p3[:, :, P_K:P_K + BR_W].reshape(bsz, seq, DA_HEADS, 2, DA_HEAD),
        v=p3[:, :, P_V:P_V + BR_W].reshape(bsz, seq, DA_HEADS, 2 * DA_HEAD),
        ssm_conv=p3[:, seq - (CONV_W - 1):, P_XBC:P_XBC + SSM_CONV_CH], ssm=ssm,
        gdn_conv=p3[:, seq - (CONV_W - 1):, P_GQKV:P_GQKV + GDN_CONV_CH], gdn=gdn,
        ml_c=ml_c, ml_n=ml_n, ml_m=ml_m[:, :ML_HEADS, 0])
    return x, new


def _sample_layer(x, lp, l, caches, states):
    db = x.shape[0]
    cache_k, cache_v, page_table, mem_k, mem_v = caches
    proj = _norm_matmul(x, lp["g_mix"], lp["w_in"], tm=db, tn=1152, name="in_proj_s")
    proj3 = proj.reshape(db, 1, PACK_W)
    o_da = _da_decode(proj3, lp["lam"], lp["sub_g"], cache_k, cache_v, page_table, l, lp["lam_init"])
    o_ssm, ssm_conv, ssm = _ssd_step(proj, states["ssm_conv"], states["ssm"], l, lp["ssm_cw"], lp["ssm_cb"],
                                     lp["bias_row"], lp["alog_row"], lp["ssm_dsk"], lp["ssm_ng"])
    o_gdn, gdn_conv, gdn = _gdn_step(proj, states["gdn_conv"], states["gdn"], l, lp["gdn_cw"],
                                     lp["bias_row"], lp["alog_row"], lp["gdn_ng"])
    o_ml, ml_c, ml_n, ml_m = _mlstm_step(proj, states["ml_c"], states["ml_n"], states["ml_m"], l,
                                         lp["bias_row"], lp["ml_ng"])
    x = _merge(x, (o_da, o_ssm, o_gdn, o_ml), proj, lp["w_branch"], lp["w_out"], tm=db)
    q = _norm_matmul(x, lp["g_cross"], lp["w_cq"], tm=db, tn=D_MODEL, name="cross_q_s")
    att = _cross_decode(q.reshape(db, 1, D_MODEL), mem_k, mem_v, l)
    x = _matmul_residual(x, att.reshape(db, D_MODEL), lp["w_co"], tm=db, name="cross_o_s")
    x = _swiglu(x, lp["g_ffn"], lp["w_gu"], lp["w_down"], tm=db)
    new = dict(
        k=proj[:, P_K:P_K + BR_W].reshape(db, 1, DA_HEADS, 2, DA_HEAD),
        v=proj[:, P_V:P_V + BR_W].reshape(db, 1, DA_HEADS, 2 * DA_HEAD),
        ssm_conv=ssm_conv.transpose(1, 0, 2), ssm=ssm.reshape(db, SSM_HEADS, SSM_HEAD, SSM_STATE),
        gdn_conv=gdn_conv.transpose(1, 0, 2), gdn=gdn,
        ml_c=ml_c, ml_n=ml_n.transpose(1, 0, 2), ml_m=ml_m)
    return x, new


_STATE_ORDER = ("ssm_conv", "ssm", "gdn_conv", "gdn", "ml_c", "ml_n", "ml_m")


def kernel(x_prompt, x_sample, cache_k, cache_v, cache_mem_k, cache_mem_v, state_ssm_conv, state_ssm, state_gdn_conv, state_gdn, state_mlstm_c, state_mlstm_n, state_mlstm_m, page_table, mem_prompt, g_mix, w_in, da_lq1, da_lk1, da_lq2, da_lk2, da_sub_g, ssm_conv_w, ssm_conv_b, ssm_dt_bias, ssm_a_log, ssm_d, ssm_norm_g, gdn_conv_w, gdn_dt_bias, gdn_a_log, gdn_norm_g, ml_i_bias, ml_f_bias, ml_norm_g, w_branch, w_out, g_cross, g_mem, w_cq, w_ckv, w_co, g_ffn, w_gu, w_down, g_final):
    p = dict(g_mix=g_mix, w_in=w_in, da_lq1=da_lq1, da_lk1=da_lk1, da_lq2=da_lq2, da_lk2=da_lk2,
             da_sub_g=da_sub_g, ssm_conv_w=ssm_conv_w, ssm_conv_b=ssm_conv_b, ssm_dt_bias=ssm_dt_bias,
             ssm_a_log=ssm_a_log, ssm_d=ssm_d, ssm_norm_g=ssm_norm_g, gdn_conv_w=gdn_conv_w,
             gdn_dt_bias=gdn_dt_bias, gdn_a_log=gdn_a_log, gdn_norm_g=gdn_norm_g,
             ml_i_bias=ml_i_bias, ml_f_bias=ml_f_bias, ml_norm_g=ml_norm_g,
             w_branch=w_branch, w_out=w_out, g_cross=g_cross, w_cq=w_cq, w_co=w_co,
             g_ffn=g_ffn, w_gu=w_gu, w_down=w_down)
    depth = w_in.shape[0]
    bsz, seq, d = x_prompt.shape
    db = x_sample.shape[0]
    n_mem = mem_prompt.shape[1]
    lps = [_layer_params(l, p) for l in range(depth)]

    mem2 = mem_prompt.reshape(bsz * n_mem, d)
    xp = x_prompt.reshape(bsz * seq, d)
    p_new = {n: [] for n in ("k", "v", "mem_k", "mem_v") + _STATE_ORDER}
    for l in range(depth):
        mkv = _norm_matmul(mem2, g_mem[l], w_ckv[l].astype(BF16), tm=min(1024, bsz * n_mem), tn=1024, name="mem_kv")
        xp, new = _prompt_layer(xp, lps[l], mkv, bsz, seq)
        mkv5 = mkv.reshape(bsz, n_mem, 2, X_HEADS, X_HEAD)
        new["mem_k"] = mkv5[:, :, 0]
        new["mem_v"] = mkv5[:, :, 1]
        for n in p_new:
            p_new[n].append(new[n])
    y_prompt = _final_norm(xp, g_final, tm=512 if (bsz * seq) % 512 == 0 else bsz * seq, name="final_norm").reshape(bsz, seq, d)

    n_pool = cache_k.shape[1]
    caches = (cache_k.transpose(0, 1, 3, 4, 5, 2).reshape(depth, n_pool, BR_W, PAGE_SIZE),
              cache_v.reshape(depth, n_pool, PAGE_SIZE * DA_HEADS, 2 * DA_HEAD),
              page_table, _mem_rows(cache_mem_k), _mem_rows(cache_mem_v))
    states = dict(ssm_conv=state_ssm_conv.transpose(0, 2, 1, 3),
                  ssm=state_ssm.reshape(depth, db, SSM_HEADS // 2, 2 * SSM_HEAD, SSM_STATE),
                  gdn_conv=state_gdn_conv.transpose(0, 2, 1, 3), gdn=state_gdn,
                  ml_c=state_mlstm_c, ml_n=state_mlstm_n.reshape(depth, db * ML_HEADS, ML_HEAD), ml_m=state_mlstm_m)
    xs = x_sample.reshape(db, d)
    s_new = {n: [] for n in ("k", "v") + _STATE_ORDER}
    for l in range(depth):
        xs, new = _sample_layer(xs, lps[l], l, caches, states)
        for n in s_new:
            s_new[n].append(new[n])
    y_sample = _final_norm(xs, g_final, tm=db, name="final_norm_s").reshape(db, 1, d)

    stk = lambda dct, n: jnp.stack(dct[n])
    return (y_prompt, y_sample,
            stk(p_new, "k"), stk(p_new, "v"), stk(p_new, "mem_k"), stk(p_new, "mem_v"),
            *(stk(p_new, n) for n in _STATE_ORDER),
            stk(s_new, "k"), stk(s_new, "v"), *(stk(s_new, n) for n in _STATE_ORDER))
```
